```python
import math
import jax, jax.numpy as jnp
from jax import lax
import numpy as np


D_MODEL = 1024
BATCH = 4
SEQ = 4096
DEPTH = 4

BLOCK_Q = 128
EPS = 1e-6
SB_HEADS = 8
SB_HEAD_DIM = 64
MLA_HEADS = 8
MLA_Q_LORA = 256
MLA_KV_LORA = 128
MLA_NOPE = 64
MLA_ROPE = 32
MLA_V = 64
ROPE_THETA = 10000.0
DSA_HEADS = 8
DSA_HEAD_DIM = 64
IDX_HEADS = 8
IDX_DIM = 32
TOPK_MAX = 256
REL_BUCKETS = 32
REL_MAX_DIST = 128
N_BRANCH = 3
BRANCH_WIDTH = 512
IN_SIZES = (SB_HEADS * SB_HEAD_DIM, SB_HEADS * SB_HEAD_DIM, SB_HEADS * SB_HEAD_DIM,
            MLA_Q_LORA, MLA_KV_LORA, MLA_ROPE,
            DSA_HEADS * DSA_HEAD_DIM, DSA_HEAD_DIM, DSA_HEAD_DIM,
            IDX_HEADS * IDX_DIM, IDX_DIM, IDX_HEADS,
            BRANCH_WIDTH, BRANCH_WIDTH, BRANCH_WIDTH,
            N_BRANCH * D_MODEL)
D_IN = sum(IN_SIZES)

kernel_name = 'hybrid_sb_mla_dsa_gated_block'


def _split_points(sizes):
    pts, acc = [], 0
    for s in sizes[:-1]:
        acc += s
        pts.append(acc)
    return pts


def rms_norm(x, g):
    xf = x.astype(jnp.float32)
    y = xf * lax.rsqrt(jnp.mean(xf * xf, axis=-1, keepdims=True) + EPS)
    return (y * g.astype(jnp.float32)).astype(x.dtype)


def apply_rope(x, cos, sin):
    xf = x.astype(jnp.float32)
    x1, x2 = xf[..., : MLA_ROPE // 2], xf[..., MLA_ROPE // 2:]
    return jnp.concatenate([x1 * cos - x2 * sin, x2 * cos + x1 * sin], axis=-1).astype(x.dtype)


def t5_bucket(dist):
    n = jnp.maximum(dist, 0)
    exact = REL_BUCKETS // 2
    nf = jnp.maximum(n, 1).astype(jnp.float32)
    large = exact + (jnp.log(nf / exact) / math.log(REL_MAX_DIST / exact)
                     * (REL_BUCKETS - exact)).astype(jnp.int32)
    large = jnp.minimum(large, REL_BUCKETS - 1)
    return jnp.where(n < exact, n, large)


def stick_breaking_attention(q, k, v):
    B, S, H, dh = q.shape
    kf, vf = k.astype(jnp.float32), v.astype(jnp.float32)
    key_idx = jnp.arange(S)

    def block(n):
        start = n * BLOCK_Q
        qb = lax.dynamic_slice_in_dim(q, start, BLOCK_Q, axis=1).astype(jnp.float32)
        z = jnp.einsum('bqhd,bkhd->bhqk', qb, kf) * (dh ** -0.5)
        q_idx = start + jnp.arange(BLOCK_Q)
        mask = key_idx[None, :] < q_idx[:, None]
        log_1m = jnp.where(mask, jax.nn.log_sigmoid(-z), 0.0)
        csum = jnp.cumsum(log_1m, axis=-1)
        log_a = jax.nn.log_sigmoid(z) + csum[..., -1:] - csum
        a = jnp.where(mask, jnp.exp(log_a), 0.0)
        return jnp.einsum('bhqk,bkhd->bqhd', a, vf)

    out = lax.map(block, jnp.arange(S // BLOCK_Q))
    return out.transpose(1, 0, 2, 3, 4).reshape(B, S, H * dh).astype(q.dtype)


def causal_softmax_attention(q, k, v, scale):
    B, S, H, _ = q.shape
    dv = v.shape[-1]
    kf, vf = k.astype(jnp.float32), v.astype(jnp.float32)
    key_idx = jnp.arange(S)

    def block(n):
        start = n * BLOCK_Q
        qb = lax.dynamic_slice_in_dim(q, start, BLOCK_Q, axis=1).astype(jnp.float32)
        logits = jnp.einsum('bqhd,bkhd->bhqk', qb, kf) * scale
        q_idx = start + jnp.arange(BLOCK_Q)
        mask = key_idx[None, :] <= q_idx[:, None]
        p = jax.nn.softmax(jnp.where(mask, logits, -jnp.inf), axis=-1)
        return jnp.einsum('bhqk,bkhd->bqhd', p, vf)

    out = lax.map(block, jnp.arange(S // BLOCK_Q))
    return out.transpose(1, 0, 2, 3, 4).reshape(B, S, H * dv).astype(q.dtype)


_gather_rows = jax.vmap(lambda a, i: a[i])


def dsa_attention(q, k, v, ix_q, ix_k, ix_w, positions, rel_bias):
    B, S, H, dh = q.shape
    topk = min(TOPK_MAX, S // 4)
    ixk = ix_k.astype(jnp.float32)
    key_idx = jnp.arange(S)

    def block(n):
        start = n * BLOCK_Q
        q_idx = start + jnp.arange(BLOCK_Q)
        iq = lax.dynamic_slice_in_dim(ix_q, start, BLOCK_Q, axis=1).astype(jnp.float32)
        iw = lax.dynamic_slice_in_dim(ix_w, start, BLOCK_Q, axis=1).astype(jnp.float32)
        s_idx = jax.nn.relu(jnp.einsum('bqhd,bkd->bqhk', iq, ixk) * (IDX_DIM ** -0.5))
        score = jnp.einsum('bqh,bqhk->bqk', iw * (IDX_HEADS ** -0.5), s_idx)
        score = jnp.where(key_idx[None, None, :] <= q_idx[None, :, None], score, -jnp.inf)
        _, sel = lax.top_k(score, topk)
        valid = sel <= q_idx[None, :, None]
        kg = _gather_rows(k, sel).astype(jnp.float32)
        vg = _gather_rows(v, sel).astype(jnp.float32)
        qb = lax.dynamic_slice_in_dim(q, start, BLOCK_Q, axis=1).astype(jnp.float32)
        logits = jnp.einsum('bqhd,bqkd->bqhk', qb, kg) * (dh ** -0.5)
        pq = lax.dynamic_slice_in_dim(positions, start, BLOCK_Q, axis=1)
        pk = _gather_rows(positions, sel)
        bias = rel_bias.astype(jnp.float32)[t5_bucket(pq[:, :, None] - pk)]
        logits = logits + jnp.moveaxis(bias, -1, 2)
        p = jax.nn.softmax(jnp.where(valid[:, :, None, :], logits, -jnp.inf), axis=-1)
        return jnp.einsum('bqhk,bqkd->bqhd', p, vg)

    out = lax.map(block, jnp.arange(S // BLOCK_Q))
    return out.transpose(1, 0, 2, 3, 4).reshape(B, S, H * dh).astype(q.dtype)


def hybrid_layer(x, positions, cos, sin, norm_g, w_in, q_norm_g, kv_norm_g, w_uq, w_ukv,
                 mla_q_g, mla_k_g, dsa_q_g, dsa_k_g, rel_bias, gate_b, w_branch, w_out):
    B, S, D = x.shape
    h = rms_norm(x, norm_g)
    proj = h @ w_in
    (sa_q, sa_k, sa_v, c_q, c_kv, k_pe, ds_q, ds_k, ds_v, ix_q, ix_k, ix_w,
     z_a, z_b, z_c, g) = jnp.split(proj, _split_points(IN_SIZES), axis=-1)

    y_a = stick_breaking_attention(sa_q.reshape(B, S, SB_HEADS, SB_HEAD_DIM),
                                   sa_k.reshape(B, S, SB_HEADS, SB_HEAD_DIM),
                                   sa_v.reshape(B, S, SB_HEADS, SB_HEAD_DIM))

    q_b = (rms_norm(c_q, q_norm_g) @ w_uq).reshape(B, S, MLA_HEADS, MLA_NOPE + MLA_ROPE)
    kv_b = (rms_norm(c_kv, kv_norm_g) @ w_ukv).reshape(B, S, MLA_HEADS, MLA_NOPE + MLA_V)
    k_nope, v_b = kv_b[..., :MLA_NOPE], kv_b[..., MLA_NOPE:]
    k_b = jnp.concatenate([k_nope, jnp.broadcast_to(k_pe[:, :, None, :], (B, S, MLA_HEADS, MLA_ROPE))], axis=-1)
    q_b = rms_norm(q_b, mla_q_g)
    k_b = rms_norm(k_b, mla_k_g)
    c4, s4 = cos[:, :, None, :], sin[:, :, None, :]
    q_b = jnp.concatenate([q_b[..., :MLA_NOPE], apply_rope(q_b[..., MLA_NOPE:], c4, s4)], axis=-1)
    k_b = jnp.concatenate([k_b[..., :MLA_NOPE], apply_rope(k_b[..., MLA_NOPE:], c4, s4)], axis=-1)
    y_b = causal_softmax_attention(q_b, k_b, v_b, (MLA_NOPE + MLA_ROPE) ** -0.5)

    y_c = dsa_attention(rms_norm(ds_q.reshape(B, S, DSA_HEADS, DSA_HEAD_DIM), dsa_q_g),
                        rms_norm(ds_k, dsa_k_g), ds_v,
                        ix_q.reshape(B, S, IDX_HEADS, IDX_DIM), ix_k, ix_w,
                        positions, rel_bias)

    branches = jnp.stack([y_a * jax.nn.silu(z_a), y_b * jax.nn.silu(z_b), y_c * jax.nn.silu(z_c)], axis=2)
    up = jnp.einsum('bsnc,ncd->bsnd', branches, w_branch)
    gates = jax.nn.sigmoid(g.reshape(B, S, N_BRANCH, D) + gate_b)
    merged = jnp.sum(gates * up, axis=2)
    return x + merged @ w_out


def setup_inputs(seed: int = 0) -> dict:
    key = jax.random.key(seed)
    ks = jax.random.split(key, 17)
    f32 = jnp.float32

    def nrm(k, shape, scale):
        return jax.random.normal(k, shape, f32) * scale

    def gain(k, shape):
        return 1.0 + 0.02 * jax.random.normal(k, shape, f32)

    x = jax.random.normal(ks[0], (BATCH, SEQ, D_MODEL), f32)
    offset = jax.random.randint(ks[1], (BATCH, 1), 0, 1024, dtype=jnp.int32)
    positions = offset + jnp.arange(SEQ, dtype=jnp.int32)[None, :]
    return {
        'x': x,
        'positions': positions,
        'norm_g': gain(ks[2], (DEPTH, D_MODEL)),
        'w_in': nrm(ks[3], (DEPTH, D_MODEL, D_IN), D_MODEL ** -0.5),
        'mla_q_norm_g': gain(ks[4], (DEPTH, MLA_Q_LORA)),
        'mla_kv_norm_g': gain(ks[5], (DEPTH, MLA_KV_LORA)),
        'mla_w_uq': nrm(ks[6], (DEPTH, MLA_Q_LORA, MLA_HEADS * (MLA_NOPE + MLA_ROPE)), MLA_Q_LORA ** -0.5),
        'mla_w_ukv': nrm(ks[7], (DEPTH, MLA_KV_LORA, MLA_HEADS * (MLA_NOPE + MLA_V)), MLA_KV_LORA ** -0.5),
        'mla_q_g': gain(ks[8], (DEPTH, MLA_NOPE + MLA_ROPE)),
        'mla_k_g': gain(ks[9], (DEPTH, MLA_NOPE + MLA_ROPE)),
        'dsa_q_g': gain(ks[10], (DEPTH, DSA_HEAD_DIM)),
        'dsa_k_g': gain(ks[11], (DEPTH, DSA_HEAD_DIM)),
        'rel_bias': nrm(ks[12], (REL_BUCKETS, DSA_HEADS), 0.5),
        'gate_b': nrm(ks[13], (DEPTH, N_BRANCH, D_MODEL), 0.02),
        'w_branch': nrm(ks[14], (DEPTH, N_BRANCH, BRANCH_WIDTH, D_MODEL), BRANCH_WIDTH ** -0.5),
        'w_out': nrm(ks[15], (DEPTH, D_MODEL, D_MODEL), D_MODEL ** -0.5),
    }


def reference(x, positions, norm_g, w_in, mla_q_norm_g, mla_kv_norm_g, mla_w_uq, mla_w_ukv,
              mla_q_g, mla_k_g, dsa_q_g, dsa_k_g, rel_bias, gate_b, w_branch, w_out):
    half = MLA_ROPE // 2
    inv_freq = ROPE_THETA ** (-(jnp.arange(half, dtype=jnp.float32) * 2.0) / MLA_ROPE)
    ang = positions.astype(jnp.float32)[..., None] * inv_freq
    cos, sin = jnp.cos(ang), jnp.sin(ang)
    for l in range(DEPTH):
        x = hybrid_layer(x, positions, cos, sin, norm_g[l], w_in[l], mla_q_norm_g[l], mla_kv_norm_g[l],
                         mla_w_uq[l], mla_w_ukv[l], mla_q_g[l], mla_k_g[l], dsa_q_g[l], dsa_k_g[l],
                         rel_bias, gate_b[l], w_branch[l], w_out[l])
    return x
```

```python
import functools
import math

import numpy as np
import jax
import jax.numpy as jnp
from jax import lax
from jax.experimental import pallas as pl
from jax.experimental.pallas import tpu as pltpu

F32 = jnp.float32
BF16 = jnp.bfloat16
I32 = jnp.int32

LANES = 128
EPS = 1e-6
D_MODEL = 1024
N_HEADS = 8
HEAD_DIM = 64
MLA_Q_LORA = 256
MLA_KV_LORA = 128
MLA_NOPE = 64
MLA_ROPE = 32
MLA_QK = MLA_NOPE + MLA_ROPE
ROPE_THETA = 10000.0
IDX_DIM = 32
TOPK_MAX = 256
REL_BUCKETS = 32
REL_MAX_DIST = 128
N_BRANCH = 3
BRANCH_WIDTH = 512

O_SAQ, O_SAK, O_SAV = 0, 512, 1024
O_CQ, O_CKV, O_KPE = 1536, 1792, 1920
O_DSQ, O_DSK, O_DSV = 1952, 2464, 2528
O_IXQ, O_IXK, O_IXW = 2592, 2848, 2880
O_Z, O_G = 2888, 4424
D_IN = 7496

NEG_BIG = -1e30
INT_MIN = -(2 ** 31)
SB_UNDERFLOW = 104.0
VMEM_LIMIT = 48 * 1024 * 1024

NT_DIMS = (((1,), (1,)), ((), ()))


def _bucket_thresholds():
    exact = REL_BUCKETS // 2
    n = np.arange(0, 4 * REL_MAX_DIST)
    nf = np.maximum(n, 1).astype(np.float64)
    large = exact + (np.log(nf / exact) / math.log(REL_MAX_DIST / exact) * (REL_BUCKETS - exact)).astype(np.int64)
    bucket = np.where(n < exact, n, np.minimum(large, REL_BUCKETS - 1))
    thr = [int(np.argmax(bucket >= b)) for b in range(1, REL_BUCKETS)]
    assert bucket[-1] == REL_BUCKETS - 1 and all(bucket[t] == b + 1 for b, t in enumerate(thr))
    return thr


BUCKET_THR = _bucket_thresholds()
FAR_DIST = BUCKET_THR[-1]


def _assemble(w, pieces):
    cols = []
    for start, width, scale in pieces:
        if start is None:
            cols.append(jnp.zeros(w.shape[:-1] + (width,), w.dtype))
        else:
            c = w[..., start:start + width]
            cols.append(c if scale == 1.0 else c * scale)
    return jnp.concatenate(cols, axis=-1)


def _head_padded(start, scale=1.0):
    pieces = []
    for h in range(N_HEADS):
        col = (start + h * HEAD_DIM, HEAD_DIM, scale)
        pad = (None, HEAD_DIM, 1.0)
        pieces += [col, pad] if h % 2 == 0 else [pad, col]
    return pieces


def _in_proj_layouts():
    bf = (_head_padded(O_SAQ, HEAD_DIM ** -0.5)
          + [(O_SAK, 512, 1.0), (O_SAV, 512, 1.0), (O_IXQ, 256, 1.0)]
          + [(O_DSV, 64, 1.0)] * 2
          + [(O_IXK, 32, 1.0)] * 4
          + [(O_Z, 1536, 1.0), (O_G, 3072, 1.0)])
    f32 = (_head_padded(O_DSQ)
           + [(O_CQ, 256, 1.0), (O_CKV, 128, 1.0)]
           + [(None, 64, 1.0), (O_KPE, 32, 1.0), (None, 32, 1.0)]
           + [(None, 64, 1.0), (O_KPE + 16, 16, 1.0), (O_KPE, 16, 1.0), (None, 32, 1.0)]
           + [(O_DSK, 64, 1.0)] * 2
           + [(O_IXW, 8, 1.0), (None, 120, 1.0)]
           + [(None, 128, 1.0)])
    return bf, f32


B_SAQ, B_SAK, B_SAV, B_IXQ, B_DSV, B_IXK, B_Z, B_G, N_BF = 0, 1024, 1536, 2048, 2304, 2432, 2560, 4096, 7168
F_DSQ, F_CQ, F_CKV, F_KPE, F_KPER, F_DSK, F_IXW, N_F32 = 0, 1024, 1280, 1408, 1536, 1664, 1792, 2048


def _in_proj_kernel(x_ref, g_ref, w_ref, o_ref, h_ref):
    @pl.when(pl.program_id(1) == 0)
    def _():
        x = x_ref[...]
        ms = jnp.mean(x * x, axis=-1, keepdims=True)
        h_ref[...] = (x * lax.rsqrt(ms + EPS) * g_ref[...]).astype(BF16)

    o_ref[...] = jnp.dot(h_ref[...], w_ref[...], preferred_element_type=F32).astype(o_ref.dtype)


def _in_proj(x2, g, w, out_dtype, tm, tn):
    m, d = x2.shape
    n = w.shape[1]
    return pl.pallas_call(
        _in_proj_kernel,
        grid=(m // tm, n // tn),
        in_specs=[pl.BlockSpec((tm, d), lambda i, j: (i, 0)),
                  pl.BlockSpec((1, d), lambda i, j: (0, 0)),
                  pl.BlockSpec((d, tn), lambda i, j: (0, j))],
        out_specs=pl.BlockSpec((tm, tn), lambda i, j: (i, j)),
        out_shape=jax.ShapeDtypeStruct((m, n), out_dtype),
        scratch_shapes=[pltpu.VMEM((tm, d), BF16)],
        compiler_params=pltpu.CompilerParams(dimension_semantics=("parallel", "arbitrary"),
                                             vmem_limit_bytes=VMEM_LIMIT),
        name="in_proj",
    )(x2, g, w)


def _rms(x, width):
    return lax.rsqrt(jnp.sum(x * x, axis=-1, keepdims=True) * (1.0 / width) + EPS)


def _prologue_kernel(dsq_ref, cq_ref, ckv_ref, kpe_ref, kper_ref, dsk_ref, cos_ref, sin_ref,
                     wuq_ref, wuqr_ref, wuk_ref, wuv_ref,
                     qng_ref, kvng_ref, gq_ref, gqr_ref, gk_ref, gkr_ref, gdq_ref, gdk_ref,
                     qb_ref, kb_ref, vb_ref, dq_ref, dkk_ref):
    cos = cos_ref[...]
    sin = sin_ref[...]

    cq = cq_ref[...]
    cqn = (cq * _rms(cq, MLA_Q_LORA) * qng_ref[...]).astype(BF16)
    q = jnp.dot(cqn, wuq_ref[...], preferred_element_type=F32)
    qr = jnp.dot(cqn, wuqr_ref[...], preferred_element_type=F32)
    cg = cos * gq_ref[...]
    sg = sin * gqr_ref[...]
    for h in range(N_HEADS):
        sl = slice(h * LANES, (h + 1) * LANES)
        qh = q[:, sl]
        qb_ref[:, sl] = ((qh * cg + qr[:, sl] * sg) * _rms(qh, MLA_QK)).astype(BF16)

    ckv = ckv_ref[...]
    ckvn = (ckv * _rms(ckv, MLA_KV_LORA) * kvng_ref[...]).astype(BF16)
    kn = jnp.dot(ckvn, wuk_ref[...], preferred_element_type=F32)
    vb_ref[...] = jnp.dot(ckvn, wuv_ref[...], preferred_element_type=F32).astype(BF16)
    kpe = kpe_ref[...]
    cgk = cos * gk_ref[...]
    rot = kper_ref[...] * (sin * gkr_ref[...])
    for h in range(N_HEADS):
        sl = slice(h * LANES, (h + 1) * LANES)
        kf = kn[:, sl] + kpe
        kb_ref[:, sl] = ((kf * cgk + rot) * _rms(kf, MLA_QK)).astype(BF16)

    for h in range(N_HEADS):
        sl = slice(h * LANES, (h + 1) * LANES)
        xq = dsq_ref[:, sl]
        dq_ref[:, sl] = (xq * _rms(xq, HEAD_DIM) * gdq_ref[:, sl]).astype(BF16)
    xk = dsk_ref[...]
    dkk_ref[...] = (xk * _rms(xk, 2 * HEAD_DIM) * gdk_ref[...]).astype(BF16)


def _prologue(pf, cos_t, sin_t, lw, tm):
    m = pf.shape[0]

    def act(width, off):
        return pl.BlockSpec((tm, width), lambda i, _b=off // width: (i, _b))

    def whole(a):
        return pl.BlockSpec(a.shape, lambda i: (0,) * a.ndim)

    weights = [lw[k] for k in ("wuq", "wuqr", "wuk", "wuv", "qng", "kvng", "gq", "gqr", "gk", "gkr", "gdq", "gdk")]
    row = lambda width: pl.BlockSpec((tm, width), lambda i: (i, 0))
    return pl.pallas_call(
        _prologue_kernel,
        grid=(m // tm,),
        in_specs=[act(1024, F_DSQ), act(256, F_CQ), act(128, F_CKV), act(128, F_KPE), act(128, F_KPER),
                  act(128, F_DSK), row(LANES), row(LANES)] + [whole(w) for w in weights],
        out_specs=[row(1024), row(1024), row(512), row(1024), row(LANES)],
        out_shape=[jax.ShapeDtypeStruct((m, 1024), BF16), jax.ShapeDtypeStruct((m, 1024), BF16),
                   jax.ShapeDtypeStruct((m, 512), BF16), jax.ShapeDtypeStruct((m, 1024), BF16),
                   jax.ShapeDtypeStruct((m, LANES), BF16)],
        compiler_params=pltpu.CompilerParams(dimension_semantics=("parallel",), vmem_limit_bytes=VMEM_LIMIT),
        name="prologue",
    )(pf, pf, pf, pf, pf, pf, cos_t, sin_t, *weights)


def _sb_kernel(q_ref, k_ref, v_ref, o_ref, acc_ref, carry_ref, *, tq, tk):
    assert tk == LANES
    qi = pl.program_id(2)
    tri = (lax.broadcasted_iota(I32, (tk, tk), 0) > lax.broadcasted_iota(I32, (tk, tk), 1)).astype(BF16)
    q_idx = qi * tq + lax.broadcasted_iota(I32, (tq, tk), 0)
    k_lane = lax.broadcasted_iota(I32, (tq, tk), 1)
    kb_first = ((qi + 1) * tq) // tk - 1

    for h in range(2):
        q = q_ref[0, :, h * LANES:(h + 1) * LANES]
        acc_ref[h] = jnp.zeros((tq, LANES), F32)
        carry_ref[...] = jnp.zeros((tq, LANES), F32)

        def cond(c):
            kb, cmax = c
            return jnp.logical_and(kb >= 0, cmax > -SB_UNDERFLOW)

        def body(c, q=q, h=h):
            kb, _ = c
            start = pl.multiple_of(kb * tk, tk)
            k = k_ref[0, pl.ds(start, tk), :]
            v = v_ref[0, pl.ds(start, tk), :]
            z = lax.dot_general(q, k, NT_DIMS, preferred_element_type=F32)
            sp = jnp.maximum(z, 0.0) + jnp.log(1.0 + jnp.exp(-jnp.abs(z)))
            mask = (start + k_lane) < q_idx
            l1m = jnp.where(mask, -sp, 0.0)
            hi = l1m.astype(BF16)
            lo = (l1m - hi.astype(F32)).astype(BF16)
            suffix = (jnp.dot(hi, tri, preferred_element_type=F32)
                      + jnp.dot(lo, tri, preferred_element_type=F32))
            carry = carry_ref[...]
            log_a = (z - sp) + carry + suffix
            a = jnp.where(mask, jnp.exp(log_a), 0.0)
            acc_ref[h] += jnp.dot(a.astype(BF16), v, preferred_element_type=F32)
            carry = carry + jnp.sum(l1m, axis=1, keepdims=True)
            carry_ref[...] = carry
            return kb - 1, jnp.max(carry)

        lax.while_loop(cond, body, (kb_first, jnp.float32(0.0)))

    lane = lax.broadcasted_iota(I32, (tq, LANES), 1)
    o_ref[0] = jnp.where(lane < HEAD_DIM, acc_ref[0], acc_ref[1])


def _sb_attention(pb, b, s, tq, tk):
    pb3 = pb.reshape(b, s, N_BF)
    kernel = functools.partial(_sb_kernel, tq=tq, tk=tk)
    return pl.pallas_call(
        kernel,
        grid=(b, N_HEADS // 2, s // tq),
        in_specs=[pl.BlockSpec((1, tq, 2 * LANES), lambda bi, p, i: (bi, i, B_SAQ // 256 + p)),
                  pl.BlockSpec((1, s, LANES), lambda bi, p, i: (bi, 0, B_SAK // LANES + p)),
                  pl.BlockSpec((1, s, LANES), lambda bi, p, i: (bi, 0, B_SAV // LANES + p))],
        out_specs=pl.BlockSpec((1, tq, LANES), lambda bi, p, i: (bi, i, p)),
        out_shape=jax.ShapeDtypeStruct((b, s, BRANCH_WIDTH), F32),
        scratch_shapes=[pltpu.VMEM((2, tq, LANES), F32), pltpu.VMEM((tq, LANES), F32)],
        compiler_params=pltpu.CompilerParams(dimension_semantics=("parallel", "parallel", "arbitrary"),
                                             vmem_limit_bytes=VMEM_LIMIT),
        name="sb_attention",
    )(pb3, pb3, pb3)


def _mla_kernel(q_ref, k_ref, v_ref, o_ref, m_ref, l_ref, acc_ref, *, t):
    qi = pl.program_id(2)
    row = lax.broadcasted_iota(I32, (t, t), 0)
    col = lax.broadcasted_iota(I32, (t, t), 1)

    for h in range(2):
        q = q_ref[0, :, h * LANES:(h + 1) * LANES]
        m_ref[...] = jnp.full((t, LANES), NEG_BIG, F32)
        l_ref[...] = jnp.zeros((t, LANES), F32)
        acc_ref[h] = jnp.zeros((t, LANES), F32)

        def step(kb, diagonal, q=q, h=h):
            start = pl.multiple_of(kb * t, t)
            k = k_ref[0, pl.ds(start, t), h * LANES:(h + 1) * LANES]
            v = v_ref[0, pl.ds(start, t), :]
            s = lax.dot_general(q, k, NT_DIMS, preferred_element_type=F32)
            if diagonal:
                s = jnp.where(col <= row, s, NEG_BIG)
            m_old = m_ref[...]
            m_new = jnp.maximum(m_old, jnp.max(s, axis=1, keepdims=True))
            p = jnp.exp(s - jnp.concatenate([m_new] * (t // LANES), axis=1))
            alpha = jnp.exp(m_old - m_new)
            l_ref[...] = alpha * l_ref[...] + jnp.sum(p, axis=1, keepdims=True)
            acc_ref[h] = alpha * acc_ref[h] + jnp.dot(p.astype(BF16), v, preferred_element_type=F32)
            m_ref[...] = m_new

        def body(kb, carry):
            step(kb, False)
            return carry

        lax.fori_loop(0, qi, body, 0)
        step(qi, True)
        acc_ref[h] = acc_ref[h] / l_ref[...]

    lane = lax.broadcasted_iota(I32, (t, LANES), 1)
    o_ref[0] = jnp.where(lane < HEAD_DIM, acc_ref[0], acc_ref[1])


def _mla_attention(qb, kb, vb, b, s, t):
    kernel = functools.partial(_mla_kernel, t=t)
    return pl.pallas_call(
        kernel,
        grid=(b, N_HEADS // 2, s // t),
        in_specs=[pl.BlockSpec((1, t, 2 * LANES), lambda bi, p, i: (bi, i, p)),
                  pl.BlockSpec((1, s, 2 * LANES), lambda bi, p, i: (bi, 0, p)),
                  pl.BlockSpec((1, s, LANES), lambda bi, p, i: (bi, 0, p))],
        out_specs=pl.BlockSpec((1, t, LANES), lambda bi, p, i: (bi, i, p)),
        out_shape=jax.ShapeDtypeStruct((b, s, BRANCH_WIDTH), F32),
        scratch_shapes=[pltpu.VMEM((t, LANES), F32), pltpu.VMEM((t, LANES), F32),
                        pltpu.VMEM((2, t, LANES), F32)],
        compiler_params=pltpu.CompilerParams(dimension_semantics=("parallel", "parallel", "arbitrary"),
                                             vmem_limit_bytes=VMEM_LIMIT),
        name="mla_attention",
    )(qb.reshape(b, s, -1), kb.reshape(b, s, -1), vb.reshape(b, s, -1))


def _sortable(x):
    bits = pltpu.bitcast(x + 0.0, I32)
    return bits ^ ((bits >> 31) & 0x7FFFFFFF)


def _dsa_kernel(rel_ref, ixq_ref, ixw_ref, ixk_ref, q_ref, kk_ref, vv_ref, pq_ref, pk_ref, o_ref,
                keys_ref, qm_ref, wb_ref, bias_ref, m_ref, l_ref, acc_ref, cnt_ref, thr_ref, *, tq, topk, seq):
    t = LANES
    qi = pl.program_id(1)
    n_tiles = ((qi + 1) * tq) // t
    lane = lax.broadcasted_iota(I32, (tq, LANES), 1)
    q_idx = qi * tq + lax.broadcasted_iota(I32, (tq, t), 0)
    k_lane = lax.broadcasted_iota(I32, (tq, t), 1)

    w_scale = (IDX_DIM ** -0.5) * (N_HEADS ** -0.5)
    for h in range(N_HEADS):
        grp = ixq_ref[0, :, (h // 4) * LANES:(h // 4 + 1) * LANES]
        lo = (h % 4) * IDX_DIM
        qm_ref[h] = jnp.where((lane >= lo) & (lane < lo + IDX_DIM), grp, jnp.zeros_like(grp))
        wb_ref[h] = jnp.broadcast_to(ixw_ref[0, :, h:h + 1] * w_scale, (tq, LANES))

    def score_tile(kb, carry):
        start = pl.multiple_of(kb * t, t)
        ks = ixk_ref[0, pl.ds(start, t), :]
        sc = jnp.zeros((tq, t), F32)
        for h in range(N_HEADS):
            s = lax.dot_general(qm_ref[h], ks, NT_DIMS, preferred_element_type=F32)
            sc = sc + jnp.maximum(s, 0.0) * wb_ref[h]
        key = jnp.where(start + k_lane <= q_idx, _sortable(sc), INT_MIN)
        keys_ref[:, pl.ds(start, t)] = key
        return carry

    lax.fori_loop(0, n_tiles, score_tile, 0)

    def count(pred):
        cnt_ref[...] = jnp.zeros((tq, t), I32)

        def body(kb, carry):
            start = pl.multiple_of(kb * t, t)
            cnt_ref[...] += pred(keys_ref[:, pl.ds(start, t)], start).astype(I32)
            return carry

        lax.fori_loop(0, n_tiles, body, 0)
        return jnp.sum(cnt_ref[...], axis=1, keepdims=True)

    @pl.when(n_tiles * t <= topk)
    def _():
        thr_ref[0] = jnp.full((tq, LANES), INT_MIN, I32)
        thr_ref[1] = jnp.full((tq, LANES), -1, I32)

    @pl.when(n_tiles * t > topk)
    def _():
        ans = jnp.zeros((tq, 1), I32)
        for bit in range(31, -1, -1):
            cand = ans | np.int32(INT_MIN if bit == 31 else 1 << bit)
            cand_b = jnp.broadcast_to(cand ^ np.int32(INT_MIN), (tq, t))
            c = count(lambda keys, start, cand_b=cand_b: keys >= cand_b)
            ans = jnp.where(c >= topk, cand, ans)
        tau = ans ^ np.int32(INT_MIN)
        tau_b = jnp.broadcast_to(tau, (tq, t))
        n_gt = count(lambda keys, start: keys > tau_b)
        n_ge = count(lambda keys, start: keys >= tau_b)
        need = topk - n_gt
        thr_ref[0] = tau_b
        thr_ref[1] = jnp.full((tq, LANES), seq, I32)

        @pl.when(jnp.max(n_ge) > topk)
        def _():
            cut = jnp.zeros((tq, 1), I32)
            for bit in range(int(math.log2(seq)), -1, -1):
                cand = cut | np.int32(1 << bit)
                cand_b = jnp.broadcast_to(cand, (tq, t))
                c = count(lambda keys, start, cand_b=cand_b: (keys == tau_b) & (start + k_lane < cand_b))
                cut = jnp.where(c < need, cand, cut)
            thr_ref[1] = jnp.broadcast_to(cut, (tq, LANES))

    tau_b = thr_ref[0]
    cut_b = thr_ref[1]

    for h in range(N_HEADS):
        m_ref[h] = jnp.full((tq, LANES), NEG_BIG, F32)
        l_ref[h] = jnp.zeros((tq, LANES), F32)
        acc_ref[h] = jnp.zeros((tq, LANES), F32)

    pq = pq_ref[0]
    pq_min = jnp.min(pq)

    def attend_tile(kb, carry):
        start = pl.multiple_of(kb * t, t)
        keys = keys_ref[:, pl.ds(start, t)]
        sel = (keys > tau_b) | ((keys == tau_b) & (start + k_lane <= cut_b))
        pk = pk_ref[0, kb]
        far = pq_min - jnp.max(pk) >= FAR_DIST

        @pl.when(far)
        def _():
            for h in range(N_HEADS):
                bias_ref[h] = jnp.full((tq, t), rel_ref[REL_BUCKETS - 1, h], F32)

        @pl.when(jnp.logical_not(far))
        def _():
            dist = pq - pk
            bias = [jnp.full((tq, t), rel_ref[0, h], F32) for h in range(N_HEADS)]
            for i, thr in enumerate(BUCKET_THR):
                ge = dist >= thr
                bias = [jnp.where(ge, rel_ref[i + 1, h], bias[h]) for h in range(N_HEADS)]
            for h in range(N_HEADS):
                bias_ref[h] = bias[h]

        kk = kk_ref[0, pl.ds(start, t), :]
        vv = vv_ref[0, pl.ds(start, t), :]
        for h in range(N_HEADS):
            qh = q_ref[0, :, h * LANES:(h + 1) * LANES]
            s = lax.dot_general(qh, kk, NT_DIMS, preferred_element_type=F32) + bias_ref[h]
            s = jnp.where(sel, s, NEG_BIG)
            m_old = m_ref[h]
            m_new = jnp.maximum(m_old, jnp.max(s, axis=1, keepdims=True))
            p = jnp.where(sel, jnp.exp(s - m_new), 0.0)
            alpha = jnp.exp(m_old - m_new)
            l_ref[h] = alpha * l_ref[h] + jnp.sum(p, axis=1, keepdims=True)
            acc_ref[h] = alpha * acc_ref[h] + jnp.dot(p.astype(BF16), vv, preferred_element_type=F32)
            m_ref[h] = m_new
        return carry

    lax.fori_loop(0, n_tiles, attend_tile, 0)

    for p in range(N_HEADS // 2):
        even = acc_ref[2 * p] / l_ref[2 * p]
        odd = acc_ref[2 * p + 1] / l_ref[2 * p + 1]
        o_ref[0, :, p * LANES:(p + 1) * LANES] = jnp.where(lane < HEAD_DIM, even, odd)


def _dsa_attention(rel_bias, pb, pf, dq, dkk, pos_col, pos_rows, b, s, tq):
    topk = min(TOPK_MAX, s // 4)
    pb3 = pb.reshape(b, s, N_BF)
    pf3 = pf.reshape(b, s, N_F32)
    kernel = functools.partial(_dsa_kernel, tq=tq, topk=topk, seq=s)
    return pl.pallas_call(
        kernel,
        grid=(b, s // tq),
        in_specs=[pl.BlockSpec(memory_space=pltpu.SMEM),
                  pl.BlockSpec((1, tq, 2 * LANES), lambda bi, i: (bi, i, B_IXQ // 256)),
                  pl.BlockSpec((1, tq, LANES), lambda bi, i: (bi, i, F_IXW // LANES)),
                  pl.BlockSpec((1, s, LANES), lambda bi, i: (bi, 0, B_IXK // LANES)),
                  pl.BlockSpec((1, tq, N_HEADS * LANES), lambda bi, i: (bi, i, 0)),
                  pl.BlockSpec((1, s, LANES), lambda bi, i: (bi, 0, 0)),
                  pl.BlockSpec((1, s, LANES), lambda bi, i: (bi, 0, B_DSV // LANES)),
                  pl.BlockSpec((1, tq, 1), lambda bi, i: (bi, i, 0)),
                  pl.BlockSpec((1, s // LANES, 1, LANES), lambda bi, i: (bi, 0, 0, 0))],
        out_specs=pl.BlockSpec((1, tq, BRANCH_WIDTH), lambda bi, i: (bi, i, 0)),
        out_shape=jax.ShapeDtypeStruct((b, s, BRANCH_WIDTH), F32),
        scratch_shapes=[pltpu.VMEM((tq, s), I32),
                        pltpu.VMEM((N_HEADS, tq, LANES), BF16),
                        pltpu.VMEM((N_HEADS, tq, LANES), F32),
                        pltpu.VMEM((N_HEADS, tq, LANES), F32),
                        pltpu.VMEM((N_HEADS, tq, LANES), F32),
                        pltpu.VMEM((N_HEADS, tq, LANES), F32),
                        pltpu.VMEM((N_HEADS, tq, LANES), F32),
                        pltpu.VMEM((tq, LANES), I32),
                        pltpu.VMEM((2, tq, LANES), I32)],
        compiler_params=pltpu.CompilerParams(dimension_semantics=("parallel", "arbitrary"),
                                             vmem_limit_bytes=VMEM_LIMIT),
        name="dsa_attention",
    )(rel_bias, pb3, pf3, pb3, dq.reshape(b, s, -1), dkk.reshape(b, s, -1), pb3, pos_col, pos_rows)


def _merge_kernel(x_ref, ya_ref, yb_ref, yc_ref, za_ref, zb_ref, zc_ref, ga_ref, gb_ref, gc_ref,
                  bias_ref, wbr_ref, wout_ref, o_ref):
    merged = None
    branches = ((ya_ref, za_ref, ga_ref), (yb_ref, zb_ref, gb_ref), (yc_ref, zc_ref, gc_ref))
    for n, (y_ref, z_ref, g_ref) in enumerate(branches):
        z = z_ref[...].astype(F32)
        branch = (y_ref[...] * (z * jax.nn.sigmoid(z))).astype(BF16)
        up = jnp.dot(branch, wbr_ref[n], preferred_element_type=F32)
        gate = jax.nn.sigmoid(g_ref[...].astype(F32) + bias_ref[n:n + 1, :])
        merged = gate * up if merged is None else merged + gate * up
    o_ref[...] = x_ref[...] + jnp.dot(merged.astype(BF16), wout_ref[...], preferred_element_type=F32)


def _merge(x2, ya, yb, yc, pb, gate_b, wbr, wout, tm):
    m, d = x2.shape
    row = lambda width: pl.BlockSpec((tm, width), lambda i: (i, 0))
    col = lambda width, off: pl.BlockSpec((tm, width), lambda i, _b=off // width: (i, _b))
    z_specs = [col(BRANCH_WIDTH, B_Z + n * BRANCH_WIDTH) for n in range(N_BRANCH)]
    g_specs = [col(d, B_G + n * d) for n in range(N_BRANCH)]
    return pl.pallas_call(
        _merge_kernel,
        grid=(m // tm,),
        in_specs=[row(d), row(BRANCH_WIDTH), row(BRANCH_WIDTH), row(BRANCH_WIDTH)] + z_specs + g_specs + [
            pl.BlockSpec(gate_b.shape, lambda i: (0, 0)),
            pl.BlockSpec(wbr.shape, lambda i: (0, 0, 0)),
            pl.BlockSpec(wout.shape, lambda i: (0, 0))],
        out_specs=row(d),
        out_shape=jax.ShapeDtypeStruct((m, d), F32),
        compiler_params=pltpu.CompilerParams(dimension_semantics=("parallel",), vmem_limit_bytes=VMEM_LIMIT),
        name="merge",
    )(x2, ya, yb, yc, pb, pb, pb, pb, pb, pb, gate_b, wbr, wout)


def _lane_table(vals_by_lane):
    t = jnp.zeros((LANES,), F32)
    for start, v in vals_by_lane.items():
        t = t.at[start:start + v.shape[0]].set(v)
    return t[None, :]


def _layer_weights(l, w_bf, w_f32, norm_g, mla_q_norm_g, mla_kv_norm_g, mla_w_uq, mla_w_ukv, mla_q_g, mla_k_g,
                   dsa_q_g, dsa_k_g, gate_b, w_branch, w_out):
    uq, ukv = mla_w_uq[l], mla_w_ukv[l]
    half = MLA_ROPE // 2
    q_pieces, qr_pieces, k_pieces, v_pieces = [], [], [], []
    for h in range(N_HEADS):
        o = h * MLA_QK
        q_pieces += [(o, MLA_QK, 1.0), (None, LANES - MLA_QK, 1.0)]
        qr_pieces += [(None, MLA_NOPE, 1.0), (o + MLA_NOPE + half, half, 1.0), (o + MLA_NOPE, half, 1.0),
                      (None, LANES - MLA_QK, 1.0)]
        k_pieces += [(h * 128, MLA_NOPE, 1.0), (None, LANES - MLA_NOPE, 1.0)]
        v_pieces += [(h * 128 + MLA_NOPE, 64, 1.0)]
    scale = MLA_QK ** -0.5
    qg, kg = mla_q_g[l], mla_k_g[l]
    rot_gain = lambda g: jnp.concatenate([g[MLA_NOPE + half:], g[MLA_NOPE:MLA_NOPE + half]])
    dq_g = jnp.concatenate([jnp.concatenate([dsa_q_g[l], jnp.zeros((HEAD_DIM,), F32)]) if h % 2 == 0 else
                            jnp.concatenate([jnp.zeros((HEAD_DIM,), F32), dsa_q_g[l]]) for h in range(N_HEADS)])
    return dict(
        norm_g=norm_g[l][None, :], w_bf=w_bf[l], w_f32=w_f32[l],
        wuq=_assemble(uq, q_pieces).astype(BF16), wuqr=_assemble(uq, qr_pieces).astype(BF16),
        wuk=_assemble(ukv, k_pieces).astype(BF16), wuv=_assemble(ukv, v_pieces).astype(BF16),
        qng=mla_q_norm_g[l][None, :], kvng=mla_kv_norm_g[l][None, :],
        gq=_lane_table({0: qg * scale}), gqr=_lane_table({MLA_NOPE: rot_gain(qg) * scale}),
        gk=_lane_table({0: kg}), gkr=_lane_table({MLA_NOPE: rot_gain(kg)}),
        gdq=(dq_g * HEAD_DIM ** -0.5)[None, :], gdk=jnp.concatenate([dsa_k_g[l], dsa_k_g[l]])[None, :],
        gate_b=gate_b[l], wbr=w_branch[l].astype(BF16), wout=w_out[l].astype(BF16))


def kernel(x, positions, norm_g, w_in, mla_q_norm_g, mla_kv_norm_g, mla_w_uq, mla_w_ukv, mla_q_g, mla_k_g,
           dsa_q_g, dsa_k_g, rel_bias, gate_b, w_branch, w_out):
    b, s, d = x.shape
    depth = w_in.shape[0]
    m = b * s
    assert d == D_MODEL and w_in.shape[-1] == D_IN and s % 256 == 0 and s & (s - 1) == 0

    half = MLA_ROPE // 2
    inv_freq = ROPE_THETA ** (-(jnp.arange(half, dtype=F32) * 2.0) / MLA_ROPE)
    ang = positions.astype(F32)[..., None] * inv_freq
    cos, sin = jnp.cos(ang).reshape(m, half), jnp.sin(ang).reshape(m, half)
    cos_t = jnp.concatenate([jnp.ones((m, MLA_NOPE), F32), cos, cos, jnp.ones((m, LANES - MLA_QK), F32)], axis=1)
    sin_t = jnp.concatenate([jnp.zeros((m, MLA_NOPE), F32), -sin, sin, jnp.zeros((m, LANES - MLA_QK), F32)], axis=1)
    pos_col = positions.reshape(b, s, 1)
    pos_rows = positions.reshape(b, s // LANES, 1, LANES)

    bf_pieces, f32_pieces = _in_proj_layouts()
    w_bf = _assemble(w_in, bf_pieces).astype(BF16)
    w_f32 = _assemble(w_in, f32_pieces).astype(BF16)

    tm = min(1024, m)
    tp = min(512, m)
    x2 = x.reshape(m, d)
    for l in range(depth):
        lw = _layer_weights(l, w_bf, w_f32, norm_g, mla_q_norm_g, mla_kv_norm_g, mla_w_uq, mla_w_ukv,
                            mla_q_g, mla_k_g, dsa_q_g, dsa_k_g, gate_b, w_branch, w_out)
        pb = _in_proj(x2, lw["norm_g"], lw["w_bf"], BF16, tm, 512)
        pf = _in_proj(x2, lw["norm_g"], lw["w_f32"], F32, tm, 512)
        qb, kb, vb, dq, dkk = _prologue(pf, cos_t, sin_t, lw, tp)
        ya = _sb_attention(pb, b, s, 128, 128)
        yb = _mla_attention(qb, kb, vb, b, s, 256)
        yc = _dsa_attention(rel_bias, pb, pf, dq, dkk, pos_col, pos_rows, b, s, 128)
        x2 = _merge(x2, ya.reshape(m, -1), yb.reshape(m, -1), yc.reshape(m, -1), pb, lw["gate_b"],
                    lw["wbr"], lw["wout"], tp)
    return x2.reshape(b, s, d)
```

```python
import functools
import math

import numpy as np
import jax
import jax.numpy as jnp
from jax import lax
from jax.experimental import pallas as pl
from jax.experimental.pallas import tpu as pltpu

F32 = jnp.float32
BF16 = jnp.bfloat16
I32 = jnp.int32

LANES = 128
EPS = 1e-6
D_MODEL = 1024
N_HEADS = 8
HEAD_DIM = 64
MLA_Q_LORA = 256
MLA_KV_LORA = 128
MLA_NOPE = 64
MLA_ROPE = 32
MLA_QK = MLA_NOPE + MLA_ROPE
ROPE_THETA = 10000.0
IDX_DIM = 32
TOPK_MAX = 256
REL_BUCKETS = 32
REL_MAX_DIST = 128
N_BRANCH = 3
BRANCH_WIDTH = 512

O_SAQ, O_SAK, O_SAV = 0, 512, 1024
O_CQ, O_CKV, O_KPE = 1536, 1792, 1920
O_DSQ, O_DSK, O_DSV = 1952, 2464, 2528
O_IXQ, O_IXK, O_IXW = 2592, 2848, 2880
O_Z, O_G = 2888, 4424
D_IN = 7496

NEG_BIG = -1e30
INT_MIN = -(2 ** 31)
SB_UNDERFLOW = 104.0
VMEM_LIMIT = 48 * 1024 * 1024

NT_DIMS = (((1,), (1,)), ((), ()))


def _bucket_thresholds():
    exact = REL_BUCKETS // 2
    n = np.arange(0, 4 * REL_MAX_DIST)
    nf = np.maximum(n, 1).astype(np.float64)
    large = exact + (np.log(nf / exact) / math.log(REL_MAX_DIST / exact) * (REL_BUCKETS - exact)).astype(np.int64)
    bucket = np.where(n < exact, n, np.minimum(large, REL_BUCKETS - 1))
    thr = [int(np.argmax(bucket >= b)) for b in range(1, REL_BUCKETS)]
    assert bucket[-1] == REL_BUCKETS - 1 and all(bucket[t] == b + 1 for b, t in enumerate(thr))
    return thr


BUCKET_THR = _bucket_thresholds()
FAR_DIST = BUCKET_THR[-1]


def _assemble(w, pieces):
    cols = []
    for start, width, scale in pieces:
        if start is None:
            cols.append(jnp.zeros(w.shape[:-1] + (width,), w.dtype))
        else:
            c = w[..., start:start + width]
            cols.append(c if scale == 1.0 else c * scale)
    return jnp.concatenate(cols, axis=-1)


def _head_padded(start, scale=1.0):
    pieces = []
    for h in range(N_HEADS):
        col = (start + h * HEAD_DIM, HEAD_DIM, scale)
        pad = (None, HEAD_DIM, 1.0)
        pieces += [col, pad] if h % 2 == 0 else [pad, col]
    return pieces


def _in_proj_layouts():
    bf = (_head_padded(O_SAQ, HEAD_DIM ** -0.5)
          + [(O_SAK, 512, 1.0), (O_SAV, 512, 1.0), (O_IXQ, 256, 1.0)]
          + [(O_DSV, 64, 1.0)] * 2
          + [(O_IXK, 32, 1.0)] * 4
          + [(O_Z, 1536, 1.0), (O_G, 3072, 1.0)])
    f32 = (_head_padded(O_DSQ)
           + [(O_CQ, 256, 1.0), (O_CKV, 128, 1.0)]
           + [(None, 64, 1.0), (O_KPE, 32, 1.0), (None, 32, 1.0)]
           + [(None, 64, 1.0), (O_KPE + 16, 16, 1.0), (O_KPE, 16, 1.0), (None, 32, 1.0)]
           + [(O_DSK, 64, 1.0)] * 2
           + [(O_IXW, 8, 1.0), (None, 120, 1.0)]
           + [(None, 128, 1.0)])
    return bf, f32


B_SAQ, B_SAK, B_SAV, B_IXQ, B_DSV, B_IXK, B_Z, B_G, N_BF = 0, 1024, 1536, 2048, 2304, 2432, 2560, 4096, 7168
F_DSQ, F_CQ, F_CKV, F_KPE, F_KPER, F_DSK, F_IXW, N_F32 = 0, 1024, 1280, 1408, 1536, 1664, 1792, 2048


def _in_proj_kernel(x_ref, g_ref, w_ref, o_ref, h_ref):
    @pl.when(pl.program_id(1) == 0)
    def _():
        x = x_ref[...]
        ms = jnp.mean(x * x, axis=-1, keepdims=True)
        h_ref[...] = (x * lax.rsqrt(ms + EPS) * g_ref[...]).astype(BF16)

    o_ref[...] = jnp.dot(h_ref[...], w_ref[...], preferred_element_type=F32).astype(o_ref.dtype)


def _in_proj(x2, g, w, out_dtype, tm, tn):
    m, d = x2.shape
    n = w.shape[1]
    return pl.pallas_call(
        _in_proj_kernel,
        grid=(m // tm, n // tn),
        in_specs=[pl.BlockSpec((tm, d), lambda i, j: (i, 0)),
                  pl.BlockSpec((1, d), lambda i, j: (0, 0)),
                  pl.BlockSpec((d, tn), lambda i, j: (0, j))],
        out_specs=pl.BlockSpec((tm, tn), lambda i, j: (i, j)),
        out_shape=jax.ShapeDtypeStruct((m, n), out_dtype),
        scratch_shapes=[pltpu.VMEM((tm, d), BF16)],
        compiler_params=pltpu.CompilerParams(dimension_semantics=("parallel", "arbitrary"),
                                             vmem_limit_bytes=VMEM_LIMIT),
        name="in_proj",
    )(x2, g, w)


def _rms(x, width):
    return lax.rsqrt(jnp.sum(x * x, axis=-1, keepdims=True) * (1.0 / width) + EPS)


def _prologue_kernel(dsq_ref, cq_ref, ckv_ref, kpe_ref, kper_ref, dsk_ref, cos_ref, sin_ref,
                     wuq_ref, wuqr_ref, wuk_ref, wuv_ref,
                     qng_ref, kvng_ref, gq_ref, gqr_ref, gk_ref, gkr_ref, gdq_ref, gdk_ref,
                     qb_ref, kb_ref, vb_ref, dq_ref, dkk_ref):
    cos = cos_ref[...]
    sin = sin_ref[...]

    cq = cq_ref[...]
    cqn = (cq * _rms(cq, MLA_Q_LORA) * qng_ref[...]).astype(BF16)
    q = jnp.dot(cqn, wuq_ref[...], preferred_element_type=F32)
    qr = jnp.dot(cqn, wuqr_ref[...], preferred_element_type=F32)
    cg = cos * gq_ref[...]
    sg = sin * gqr_ref[...]
    for h in range(N_HEADS):
        sl = slice(h * LANES, (h + 1) * LANES)
        qh = q[:, sl]
        qb_ref[:, sl] = ((qh * cg + qr[:, sl] * sg) * _rms(qh, MLA_QK)).astype(BF16)

    ckv = ckv_ref[...]
    ckvn = (ckv * _rms(ckv, MLA_KV_LORA) * kvng_ref[...]).astype(BF16)
    kn = jnp.dot(ckvn, wuk_ref[...], preferred_element_type=F32)
    vb_ref[...] = jnp.dot(ckvn, wuv_ref[...], preferred_element_type=F32).astype(BF16)
    kpe = kpe_ref[...]
    cgk = cos * gk_ref[...]
    rot = kper_ref[...] * (sin * gkr_ref[...])
    for h in range(N_HEADS):
        sl = slice(h * LANES, (h + 1) * LANES)
        kf = kn[:, sl] + kpe
        kb_ref[:, sl] = ((kf * cgk + rot) * _rms(kf, MLA_QK)).astype(BF16)

    for h in range(N_HEADS):
        sl = slice(h * LANES, (h + 1) * LANES)
        xq = dsq_ref[:, sl]
        dq_ref[:, sl] = (xq * _rms(xq, HEAD_DIM) * gdq_ref[:, sl]).astype(BF16)
    xk = dsk_ref[...]
    dkk_ref[...] = (xk * _rms(xk, 2 * HEAD_DIM) * gdk_ref[...]).astype(BF16)


def _prologue(pf, cos_t, sin_t, lw, tm):
    m = pf.shape[0]

    def act(width, off):
        return pl.BlockSpec((tm, width), lambda i, _b=off // width: (i, _b))

    def whole(a):
        return pl.BlockSpec(a.shape, lambda i: (0,) * a.ndim)

    weights = [lw[k] for k in ("wuq", "wuqr", "wuk", "wuv", "qng", "kvng", "gq", "gqr", "gk", "gkr", "gdq", "gdk")]
    row = lambda width: pl.BlockSpec((tm, width), lambda i: (i, 0))
    return pl.pallas_call(
        _prologue_kernel,
        grid=(m // tm,),
        in_specs=[act(1024, F_DSQ), act(256, F_CQ), act(128, F_CKV), act(128, F_KPE), act(128, F_KPER),
                  act(128, F_DSK), row(LANES), row(LANES)] + [whole(w) for w in weights],
        out_specs=[row(1024), row(1024), row(512), row(1024), row(LANES)],
        out_shape=[jax.ShapeDtypeStruct((m, 1024), BF16), jax.ShapeDtypeStruct((m, 1024), BF16),
                   jax.ShapeDtypeStruct((m, 512), BF16), jax.ShapeDtypeStruct((m, 1024), BF16),
                   jax.ShapeDtypeStruct((m, LANES), BF16)],
        compiler_params=pltpu.CompilerParams(dimension_semantics=("parallel",), vmem_limit_bytes=VMEM_LIMIT),
        name="prologue",
    )(pf, pf, pf, pf, pf, pf, cos_t, sin_t, *weights)


def _sb_kernel(q_ref, k_ref, v_ref, o_ref, acc_ref, carry_ref, hl_ref, lb_ref, *, tq):
    t = LANES
    qi = pl.program_id(1)
    jj = lax.broadcasted_iota(I32, (t, 2 * t), 0)
    ss = lax.broadcasted_iota(I32, (t, 2 * t), 1)
    tri_ones = jnp.logical_or(ss >= t, jj > ss).astype(BF16)
    q_idx = qi * tq + lax.broadcasted_iota(I32, (tq, t), 0)
    k_lane = lax.broadcasted_iota(I32, (tq, t), 1)

    acc_ref[...] = jnp.zeros(acc_ref.shape, F32)
    carry_ref[...] = jnp.zeros(carry_ref.shape, F32)

    def cond(c):
        kb, cmax = c
        return jnp.logical_and(kb >= 0, cmax > -SB_UNDERFLOW)

    def body(c):
        kb, _ = c
        start = pl.multiple_of(kb * t, t)
        mask = (start + k_lane) < q_idx
        for h in range(N_HEADS):
            q = q_ref[0, :, h * LANES:(h + 1) * LANES]
            k = k_ref[0, pl.ds(start, t), (h // 2) * LANES:(h // 2 + 1) * LANES]
            z = lax.dot_general(q, k, NT_DIMS, preferred_element_type=F32)
            sp = jnp.maximum(z, 0.0) + jnp.log(1.0 + jnp.exp(-jnp.abs(z)))
            l1m = jnp.where(mask, -sp, 0.0)
            hi = l1m.astype(BF16)
            hl_ref[2 * h * tq:(2 * h + 1) * tq, :] = hi
            hl_ref[(2 * h + 1) * tq:(2 * h + 2) * tq, :] = (l1m - hi.astype(F32)).astype(BF16)
            lb_ref[h * tq:(h + 1) * tq, :] = z - sp
        sums = jnp.dot(hl_ref[...], tri_ones, preferred_element_type=F32)
        for h in range(N_HEADS):
            rows = slice(h * tq, (h + 1) * tq)
            both = sums[2 * h * tq:(2 * h + 1) * tq, :] + sums[(2 * h + 1) * tq:(2 * h + 2) * tq, :]
            carry = carry_ref[rows, :]
            a = jnp.where(mask, jnp.exp(lb_ref[rows, :] + carry + both[:, :t]), 0.0)
            v = v_ref[0, pl.ds(start, t), (h // 2) * LANES:(h // 2 + 1) * LANES]
            acc_ref[rows, :] += jnp.dot(a.astype(BF16), v, preferred_element_type=F32)
            carry_ref[rows, :] = carry + both[:, t:]
        return kb - 1, jnp.max(carry_ref[...])

    lax.while_loop(cond, body, (((qi + 1) * tq) // t - 1, jnp.float32(0.0)))

    lane = lax.broadcasted_iota(I32, (tq, LANES), 1)
    for p in range(N_HEADS // 2):
        even = acc_ref[2 * p * tq:(2 * p + 1) * tq, :]
        odd = acc_ref[(2 * p + 1) * tq:(2 * p + 2) * tq, :]
        o_ref[0, :, p * LANES:(p + 1) * LANES] = jnp.where(lane < HEAD_DIM, even, odd)


def _sb_attention(pb, b, s, tq):
    pb3 = pb.reshape(b, s, N_BF)
    kernel = functools.partial(_sb_kernel, tq=tq)
    return pl.pallas_call(
        kernel,
        grid=(b, s // tq),
        in_specs=[pl.BlockSpec((1, tq, N_HEADS * LANES), lambda bi, i: (bi, i, B_SAQ // (N_HEADS * LANES))),
                  pl.BlockSpec((1, s, BRANCH_WIDTH), lambda bi, i: (bi, 0, B_SAK // BRANCH_WIDTH)),
                  pl.BlockSpec((1, s, BRANCH_WIDTH), lambda bi, i: (bi, 0, B_SAV // BRANCH_WIDTH))],
        out_specs=pl.BlockSpec((1, tq, BRANCH_WIDTH), lambda bi, i: (bi, i, 0)),
        out_shape=jax.ShapeDtypeStruct((b, s, BRANCH_WIDTH), F32),
        scratch_shapes=[pltpu.VMEM((N_HEADS * tq, LANES), F32), pltpu.VMEM((N_HEADS * tq, LANES), F32),
                        pltpu.VMEM((2 * N_HEADS * tq, LANES), BF16), pltpu.VMEM((N_HEADS * tq, LANES), F32)],
        compiler_params=pltpu.CompilerParams(dimension_semantics=("parallel", "arbitrary"),
                                             vmem_limit_bytes=VMEM_LIMIT),
        name="sb_attention",
    )(pb3, pb3, pb3)


def _mla_kernel(q_ref, k_ref, v_ref, o_ref, m_ref, l_ref, acc_ref, *, tq, tk):
    qi = pl.program_id(2)
    nj = tk // LANES
    n_full = (qi * tq) // tk
    row = qi * tq + lax.broadcasted_iota(I32, (tq, tk), 0)
    col = lax.broadcasted_iota(I32, (tq, tk), 1)

    m_ref[...] = jnp.full(m_ref.shape, NEG_BIG, F32)
    l_ref[...] = jnp.zeros(l_ref.shape, F32)
    acc_ref[...] = jnp.zeros(acc_ref.shape, F32)

    def step(c, diagonal):
        start = pl.multiple_of(c * tk, tk)
        v = v_ref[0, pl.ds(start, tk), :]
        for h in range(2):
            rows = slice(h * tq, (h + 1) * tq)
            q = q_ref[0, :, h * LANES:(h + 1) * LANES]
            k = k_ref[0, pl.ds(start, tk), h * LANES:(h + 1) * LANES]
            s = lax.dot_general(q, k, NT_DIMS, preferred_element_type=F32)
            if diagonal:
                s = jnp.where(start + col <= row, s, NEG_BIG)
            m_old = m_ref[rows, :]
            m_new = jnp.maximum(m_old, jnp.max(s, axis=1, keepdims=True))
            ps = [jnp.exp(s[:, j * LANES:(j + 1) * LANES] - m_new) for j in range(nj)]
            psum = functools.reduce(lambda a, b: a + b, ps)
            alpha = jnp.exp(m_old - m_new)
            l_ref[rows, :] = alpha * l_ref[rows, :] + jnp.sum(psum, axis=1, keepdims=True)
            p = jnp.concatenate(ps, axis=1).astype(BF16)
            acc_ref[rows, :] = alpha * acc_ref[rows, :] + jnp.dot(p, v, preferred_element_type=F32)
            m_ref[rows, :] = m_new

    def body(c, carry):
        step(c, False)
        return carry

    lax.fori_loop(0, n_full, body, 0)
    step(n_full, True)

    lane = lax.broadcasted_iota(I32, (tq, LANES), 1)
    even = acc_ref[0:tq, :] / l_ref[0:tq, :]
    odd = acc_ref[tq:2 * tq, :] / l_ref[tq:2 * tq, :]
    o_ref[0] = jnp.where(lane < HEAD_DIM, even, odd)


def _mla_attention(qb, kb, vb, b, s, tq, tk):
    kernel = functools.partial(_mla_kernel, tq=tq, tk=tk)
    return pl.pallas_call(
        kernel,
        grid=(b, N_HEADS // 2, s // tq),
        in_specs=[pl.BlockSpec((1, tq, 2 * LANES), lambda bi, p, i: (bi, i, p)),
                  pl.BlockSpec((1, s, 2 * LANES), lambda bi, p, i: (bi, 0, p)),
                  pl.BlockSpec((1, s, LANES), lambda bi, p, i: (bi, 0, p))],
        out_specs=pl.BlockSpec((1, tq, LANES), lambda bi, p, i: (bi, i, p)),
        out_shape=jax.ShapeDtypeStruct((b, s, BRANCH_WIDTH), F32),
        scratch_shapes=[pltpu.VMEM((2 * tq, LANES), F32), pltpu.VMEM((2 * tq, LANES), F32),
                        pltpu.VMEM((2 * tq, LANES), F32)],
        compiler_params=pltpu.CompilerParams(dimension_semantics=("parallel", "parallel", "arbitrary"),
                                             vmem_limit_bytes=VMEM_LIMIT),
        name="mla_attention",
    )(qb.reshape(b, s, -1), kb.reshape(b, s, -1), vb.reshape(b, s, -1))


def _sortable(x):
    bits = pltpu.bitcast(x + 0.0, I32)
    return bits ^ ((bits >> 31) & 0x7FFFFFFF)


def _dsa_kernel(rel_ref, ixq_ref, ixw_ref, ixk_ref, q_ref, kk_ref, vv_ref, pq_ref, pk_ref, o_ref,
                keys_ref, qm_ref, wb_ref, qs_ref, selb_ref, s_ref, p_ref, add_ref,
                m_ref, l_ref, alpha_ref, acc_ref, thr_ref, *, tq, tc, topk, seq):
    t = LANES
    nj = tc // t
    qi = pl.program_id(1)
    n_chunks = (qi * tq) // tc + 1
    lane = lax.broadcasted_iota(I32, (tq, t), 1)
    q_idx = qi * tq + lax.broadcasted_iota(I32, (tq, t), 0)
    head_rows = [slice(h * tq, (h + 1) * tq) for h in range(N_HEADS)]

    w_scale = (IDX_DIM ** -0.5) * (N_HEADS ** -0.5)
    for h in range(N_HEADS):
        grp = ixq_ref[0, :, (h // 4) * LANES:(h // 4 + 1) * LANES]
        lo = (h % 4) * IDX_DIM
        qm_ref[head_rows[h], :] = jnp.where((lane >= lo) & (lane < lo + IDX_DIM), grp, jnp.zeros_like(grp))
        wb_ref[head_rows[h], :] = jnp.broadcast_to(ixw_ref[0, :, h:h + 1] * w_scale, (tq, t))
        qs_ref[head_rows[h], :] = q_ref[0, :, h * LANES:(h + 1) * LANES]

    def score_chunk(c, carry):
        start = pl.multiple_of(c * tc, tc)
        s_ref[...] = lax.dot_general(qm_ref[...], ixk_ref[0, pl.ds(start, tc), :], NT_DIMS,
                                     preferred_element_type=F32)
        for j in range(nj):
            cols = slice(j * t, (j + 1) * t)
            sc = None
            for h in range(N_HEADS):
                term = jnp.maximum(s_ref[head_rows[h], cols], 0.0) * wb_ref[head_rows[h], :]
                sc = term if sc is None else sc + term
            off = start + j * t
            keys_ref[:, pl.ds(off, t)] = jnp.where(off + lane <= q_idx, _sortable(sc), INT_MIN)
        return carry

    lax.fori_loop(0, n_chunks, score_chunk, 0)

    ones = jnp.ones((t, t), BF16)

    def count(pred):
        def body(c, cnt):
            start = pl.multiple_of(c * tc, tc)
            for j in range(nj):
                off = start + j * t
                cnt = cnt + pred(keys_ref[:, pl.ds(off, t)], off).astype(I32)
            return cnt

        cnt = lax.fori_loop(0, n_chunks, body, jnp.zeros((tq, t), I32))
        return jnp.dot(cnt.astype(F32).astype(BF16), ones, preferred_element_type=F32)

    @pl.when((qi + 1) * tq <= topk)
    def _():
        thr_ref[0] = jnp.full((tq, t), INT_MIN, I32)
        thr_ref[1] = jnp.full((tq, t), -1, I32)

    @pl.when((qi + 1) * tq > topk)
    def _():
        ans = jnp.zeros((tq, t), I32)
        for bit in range(31, -1, -1):
            cand = ans | np.int32(INT_MIN if bit == 31 else 1 << bit)
            cand_s = cand ^ np.int32(INT_MIN)
            c = count(lambda keys, off, cand_s=cand_s: keys >= cand_s)
            ans = jnp.where(c >= topk, cand, ans)
        tau = ans ^ np.int32(INT_MIN)
        n_gt = count(lambda keys, off: keys > tau)
        n_ge = count(lambda keys, off: keys >= tau)
        need = topk - n_gt
        thr_ref[0] = tau
        thr_ref[1] = jnp.full((tq, t), seq, I32)

        @pl.when(jnp.max(n_ge) > topk)
        def _():
            cut = jnp.zeros((tq, t), I32)
            for bit in range(int(math.log2(seq)), -1, -1):
                cand = cut | np.int32(1 << bit)
                c = count(lambda keys, off, cand=cand: (keys == tau) & (off + lane < cand))
                cut = jnp.where(c < need, cand, cut)
            thr_ref[1] = cut

    tau = thr_ref[0]
    cut = thr_ref[1]

    m_ref[...] = jnp.full(m_ref.shape, NEG_BIG, F32)
    l_ref[...] = jnp.zeros(l_ref.shape, F32)
    acc_ref[...] = jnp.zeros(acc_ref.shape, F32)
    pq_min = jnp.min(pq_ref[0])
    last_bias = [rel_ref[REL_BUCKETS - 1, h] for h in range(N_HEADS)]

    def softmax_heads(addend_of, bias_of):
        for h in range(N_HEADS):
            rows = head_rows[h]
            sb = s_ref[rows, :] + addend_of(h)
            m_old = m_ref[rows, :]
            m_new = jnp.maximum(m_old, jnp.max(sb, axis=1, keepdims=True) + bias_of(h))
            shift = m_new - bias_of(h)
            psum = None
            for j in range(nj):
                cols = slice(j * t, (j + 1) * t)
                pj = jnp.exp(sb[:, cols] - shift)
                psum = pj if psum is None else psum + pj
                p_ref[rows, cols] = pj.astype(BF16)
            alpha = jnp.exp(m_old - m_new)
            l_ref[rows, :] = alpha * l_ref[rows, :] + jnp.sum(psum, axis=1, keepdims=True)
            alpha_ref[rows, :] = alpha
            m_ref[rows, :] = m_new

    def attend_chunk(c, carry):
        start = pl.multiple_of(c * tc, tc)
        for j in range(nj):
            off = start + j * t
            keys = keys_ref[:, pl.ds(off, t)]
            sel = (keys > tau) | ((keys == tau) & (off + lane <= cut))
            selb_ref[:, j * t:(j + 1) * t] = jnp.where(sel, 0.0, NEG_BIG)
        s_ref[...] = lax.dot_general(qs_ref[...], kk_ref[0, pl.ds(start, tc), :], NT_DIMS,
                                     preferred_element_type=F32)
        pk = pk_ref[0, c]
        chunk_far = pq_min - jnp.max(pk) >= FAR_DIST

        @pl.when(chunk_far)
        def _():
            softmax_heads(lambda h: selb_ref[...], lambda h: last_bias[h])

        @pl.when(jnp.logical_not(chunk_far))
        def _():
            for j in range(nj):
                cols = slice(j * t, (j + 1) * t)
                pk_j = pk[:, cols]
                tile_far = pq_min - jnp.max(pk_j) >= FAR_DIST

                @pl.when(tile_far)
                def _():
                    for h in range(N_HEADS):
                        add_ref[h, :, cols] = selb_ref[:, cols] + last_bias[h]

                @pl.when(jnp.logical_not(tile_far))
                def _():
                    slab = 16

                    def slab_body(r, carry2):
                        r0 = pl.multiple_of(r * slab, slab)
                        dist = pq_ref[0, pl.ds(r0, slab), :] - pk_j
                        bias = [jnp.full((slab, t), rel_ref[0, h], F32) for h in range(N_HEADS)]
                        for i, thr in enumerate(BUCKET_THR):
                            ge = dist >= thr
                            bias = [jnp.where(ge, rel_ref[i + 1, h], bias[h]) for h in range(N_HEADS)]
                        sb = selb_ref[pl.ds(r0, slab), cols]
                        for h in range(N_HEADS):
                            add_ref[h, pl.ds(r0, slab), cols] = bias[h] + sb
                        return carry2

                    lax.fori_loop(0, tq // slab, slab_body, 0)

            softmax_heads(lambda h: add_ref[h], lambda h: 0.0)

        pv = jnp.dot(p_ref[...], vv_ref[0, pl.ds(start, tc), :], preferred_element_type=F32)
        acc_ref[...] = alpha_ref[...] * acc_ref[...] + pv
        return carry

    lax.fori_loop(0, n_chunks, attend_chunk, 0)

    for p in range(N_HEADS // 2):
        even = acc_ref[head_rows[2 * p], :] / l_ref[head_rows[2 * p], :]
        odd = acc_ref[head_rows[2 * p + 1], :] / l_ref[head_rows[2 * p + 1], :]
        o_ref[0, :, p * LANES:(p + 1) * LANES] = jnp.where(lane < HEAD_DIM, even, odd)


def _dsa_attention(rel_bias, pb, pf, dq, dkk, pos_col, pos_rows, b, s, tq, tc):
    topk = min(TOPK_MAX, s // 4)
    pb3 = pb.reshape(b, s, N_BF)
    pf3 = pf.reshape(b, s, N_F32)
    rows = N_HEADS * tq
    kernel = functools.partial(_dsa_kernel, tq=tq, tc=tc, topk=topk, seq=s)
    return pl.pallas_call(
        kernel,
        grid=(b, s // tq),
        in_specs=[pl.BlockSpec(memory_space=pltpu.SMEM),
                  pl.BlockSpec((1, tq, 2 * LANES), lambda bi, i: (bi, i, B_IXQ // 256)),
                  pl.BlockSpec((1, tq, LANES), lambda bi, i: (bi, i, F_IXW // LANES)),
                  pl.BlockSpec((1, s, LANES), lambda bi, i: (bi, 0, B_IXK // LANES)),
                  pl.BlockSpec((1, tq, N_HEADS * LANES), lambda bi, i: (bi, i, 0)),
                  pl.BlockSpec((1, s, LANES), lambda bi, i: (bi, 0, 0)),
                  pl.BlockSpec((1, s, LANES), lambda bi, i: (bi, 0, B_DSV // LANES)),
                  pl.BlockSpec((1, tq, 1), lambda bi, i: (bi, i, 0)),
                  pl.BlockSpec((1, s // tc, 1, tc), lambda bi, i: (bi, 0, 0, 0))],
        out_specs=pl.BlockSpec((1, tq, BRANCH_WIDTH), lambda bi, i: (bi, i, 0)),
        out_shape=jax.ShapeDtypeStruct((b, s, BRANCH_WIDTH), F32),
        scratch_shapes=[pltpu.VMEM((tq, s), I32),
                        pltpu.VMEM((rows, LANES), BF16),
                        pltpu.VMEM((rows, LANES), F32),
                        pltpu.VMEM((rows, LANES), BF16),
                        pltpu.VMEM((tq, tc), F32),
                        pltpu.VMEM((rows, tc), F32),
                        pltpu.VMEM((rows, tc), BF16),
                        pltpu.VMEM((N_HEADS, tq, tc), F32),
                        pltpu.VMEM((rows, LANES), F32),
                        pltpu.VMEM((rows, LANES), F32),
                        pltpu.VMEM((rows, LANES), F32),
                        pltpu.VMEM((rows, LANES), F32),
                        pltpu.VMEM((2, tq, LANES), I32)],
        compiler_params=pltpu.CompilerParams(dimension_semantics=("parallel", "arbitrary"),
                                             vmem_limit_bytes=VMEM_LIMIT),
        name="dsa_attention",
    )(rel_bias, pb3, pf3, pb3, dq.reshape(b, s, -1), dkk.reshape(b, s, -1), pb3, pos_col, pos_rows)


def _merge_kernel(x_ref, ya_ref, yb_ref, yc_ref, za_ref, zb_ref, zc_ref, ga_ref, gb_ref, gc_ref,
                  bias_ref, wbr_ref, wout_ref, o_ref):
    merged = None
    branches = ((ya_ref, za_ref, ga_ref), (yb_ref, zb_ref, gb_ref), (yc_ref, zc_ref, gc_ref))
    for n, (y_ref, z_ref, g_ref) in enumerate(branches):
        z = z_ref[...].astype(F32)
        branch = (y_ref[...] * (z * jax.nn.sigmoid(z))).astype(BF16)
        up = jnp.dot(branch, wbr_ref[n], preferred_element_type=F32)
        gate = jax.nn.sigmoid(g_ref[...].astype(F32) + bias_ref[n:n + 1, :])
        merged = gate * up if merged is None else merged + gate * up
    o_ref[...] = x_ref[...] + jnp.dot(merged.astype(BF16), wout_ref[...], preferred_element_type=F32)


def _merge(x2, ya, yb, yc, pb, gate_b, wbr, wout, tm):
    m, d = x2.shape
    row = lambda width: pl.BlockSpec((tm, width), lambda i: (i, 0))
    col = lambda width, off: pl.BlockSpec((tm, width), lambda i, _b=off // width: (i, _b))
    z_specs = [col(BRANCH_WIDTH, B_Z + n * BRANCH_WIDTH) for n in range(N_BRANCH)]
    g_specs = [col(d, B_G + n * d) for n in range(N_BRANCH)]
    return pl.pallas_call(
        _merge_kernel,
        grid=(m // tm,),
        in_specs=[row(d), row(BRANCH_WIDTH), row(BRANCH_WIDTH), row(BRANCH_WIDTH)] + z_specs + g_specs + [
            pl.BlockSpec(gate_b.shape, lambda i: (0, 0)),
            pl.BlockSpec(wbr.shape, lambda i: (0, 0, 0)),
            pl.BlockSpec(wout.shape, lambda i: (0, 0))],
        out_specs=row(d),
        out_shape=jax.ShapeDtypeStruct((m, d), F32),
        compiler_params=pltpu.CompilerParams(dimension_semantics=("parallel",), vmem_limit_bytes=VMEM_LIMIT),
        name="merge",
    )(x2, ya, yb, yc, pb, pb, pb, pb, pb, pb, gate_b, wbr, wout)


def _lane_table(vals_by_lane):
    t = jnp.zeros((LANES,), F32)
    for start, v in vals_by_lane.items():
        t = t.at[start:start + v.shape[0]].set(v)
    return t[None, :]


def _layer_weights(l, w_bf, w_f32, norm_g, mla_q_norm_g, mla_kv_norm_g, mla_w_uq, mla_w_ukv, mla_q_g, mla_k_g,
                   dsa_q_g, dsa_k_g, gate_b, w_branch, w_out):
    uq, ukv = mla_w_uq[l], mla_w_ukv[l]
    half = MLA_ROPE // 2
    q_pieces, qr_pieces, k_pieces, v_pieces = [], [], [], []
    for h in range(N_HEADS):
        o = h * MLA_QK
        q_pieces += [(o, MLA_QK, 1.0), (None, LANES - MLA_QK, 1.0)]
        qr_pieces += [(None, MLA_NOPE, 1.0), (o + MLA_NOPE + half, half, 1.0), (o + MLA_NOPE, half, 1.0),
                      (None, LANES - MLA_QK, 1.0)]
        k_pieces += [(h * 128, MLA_NOPE, 1.0), (None, LANES - MLA_NOPE, 1.0)]
        v_pieces += [(h * 128 + MLA_NOPE, 64, 1.0)]
    scale = MLA_QK ** -0.5
    qg, kg = mla_q_g[l], mla_k_g[l]
    rot_gain = lambda g: jnp.concatenate([g[MLA_NOPE + half:], g[MLA_NOPE:MLA_NOPE + half]])
    dq_g = jnp.concatenate([jnp.concatenate([dsa_q_g[l], jnp.zeros((HEAD_DIM,), F32)]) if h % 2 == 0 else
                            jnp.concatenate([jnp.zeros((HEAD_DIM,), F32), dsa_q_g[l]]) for h in range(N_HEADS)])
    return dict(
        norm_g=norm_g[l][None, :], w_bf=w_bf[l], w_f32=w_f32[l],
        wuq=_assemble(uq, q_pieces).astype(BF16), wuqr=_assemble(uq, qr_pieces).astype(BF16),
        wuk=_assemble(ukv, k_pieces).astype(BF16), wuv=_assemble(ukv, v_pieces).astype(BF16),
        qng=mla_q_norm_g[l][None, :], kvng=mla_kv_norm_g[l][None, :],
        gq=_lane_table({0: qg * scale}), gqr=_lane_table({MLA_NOPE: rot_gain(qg) * scale}),
        gk=_lane_table({0: kg}), gkr=_lane_table({MLA_NOPE: rot_gain(kg)}),
        gdq=(dq_g * HEAD_DIM ** -0.5)[None, :], gdk=jnp.concatenate([dsa_k_g[l], dsa_k_g[l]])[None, :],
        gate_b=gate_b[l], wbr=w_branch[l].astype(BF16), wout=w_out[l].astype(BF16))


def kernel(x, positions, norm_g, w_in, mla_q_norm_g, mla_kv_norm_g, mla_w_uq, mla_w_ukv, mla_q_g, mla_k_g,
           dsa_q_g, dsa_k_g, rel_bias, gate_b, w_branch, w_out):
    b, s, d = x.shape
    depth = w_in.shape[0]
    m = b * s
    assert d == D_MODEL and w_in.shape[-1] == D_IN and s % 512 == 0 and s & (s - 1) == 0

    half = MLA_ROPE // 2
    inv_freq = ROPE_THETA ** (-(jnp.arange(half, dtype=F32) * 2.0) / MLA_ROPE)
    ang = positions.astype(F32)[..., None] * inv_freq
    cos, sin = jnp.cos(ang).reshape(m, half), jnp.sin(ang).reshape(m, half)
    cos_t = jnp.concatenate([jnp.ones((m, MLA_NOPE), F32), cos, cos, jnp.ones((m, LANES - MLA_QK), F32)], axis=1)
    sin_t = jnp.concatenate([jnp.zeros((m, MLA_NOPE), F32), -sin, sin, jnp.zeros((m, LANES - MLA_QK), F32)], axis=1)

    bf_pieces, f32_pieces = _in_proj_layouts()
    w_bf = _assemble(w_in, bf_pieces).astype(BF16)
    w_f32 = _assemble(w_in, f32_pieces).astype(BF16)

    tm = min(1024, m)
    tp = min(512, m)
    dsa_tc = 512
    pos_col = positions.reshape(b, s, 1)
    pos_rows = positions.reshape(b, s // dsa_tc, 1, dsa_tc)
    x2 = x.reshape(m, d)
    for l in range(depth):
        lw = _layer_weights(l, w_bf, w_f32, norm_g, mla_q_norm_g, mla_kv_norm_g, mla_w_uq, mla_w_ukv,
                            mla_q_g, mla_k_g, dsa_q_g, dsa_k_g, gate_b, w_branch, w_out)
        pb = _in_proj(x2, lw["norm_g"], lw["w_bf"], BF16, tm, 512)
        pf = _in_proj(x2, lw["norm_g"], lw["w_f32"], F32, tm, 512)
        qb, kb, vb, dq, dkk = _prologue(pf, cos_t, sin_t, lw, tp)
        ya = _sb_attention(pb, b, s, 128)
        yb = _mla_attention(qb, kb, vb, b, s, 256, 512)
        yc = _dsa_attention(rel_bias, pb, pf, dq, dkk, pos_col, pos_rows, b, s, 128, dsa_tc)
        x2 = _merge(x2, ya.reshape(m, -1), yb.reshape(m, -1), yc.reshape(m, -1), pb, lw["gate_b"],
                    lw["wbr"], lw["wout"], tp)
    return x2.reshape(b, s, d)
```

```python
import functools
import math

import numpy as np
import jax
import jax.numpy as jnp
from jax import lax
from jax.experimental import pallas as pl
from jax.experimental.pallas import tpu as pltpu

F32 = jnp.float32
BF16 = jnp.bfloat16
I32 = jnp.int32

LANES = 128
EPS = 1e-6
D_MODEL = 1024
N_HEADS = 8
HEAD_DIM = 64
MLA_Q_LORA = 256
MLA_KV_LORA = 128
MLA_NOPE = 64
MLA_ROPE = 32
MLA_QK = MLA_NOPE + MLA_ROPE
ROPE_THETA = 10000.0
IDX_DIM = 32
TOPK_MAX = 256
REL_BUCKETS = 32
REL_MAX_DIST = 128
N_BRANCH = 3
BRANCH_WIDTH = 512

O_SAQ, O_SAK, O_SAV = 0, 512, 1024
O_CQ, O_CKV, O_KPE = 1536, 1792, 1920
O_DSQ, O_DSK, O_DSV = 1952, 2464, 2528
O_IXQ, O_IXK, O_IXW = 2592, 2848, 2880
O_Z, O_G = 2888, 4424
D_IN = 7496

NEG_BIG = -1e30
INT_MIN = -(2 ** 31)
SB_UNDERFLOW = 104.0
VMEM_LIMIT = 48 * 1024 * 1024

NT_DIMS = (((1,), (1,)), ((), ()))


def _t5_buckets():
    exact = REL_BUCKETS // 2
    n = np.arange(0, 4 * REL_MAX_DIST)
    nf = np.maximum(n, 1).astype(np.float64)
    large = exact + (np.log(nf / exact) / math.log(REL_MAX_DIST / exact) * (REL_BUCKETS - exact)).astype(np.int64)
    bucket = np.where(n < exact, n, np.minimum(large, REL_BUCKETS - 1))
    assert np.all(np.diff(bucket) >= 0) and np.all(bucket[LANES - 1:] == REL_BUCKETS - 1)
    return bucket[:LANES]


BUCKET_OF_DIST = _t5_buckets()
FAR_DIST = int(np.argmax(BUCKET_OF_DIST == REL_BUCKETS - 1))


def _assemble(w, pieces):
    cols = []
    for start, width, scale in pieces:
        if start is None:
            cols.append(jnp.zeros(w.shape[:-1] + (width,), w.dtype))
        else:
            c = w[..., start:start + width]
            cols.append(c if scale == 1.0 else c * scale)
    return jnp.concatenate(cols, axis=-1)


def _head_padded(start, scale=1.0):
    pieces = []
    for h in range(N_HEADS):
        col = (start + h * HEAD_DIM, HEAD_DIM, scale)
        pad = (None, HEAD_DIM, 1.0)
        pieces += [col, pad] if h % 2 == 0 else [pad, col]
    return pieces


def _in_proj_layouts():
    bf = (_head_padded(O_SAQ, HEAD_DIM ** -0.5)
          + [(O_SAK, 512, 1.0), (O_SAV, 512, 1.0), (O_IXQ, 256, 1.0)]
          + [(O_DSV, 64, 1.0)] * 2
          + [(O_IXK, 32, 1.0)] * 4
          + [(O_Z, 1536, 1.0), (O_G, 3072, 1.0)])
    f32 = (_head_padded(O_DSQ)
           + [(O_CQ, 256, 1.0), (O_CKV, 128, 1.0)]
           + [(None, 64, 1.0), (O_KPE, 32, 1.0), (None, 32, 1.0)]
           + [(None, 64, 1.0), (O_KPE + 16, 16, 1.0), (O_KPE, 16, 1.0), (None, 32, 1.0)]
           + [(O_DSK, 64, 1.0)] * 2
           + [(O_IXW, 8, 1.0), (None, 120, 1.0)]
           + [(None, 128, 1.0)])
    return bf, f32


B_SAQ, B_SAK, B_SAV, B_IXQ, B_DSV, B_IXK, B_Z, B_G, N_BF = 0, 1024, 1536, 2048, 2304, 2432, 2560, 4096, 7168
F_DSQ, F_CQ, F_CKV, F_KPE, F_KPER, F_DSK, F_IXW, N_F32 = 0, 1024, 1280, 1408, 1536, 1664, 1792, 2048


def _in_proj_kernel(x_ref, g_ref, w_ref, o_ref, h_ref):
    @pl.when(pl.program_id(1) == 0)
    def _():
        x = x_ref[...]
        ms = jnp.mean(x * x, axis=-1, keepdims=True)
        h_ref[...] = (x * lax.rsqrt(ms + EPS) * g_ref[...]).astype(BF16)

    o_ref[...] = jnp.dot(h_ref[...], w_ref[...], preferred_element_type=F32).astype(o_ref.dtype)


def _in_proj(x2, g, w, out_dtype, tm, tn):
    m, d = x2.shape
    n = w.shape[1]
    return pl.pallas_call(
        _in_proj_kernel,
        grid=(m // tm, n // tn),
        in_specs=[pl.BlockSpec((tm, d), lambda i, j: (i, 0)),
                  pl.BlockSpec((1, d), lambda i, j: (0, 0)),
                  pl.BlockSpec((d, tn), lambda i, j: (0, j))],
        out_specs=pl.BlockSpec((tm, tn), lambda i, j: (i, j)),
        out_shape=jax.ShapeDtypeStruct((m, n), out_dtype),
        scratch_shapes=[pltpu.VMEM((tm, d), BF16)],
        compiler_params=pltpu.CompilerParams(dimension_semantics=("parallel", "arbitrary"),
                                             vmem_limit_bytes=VMEM_LIMIT),
        name="in_proj",
    )(x2, g, w)


def _rms(x, width):
    return lax.rsqrt(jnp.sum(x * x, axis=-1, keepdims=True) * (1.0 / width) + EPS)


def _prologue_kernel(dsq_ref, cq_ref, ckv_ref, kpe_ref, kper_ref, dsk_ref, cos_ref, sin_ref,
                     wuq_ref, wuqr_ref, wuk_ref, wuv_ref,
                     qng_ref, kvng_ref, gq_ref, gqr_ref, gk_ref, gkr_ref, gdq_ref, gdk_ref,
                     qb_ref, kb_ref, vb_ref, dq_ref, dkk_ref):
    cos = cos_ref[...]
    sin = sin_ref[...]

    cq = cq_ref[...]
    cqn = (cq * _rms(cq, MLA_Q_LORA) * qng_ref[...]).astype(BF16)
    q = jnp.dot(cqn, wuq_ref[...], preferred_element_type=F32)
    qr = jnp.dot(cqn, wuqr_ref[...], preferred_element_type=F32)
    cg = cos * gq_ref[...]
    sg = sin * gqr_ref[...]
    for h in range(N_HEADS):
        sl = slice(h * LANES, (h + 1) * LANES)
        qh = q[:, sl]
        qb_ref[:, sl] = ((qh * cg + qr[:, sl] * sg) * _rms(qh, MLA_QK)).astype(BF16)

    ckv = ckv_ref[...]
    ckvn = (ckv * _rms(ckv, MLA_KV_LORA) * kvng_ref[...]).astype(BF16)
    kn = jnp.dot(ckvn, wuk_ref[...], preferred_element_type=F32)
    vb_ref[...] = jnp.dot(ckvn, wuv_ref[...], preferred_element_type=F32).astype(BF16)
    kpe = kpe_ref[...]
    cgk = cos * gk_ref[...]
    rot = kper_ref[...] * (sin * gkr_ref[...])
    for h in range(N_HEADS):
        sl = slice(h * LANES, (h + 1) * LANES)
        kf = kn[:, sl] + kpe
        kb_ref[:, sl] = ((kf * cgk + rot) * _rms(kf, MLA_QK)).astype(BF16)

    for h in range(N_HEADS):
        sl = slice(h * LANES, (h + 1) * LANES)
        xq = dsq_ref[:, sl]
        dq_ref[:, sl] = (xq * _rms(xq, HEAD_DIM) * gdq_ref[:, sl]).astype(BF16)
    xk = dsk_ref[...]
    dkk_ref[...] = (xk * _rms(xk, 2 * HEAD_DIM) * gdk_ref[...]).astype(BF16)


def _prologue(pf, cos_t, sin_t, lw, tm):
    m = pf.shape[0]

    def act(width, off):
        return pl.BlockSpec((tm, width), lambda i, _b=off // width: (i, _b))

    def whole(a):
        return pl.BlockSpec(a.shape, lambda i: (0,) * a.ndim)

    weights = [lw[k] for k in ("wuq", "wuqr", "wuk", "wuv", "qng", "kvng", "gq", "gqr", "gk", "gkr", "gdq", "gdk")]
    row = lambda width: pl.BlockSpec((tm, width), lambda i: (i, 0))
    return pl.pallas_call(
        _prologue_kernel,
        grid=(m // tm,),
        in_specs=[act(1024, F_DSQ), act(256, F_CQ), act(128, F_CKV), act(128, F_KPE), act(128, F_KPER),
                  act(128, F_DSK), row(LANES), row(LANES)] + [whole(w) for w in weights],
        out_specs=[row(1024), row(1024), row(512), row(1024), row(LANES)],
        out_shape=[jax.ShapeDtypeStruct((m, 1024), BF16), jax.ShapeDtypeStruct((m, 1024), BF16),
                   jax.ShapeDtypeStruct((m, 512), BF16), jax.ShapeDtypeStruct((m, 1024), BF16),
                   jax.ShapeDtypeStruct((m, LANES), BF16)],
        compiler_params=pltpu.CompilerParams(dimension_semantics=("parallel",), vmem_limit_bytes=VMEM_LIMIT),
        name="prologue",
    )(pf, pf, pf, pf, pf, pf, cos_t, sin_t, *weights)


def _sb_kernel(q_ref, k_ref, v_ref, o_ref, acc_ref, carry_ref, hl_ref, lb_ref, *, tq):
    t = LANES
    qi = pl.program_id(1)
    jj = lax.broadcasted_iota(I32, (t, 2 * t), 0)
    ss = lax.broadcasted_iota(I32, (t, 2 * t), 1)
    tri_ones = jnp.logical_or(ss >= t, jj > ss).astype(BF16)
    q_idx = qi * tq + lax.broadcasted_iota(I32, (tq, t), 0)
    k_lane = lax.broadcasted_iota(I32, (tq, t), 1)

    acc_ref[...] = jnp.zeros(acc_ref.shape, F32)
    carry_ref[...] = jnp.zeros(carry_ref.shape, F32)

    def cond(c):
        kb, cmax = c
        return jnp.logical_and(kb >= 0, cmax > -SB_UNDERFLOW)

    def body(c):
        kb, _ = c
        start = pl.multiple_of(kb * t, t)
        mask = (start + k_lane) < q_idx
        for h in range(N_HEADS):
            q = q_ref[0, :, h * LANES:(h + 1) * LANES]
            k = k_ref[0, pl.ds(start, t), (h // 2) * LANES:(h // 2 + 1) * LANES]
            z = lax.dot_general(q, k, NT_DIMS, preferred_element_type=F32)
            sp = jnp.maximum(z, 0.0) + jnp.log(1.0 + jnp.exp(-jnp.abs(z)))
            l1m = jnp.where(mask, -sp, 0.0)
            hi = l1m.astype(BF16)
            hl_ref[2 * h * tq:(2 * h + 1) * tq, :] = hi
            hl_ref[(2 * h + 1) * tq:(2 * h + 2) * tq, :] = (l1m - hi.astype(F32)).astype(BF16)
            lb_ref[h * tq:(h + 1) * tq, :] = z - sp
        sums = jnp.dot(hl_ref[...], tri_ones, preferred_element_type=F32)
        for h in range(N_HEADS):
            rows = slice(h * tq, (h + 1) * tq)
            both = sums[2 * h * tq:(2 * h + 1) * tq, :] + sums[(2 * h + 1) * tq:(2 * h + 2) * tq, :]
            carry = carry_ref[rows, :]
            a = jnp.where(mask, jnp.exp(lb_ref[rows, :] + carry + both[:, :t]), 0.0)
            v = v_ref[0, pl.ds(start, t), (h // 2) * LANES:(h // 2 + 1) * LANES]
            acc_ref[rows, :] += jnp.dot(a.astype(BF16), v, preferred_element_type=F32)
            carry_ref[rows, :] = carry + both[:, t:]
        return kb - 1, jnp.max(carry_ref[...])

    lax.while_loop(cond, body, (((qi + 1) * tq) // t - 1, jnp.float32(0.0)))

    lane = lax.broadcasted_iota(I32, (tq, LANES), 1)
    for p in range(N_HEADS // 2):
        even = acc_ref[2 * p * tq:(2 * p + 1) * tq, :]
        odd = acc_ref[(2 * p + 1) * tq:(2 * p + 2) * tq, :]
        o_ref[0, :, p * LANES:(p + 1) * LANES] = jnp.where(lane < HEAD_DIM, even, odd)


def _sb_attention(pb, b, s, tq):
    pb3 = pb.reshape(b, s, N_BF)
    kernel = functools.partial(_sb_kernel, tq=tq)
    return pl.pallas_call(
        kernel,
        grid=(b, s // tq),
        in_specs=[pl.BlockSpec((1, tq, N_HEADS * LANES), lambda bi, i: (bi, i, B_SAQ // (N_HEADS * LANES))),
                  pl.BlockSpec((1, s, BRANCH_WIDTH), lambda bi, i: (bi, 0, B_SAK // BRANCH_WIDTH)),
                  pl.BlockSpec((1, s, BRANCH_WIDTH), lambda bi, i: (bi, 0, B_SAV // BRANCH_WIDTH))],
        out_specs=pl.BlockSpec((1, tq, BRANCH_WIDTH), lambda bi, i: (bi, i, 0)),
        out_shape=jax.ShapeDtypeStruct((b, s, BRANCH_WIDTH), F32),
        scratch_shapes=[pltpu.VMEM((N_HEADS * tq, LANES), F32), pltpu.VMEM((N_HEADS * tq, LANES), F32),
                        pltpu.VMEM((2 * N_HEADS * tq, LANES), BF16), pltpu.VMEM((N_HEADS * tq, LANES), F32)],
        compiler_params=pltpu.CompilerParams(dimension_semantics=("parallel", "arbitrary"),
                                             vmem_limit_bytes=VMEM_LIMIT),
        name="sb_attention",
    )(pb3, pb3, pb3)


def _mla_kernel(q_ref, k_ref, v_ref, o_ref, m_ref, l_ref, acc_ref, *, tq, tk):
    qi = pl.program_id(2)
    nj = tk // LANES
    n_full = (qi * tq) // tk
    row = qi * tq + lax.broadcasted_iota(I32, (tq, tk), 0)
    col = lax.broadcasted_iota(I32, (tq, tk), 1)

    m_ref[...] = jnp.full(m_ref.shape, NEG_BIG, F32)
    l_ref[...] = jnp.zeros(l_ref.shape, F32)
    acc_ref[...] = jnp.zeros(acc_ref.shape, F32)

    def step(c, diagonal):
        start = pl.multiple_of(c * tk, tk)
        v = v_ref[0, pl.ds(start, tk), :]
        for h in range(2):
            rows = slice(h * tq, (h + 1) * tq)
            q = q_ref[0, :, h * LANES:(h + 1) * LANES]
            k = k_ref[0, pl.ds(start, tk), h * LANES:(h + 1) * LANES]
            s = lax.dot_general(q, k, NT_DIMS, preferred_element_type=F32)
            if diagonal:
                s = jnp.where(start + col <= row, s, NEG_BIG)
            m_old = m_ref[rows, :]
            m_new = jnp.maximum(m_old, jnp.max(s, axis=1, keepdims=True))
            ps = [jnp.exp(s[:, j * LANES:(j + 1) * LANES] - m_new) for j in range(nj)]
            psum = functools.reduce(lambda a, b: a + b, ps)
            alpha = jnp.exp(m_old - m_new)
            l_ref[rows, :] = alpha * l_ref[rows, :] + jnp.sum(psum, axis=1, keepdims=True)
            p = jnp.concatenate(ps, axis=1).astype(BF16)
            acc_ref[rows, :] = alpha * acc_ref[rows, :] + jnp.dot(p, v, preferred_element_type=F32)
            m_ref[rows, :] = m_new

    def body(c, carry):
        step(c, False)
        return carry

    lax.fori_loop(0, n_full, body, 0)
    step(n_full, True)

    lane = lax.broadcasted_iota(I32, (tq, LANES), 1)
    even = acc_ref[0:tq, :] / l_ref[0:tq, :]
    odd = acc_ref[tq:2 * tq, :] / l_ref[tq:2 * tq, :]
    o_ref[0] = jnp.where(lane < HEAD_DIM, even, odd)


def _mla_attention(qb, kb, vb, b, s, tq, tk):
    kernel = functools.partial(_mla_kernel, tq=tq, tk=tk)
    return pl.pallas_call(
        kernel,
        grid=(b, N_HEADS // 2, s // tq),
        in_specs=[pl.BlockSpec((1, tq, 2 * LANES), lambda bi, p, i: (bi, i, p)),
                  pl.BlockSpec((1, s, 2 * LANES), lambda bi, p, i: (bi, 0, p)),
                  pl.BlockSpec((1, s, LANES), lambda bi, p, i: (bi, 0, p))],
        out_specs=pl.BlockSpec((1, tq, LANES), lambda bi, p, i: (bi, i, p)),
        out_shape=jax.ShapeDtypeStruct((b, s, BRANCH_WIDTH), F32),
        scratch_shapes=[pltpu.VMEM((2 * tq, LANES), F32), pltpu.VMEM((2 * tq, LANES), F32),
                        pltpu.VMEM((2 * tq, LANES), F32)],
        compiler_params=pltpu.CompilerParams(dimension_semantics=("parallel", "parallel", "arbitrary"),
                                             vmem_limit_bytes=VMEM_LIMIT),
        name="mla_attention",
    )(qb.reshape(b, s, -1), kb.reshape(b, s, -1), vb.reshape(b, s, -1))


def _sortable(x):
    bits = pltpu.bitcast(x + 0.0, I32)
    return bits ^ ((bits >> 31) & 0x7FFFFFFF)


def _sublane_all(x, op):
    for shift in (4, 2, 1):
        x = op(x, pltpu.roll(x, shift, 0))
    return x


def _dsa_kernel(rel_ref, tab_ref, ixq_ref, ixw_ref, ixk_ref, q_ref, kk_ref, vvt_ref, pos_ref, o_ref,
                keys_ref, qm_ref, qs_ref, wt_ref, s_ref, p_ref, selb_ref, add_ref,
                m_ref, l_ref, alpha_ref, acc_ref, thr_ref, *, tq, tc, topk, seq):
    t = LANES
    assert tq == t
    groups = tc // 8
    width = N_HEADS * tq
    qi = pl.program_id(1)
    n_chunks = (qi * tq) // tc + 1
    head_cols = [slice(h * tq, (h + 1) * tq) for h in range(N_HEADS)]
    lane = lax.broadcasted_iota(I32, (tq, t), 1)
    q_lane = qi * tq + lax.broadcasted_iota(I32, (groups, 8, tq), 2)
    k_in_chunk = (lax.broadcasted_iota(I32, (groups, 8, tq), 0) * 8
                  + lax.broadcasted_iota(I32, (groups, 8, tq), 1))

    for h in range(N_HEADS):
        grp = ixq_ref[0, :, (h // 4) * LANES:(h // 4 + 1) * LANES]
        lo = (h % 4) * IDX_DIM
        qm_ref[h * tq:(h + 1) * tq, :] = jnp.where((lane >= lo) & (lane < lo + IDX_DIM), grp, jnp.zeros_like(grp))
        qs_ref[h * tq:(h + 1) * tq, :] = q_ref[0, :, h * LANES:(h + 1) * LANES]
    wt_ref[...] = jnp.transpose(ixw_ref[0] * ((IDX_DIM ** -0.5) * (N_HEADS ** -0.5)))

    def chunk3(ref, c):
        start = pl.multiple_of(c * tc, tc)
        return ref[pl.ds(start, tc), :].reshape(groups, 8, tq)

    def score_chunk(c, carry):
        start = pl.multiple_of(c * tc, tc)
        st = lax.dot_general(ixk_ref[0, pl.ds(start, tc), :], qm_ref[...], NT_DIMS, preferred_element_type=F32)
        sc = None
        for h in range(N_HEADS):
            term = jnp.maximum(st[:, head_cols[h]], 0.0) * wt_ref[h:h + 1, :]
            sc = term if sc is None else sc + term
        key = jnp.where(start + k_in_chunk <= q_lane, _sortable(sc).reshape(groups, 8, tq), INT_MIN)
        keys_ref[pl.ds(start, tc), :] = key.reshape(tc, tq)
        return carry

    lax.fori_loop(0, n_chunks, score_chunk, 0)

    def count(pred):
        def body(c, cnt):
            return cnt + jnp.sum(pred(chunk3(keys_ref, c), c * tc).astype(I32), axis=0)

        return _sublane_all(lax.fori_loop(0, n_chunks, body, jnp.zeros((8, tq), I32)), jnp.add)

    @pl.when((qi + 1) * tq <= topk)
    def _():
        thr_ref[0] = jnp.full((8, tq), INT_MIN, I32)
        thr_ref[1] = jnp.full((8, tq), -1, I32)

    @pl.when((qi + 1) * tq > topk)
    def _():
        ans = jnp.zeros((8, tq), I32)
        n_ge = jnp.full((8, tq), seq, I32)
        for bit in range(31, -1, -1):
            cand = ans | np.int32(INT_MIN if bit == 31 else 1 << bit)
            cand_s = cand ^ np.int32(INT_MIN)
            c = count(lambda keys, start, cand_s=cand_s: keys >= cand_s[None])
            take = c >= topk
            ans = jnp.where(take, cand, ans)
            n_ge = jnp.where(take, c, n_ge)
        tau = ans ^ np.int32(INT_MIN)
        thr_ref[0] = tau
        thr_ref[1] = jnp.full((8, tq), seq, I32)

        @pl.when(jnp.max(n_ge) > topk)
        def _():
            need = topk - count(lambda keys, start: keys > tau[None])
            cut = jnp.zeros((8, tq), I32)
            for bit in range(int(math.log2(seq)), -1, -1):
                cand = cut | np.int32(1 << bit)
                c = count(lambda keys, start, cand=cand: (keys == tau[None]) & (start + k_in_chunk < cand[None]))
                cut = jnp.where(c < need, cand, cut)
            thr_ref[1] = cut

    tau = thr_ref[0]
    cut = thr_ref[1]

    m_ref[...] = jnp.full(m_ref.shape, NEG_BIG, F32)
    l_ref[...] = jnp.zeros(l_ref.shape, F32)
    acc_ref[...] = jnp.zeros(acc_ref.shape, F32)
    pq = pos_ref[0, qi]
    pq_min = jnp.min(pq)
    last_bias = [rel_ref[REL_BUCKETS - 1, h] for h in range(N_HEADS)]

    def scores(c):
        start = pl.multiple_of(c * tc, tc)
        return lax.dot_general(kk_ref[0, pl.ds(start, tc), :], qs_ref[...], NT_DIMS, preferred_element_type=F32)

    def softmax_head(h, cur, addend, bias):
        cols = head_cols[h]
        sb = s_ref[cur, :, cols].reshape(groups, 8, tq) + addend
        m_old = m_ref[:, cols]
        m_new = jnp.maximum(m_old, _sublane_all(jnp.max(sb, axis=0), jnp.maximum) + bias)
        p = jnp.exp(sb - (m_new - bias)[None])
        alpha = jnp.exp(m_old - m_new)
        l_ref[:, cols] = alpha * l_ref[:, cols] + _sublane_all(jnp.sum(p, axis=0), jnp.add)
        p_ref[:, cols] = p.reshape(tc, tq).astype(BF16)
        alpha_ref[:, cols] = alpha
        m_ref[:, cols] = m_new

    def accumulate(c, half):
        cols = slice(half * (width // 2), (half + 1) * (width // 2))
        pv = jnp.dot(vvt_ref[0, c], p_ref[:, cols], preferred_element_type=F32)
        scaled = acc_ref[:, cols].reshape(t // 8, 8, width // 2) * alpha_ref[:, cols][None]
        acc_ref[:, cols] = scaled.reshape(t, width // 2) + pv

    def all_heads(c, cur, addend_of, bias_of):
        s_ref[1 - cur] = scores(jnp.minimum(c + 1, n_chunks - 1))
        for half in range(2):
            for h in range(half * N_HEADS // 2, (half + 1) * N_HEADS // 2):
                softmax_head(h, cur, addend_of(h), bias_of(h))
            accumulate(c, half)

    s_ref[0] = scores(0)

    def attend_chunk(c, carry):
        cur = lax.rem(c, 2)
        keys = chunk3(keys_ref, c)
        sel = (keys > tau[None]) | ((keys == tau[None]) & (k_in_chunk <= (cut - c * tc)[None]))
        selb_ref[...] = jnp.where(sel, 0.0, NEG_BIG).reshape(tc, tq)
        first_tile = c * (tc // t)
        pk_max = pos_ref[0, first_tile]
        for j in range(1, tc // t):
            pk_max = jnp.maximum(pk_max, pos_ref[0, first_tile + j])
        chunk_far = pq_min - jnp.max(pk_max) >= FAR_DIST

        @pl.when(chunk_far)
        def _():
            all_heads(c, cur, lambda h: selb_ref[...].reshape(groups, 8, tq), lambda h: last_bias[h])

        @pl.when(jnp.logical_not(chunk_far))
        def _():
            for j in range(tc // t):
                rows = slice(j * t, (j + 1) * t)
                pk = pos_ref[0, first_tile + j]
                tile_far = pq_min - jnp.max(pk) >= FAR_DIST

                @pl.when(tile_far)
                def _():
                    for h in range(N_HEADS):
                        add_ref[rows, head_cols[h]] = selb_ref[rows, :] + last_bias[h]

                @pl.when(jnp.logical_not(tile_far))
                def _():
                    pk_col = jnp.transpose(jnp.broadcast_to(pk, (t, t)))
                    dist = jnp.clip(pq - pk_col, 0, t - 1)
                    for h in range(N_HEADS):
                        table = jnp.broadcast_to(tab_ref[h:h + 1, :], (t, t))
                        bias = jnp.take_along_axis(table, dist, axis=1, mode="promise_in_bounds")
                        add_ref[rows, head_cols[h]] = selb_ref[rows, :] + bias

            all_heads(c, cur, lambda h: add_ref[:, head_cols[h]].reshape(groups, 8, tq), lambda h: 0.0)

        return carry

    lax.fori_loop(0, n_chunks, attend_chunk, 0)

    for p in range(N_HEADS // 2):
        outs = []
        for h in (2 * p, 2 * p + 1):
            cols = head_cols[h]
            o_t = acc_ref[:, cols].reshape(t // 8, 8, tq) / l_ref[:, cols][None]
            outs.append(jnp.transpose(o_t.reshape(t, tq)))
        o_ref[0, :, p * LANES:(p + 1) * LANES] = jnp.where(lane < HEAD_DIM, outs[0], outs[1])


def _dsa_attention(rel_bias, dist_table, pb, pf, dq, dkk, vvt, pos_rows, b, s, tq, tc):
    topk = min(TOPK_MAX, s // 4)
    pb3 = pb.reshape(b, s, N_BF)
    pf3 = pf.reshape(b, s, N_F32)
    width = N_HEADS * tq
    kernel = functools.partial(_dsa_kernel, tq=tq, tc=tc, topk=topk, seq=s)
    return pl.pallas_call(
        kernel,
        grid=(b, s // tq),
        in_specs=[pl.BlockSpec(memory_space=pltpu.SMEM),
                  pl.BlockSpec(dist_table.shape, lambda bi, i: (0, 0)),
                  pl.BlockSpec((1, tq, 2 * LANES), lambda bi, i: (bi, i, B_IXQ // 256)),
                  pl.BlockSpec((1, tq, LANES), lambda bi, i: (bi, i, F_IXW // LANES)),
                  pl.BlockSpec((1, s, LANES), lambda bi, i: (bi, 0, B_IXK // LANES)),
                  pl.BlockSpec((1, tq, N_HEADS * LANES), lambda bi, i: (bi, i, 0)),
                  pl.BlockSpec((1, s, LANES), lambda bi, i: (bi, 0, 0)),
                  pl.BlockSpec((1, s // tc, LANES, tc), lambda bi, i: (bi, 0, 0, 0)),
                  pl.BlockSpec((1, s // LANES, 1, LANES), lambda bi, i: (bi, 0, 0, 0))],
        out_specs=pl.BlockSpec((1, tq, BRANCH_WIDTH), lambda bi, i: (bi, i, 0)),
        out_shape=jax.ShapeDtypeStruct((b, s, BRANCH_WIDTH), F32),
        scratch_shapes=[pltpu.VMEM((s, tq), I32),
                        pltpu.VMEM((width, LANES), BF16),
                        pltpu.VMEM((width, LANES), BF16),
                        pltpu.VMEM((LANES, tq), F32),
                        pltpu.VMEM((2, tc, width), F32),
                        pltpu.VMEM((tc, width), BF16),
                        pltpu.VMEM((tc, tq), F32),
                        pltpu.VMEM((tc, width), F32),
                        pltpu.VMEM((8, width), F32),
                        pltpu.VMEM((8, width), F32),
                        pltpu.VMEM((8, width), F32),
                        pltpu.VMEM((LANES, width), F32),
                        pltpu.VMEM((2, 8, tq), I32)],
        compiler_params=pltpu.CompilerParams(dimension_semantics=("parallel", "arbitrary"),
                                             vmem_limit_bytes=VMEM_LIMIT),
        name="dsa_attention",
    )(rel_bias, dist_table, pb3, pf3, pb3, dq.reshape(b, s, -1), dkk.reshape(b, s, -1), vvt, pos_rows)


def _merge_kernel(x_ref, ya_ref, yb_ref, yc_ref, za_ref, zb_ref, zc_ref, ga_ref, gb_ref, gc_ref,
                  bias_ref, wbr_ref, wout_ref, o_ref):
    merged = None
    branches = ((ya_ref, za_ref, ga_ref), (yb_ref, zb_ref, gb_ref), (yc_ref, zc_ref, gc_ref))
    for n, (y_ref, z_ref, g_ref) in enumerate(branches):
        z = z_ref[...].astype(F32)
        branch = (y_ref[...] * (z * jax.nn.sigmoid(z))).astype(BF16)
        up = jnp.dot(branch, wbr_ref[n], preferred_element_type=F32)
        gate = jax.nn.sigmoid(g_ref[...].astype(F32) + bias_ref[n:n + 1, :])
        merged = gate * up if merged is None else merged + gate * up
    o_ref[...] = x_ref[...] + jnp.dot(merged.astype(BF16), wout_ref[...], preferred_element_type=F32)


def _merge(x2, ya, yb, yc, pb, gate_b, wbr, wout, tm):
    m, d = x2.shape
    row = lambda width: pl.BlockSpec((tm, width), lambda i: (i, 0))
    col = lambda width, off: pl.BlockSpec((tm, width), lambda i, _b=off // width: (i, _b))
    z_specs = [col(BRANCH_WIDTH, B_Z + n * BRANCH_WIDTH) for n in range(N_BRANCH)]
    g_specs = [col(d, B_G + n * d) for n in range(N_BRANCH)]
    return pl.pallas_call(
        _merge_kernel,
        grid=(m // tm,),
        in_specs=[row(d), row(BRANCH_WIDTH), row(BRANCH_WIDTH), row(BRANCH_WIDTH)] + z_specs + g_specs + [
            pl.BlockSpec(gate_b.shape, lambda i: (0, 0)),
            pl.BlockSpec(wbr.shape, lambda i: (0, 0, 0)),
            pl.BlockSpec(wout.shape, lambda i: (0, 0))],
        out_specs=row(d),
        out_shape=jax.ShapeDtypeStruct((m, d), F32),
        compiler_params=pltpu.CompilerParams(dimension_semantics=("parallel",), vmem_limit_bytes=VMEM_LIMIT),
        name="merge",
    )(x2, ya, yb, yc, pb, pb, pb, pb, pb, pb, gate_b, wbr, wout)


def _lane_table(vals_by_lane):
    t = jnp.zeros((LANES,), F32)
    for start, v in vals_by_lane.items():
        t = t.at[start:start + v.shape[0]].set(v)
    return t[None, :]


def _layer_weights(l, w_bf, w_f32, norm_g, mla_q_norm_g, mla_kv_norm_g, mla_w_uq, mla_w_ukv, mla_q_g, mla_k_g,
                   dsa_q_g, dsa_k_g, gate_b, w_branch, w_out):
    uq, ukv = mla_w_uq[l], mla_w_ukv[l]
    half = MLA_ROPE // 2
    q_pieces, qr_pieces, k_pieces, v_pieces = [], [], [], []
    for h in range(N_HEADS):
        o = h * MLA_QK
        q_pieces += [(o, MLA_QK, 1.0), (None, LANES - MLA_QK, 1.0)]
        qr_pieces += [(None, MLA_NOPE, 1.0), (o + MLA_NOPE + half, half, 1.0), (o + MLA_NOPE, half, 1.0),
                      (None, LANES - MLA_QK, 1.0)]
        k_pieces += [(h * 128, MLA_NOPE, 1.0), (None, LANES - MLA_NOPE, 1.0)]
        v_pieces += [(h * 128 + MLA_NOPE, 64, 1.0)]
    scale = MLA_QK ** -0.5
    qg, kg = mla_q_g[l], mla_k_g[l]
    rot_gain = lambda g: jnp.concatenate([g[MLA_NOPE + half:], g[MLA_NOPE:MLA_NOPE + half]])
    dq_g = jnp.concatenate([jnp.concatenate([dsa_q_g[l], jnp.zeros((HEAD_DIM,), F32)]) if h % 2 == 0 else
                            jnp.concatenate([jnp.zeros((HEAD_DIM,), F32), dsa_q_g[l]]) for h in range(N_HEADS)])
    return dict(
        norm_g=norm_g[l][None, :], w_bf=w_bf[l], w_f32=w_f32[l],
        wuq=_assemble(uq, q_pieces).astype(BF16), wuqr=_assemble(uq, qr_pieces).astype(BF16),
        wuk=_assemble(ukv, k_pieces).astype(BF16), wuv=_assemble(ukv, v_pieces).astype(BF16),
        qng=mla_q_norm_g[l][None, :], kvng=mla_kv_norm_g[l][None, :],
        gq=_lane_table({0: qg * scale}), gqr=_lane_table({MLA_NOPE: rot_gain(qg) * scale}),
        gk=_lane_table({0: kg}), gkr=_lane_table({MLA_NOPE: rot_gain(kg)}),
        gdq=(dq_g * HEAD_DIM ** -0.5)[None, :], gdk=jnp.concatenate([dsa_k_g[l], dsa_k_g[l]])[None, :],
        gate_b=gate_b[l], wbr=w_branch[l].astype(BF16), wout=w_out[l].astype(BF16))


def kernel(x, positions, norm_g, w_in, mla_q_norm_g, mla_kv_norm_g, mla_w_uq, mla_w_ukv, mla_q_g, mla_k_g,
           dsa_q_g, dsa_k_g, rel_bias, gate_b, w_branch, w_out):
    b, s, d = x.shape
    depth = w_in.shape[0]
    m = b * s
    assert d == D_MODEL and w_in.shape[-1] == D_IN and s % 512 == 0 and s & (s - 1) == 0

    half = MLA_ROPE // 2
    inv_freq = ROPE_THETA ** (-(jnp.arange(half, dtype=F32) * 2.0) / MLA_ROPE)
    ang = positions.astype(F32)[..., None] * inv_freq
    cos, sin = jnp.cos(ang).reshape(m, half), jnp.sin(ang).reshape(m, half)
    cos_t = jnp.concatenate([jnp.ones((m, MLA_NOPE), F32), cos, cos, jnp.ones((m, LANES - MLA_QK), F32)], axis=1)
    sin_t = jnp.concatenate([jnp.zeros((m, MLA_NOPE), F32), -sin, sin, jnp.zeros((m, LANES - MLA_QK), F32)], axis=1)

    bf_pieces, f32_pieces = _in_proj_layouts()
    w_bf = _assemble(w_in, bf_pieces).astype(BF16)
    w_f32 = _assemble(w_in, f32_pieces).astype(BF16)

    tm = min(1024, m)
    tp = min(512, m)
    dsa_tc = 512
    pos_rows = positions.reshape(b, s // LANES, 1, LANES)
    dist_table = rel_bias[BUCKET_OF_DIST, :].T
    x2 = x.reshape(m, d)
    for l in range(depth):
        lw = _layer_weights(l, w_bf, w_f32, norm_g, mla_q_norm_g, mla_kv_norm_g, mla_w_uq, mla_w_ukv,
                            mla_q_g, mla_k_g, dsa_q_g, dsa_k_g, gate_b, w_branch, w_out)
        pb = _in_proj(x2, lw["norm_g"], lw["w_bf"], BF16, tm, 512)
        pf = _in_proj(x2, lw["norm_g"], lw["w_f32"], F32, tm, 512)
        qb, kb, vb, dq, dkk = _prologue(pf, cos_t, sin_t, lw, tp)
        ya = _sb_attention(pb, b, s, 128)
        yb = _mla_attention(qb, kb, vb, b, s, 256, 512)
        vvt = pb.reshape(b, s // dsa_tc, dsa_tc, N_BF)[..., B_DSV:B_DSV + LANES].transpose(0, 1, 3, 2)
        yc = _dsa_attention(rel_bias, dist_table, pb, pf, dq, dkk, vvt, pos_rows, b, s, LANES, dsa_tc)
        x2 = _merge(x2, ya.reshape(m, -1), yb.reshape(m, -1), yc.reshape(m, -1), pb, lw["gate_b"],
                    lw["wbr"], lw["wout"], tp)
    return x2.reshape(b, s, d)
```

```python
import functools
import math

import numpy as np
import jax
import jax.numpy as jnp
from jax import lax
from jax.experimental import pallas as pl
from jax.experimental.pallas import tpu as pltpu

F32 = jnp.float32
BF16 = jnp.bfloat16
I32 = jnp.int32

LANES = 128
EPS = 1e-6
D_MODEL = 1024
N_HEADS = 8
HEAD_DIM = 64
MLA_Q_LORA = 256
MLA_KV_LORA = 128
MLA_NOPE = 64
MLA_ROPE = 32
MLA_QK = MLA_NOPE + MLA_ROPE
ROPE_THETA = 10000.0
IDX_DIM = 32
TOPK_MAX = 256
REL_BUCKETS = 32
REL_MAX_DIST = 128
N_BRANCH = 3
BRANCH_WIDTH = 512

O_SAQ, O_SAK, O_SAV = 0, 512, 1024
O_CQ, O_CKV, O_KPE = 1536, 1792, 1920
O_DSQ, O_DSK, O_DSV = 1952, 2464, 2528
O_IXQ, O_IXK, O_IXW = 2592, 2848, 2880
O_Z, O_G = 2888, 4424
D_IN = 7496

NEG_BIG = -1e30
INT_MIN = -(2 ** 31)
SB_UNDERFLOW = 104.0
VMEM_LIMIT = 48 * 1024 * 1024

NT_DIMS = (((1,), (1,)), ((), ()))


def _t5_buckets():
    exact = REL_BUCKETS // 2
    n = np.arange(0, 4 * REL_MAX_DIST)
    nf = np.maximum(n, 1).astype(np.float64)
    large = exact + (np.log(nf / exact) / math.log(REL_MAX_DIST / exact) * (REL_BUCKETS - exact)).astype(np.int64)
    bucket = np.where(n < exact, n, np.minimum(large, REL_BUCKETS - 1))
    assert np.all(np.diff(bucket) >= 0) and np.all(bucket[LANES - 1:] == REL_BUCKETS - 1)
    return bucket[:LANES]


BUCKET_OF_DIST = _t5_buckets()
FAR_DIST = int(np.argmax(BUCKET_OF_DIST == REL_BUCKETS - 1))


def _assemble(w, pieces):
    cols = []
    for start, width, scale in pieces:
        if start is None:
            cols.append(jnp.zeros(w.shape[:-1] + (width,), w.dtype))
        else:
            c = w[..., start:start + width]
            cols.append(c if scale == 1.0 else c * scale)
    return jnp.concatenate(cols, axis=-1)


def _head_padded(start, scale=1.0):
    pieces = []
    for h in range(N_HEADS):
        col = (start + h * HEAD_DIM, HEAD_DIM, scale)
        pad = (None, HEAD_DIM, 1.0)
        pieces += [col, pad] if h % 2 == 0 else [pad, col]
    return pieces


def _in_proj_layouts():
    bf = (_head_padded(O_SAQ, HEAD_DIM ** -0.5)
          + [(O_SAK, 512, 1.0), (O_SAV, 512, 1.0), (O_IXQ, 256, 1.0)]
          + [(O_DSV, 64, 1.0)] * 2
          + [(O_IXK, 32, 1.0)] * 4
          + [(O_Z, 1536, 1.0), (O_G, 3072, 1.0)])
    f32 = (_head_padded(O_DSQ)
           + [(O_CQ, 256, 1.0), (O_CKV, 128, 1.0)]
           + [(None, 64, 1.0), (O_KPE, 32, 1.0), (None, 32, 1.0)]
           + [(None, 64, 1.0), (O_KPE + 16, 16, 1.0), (O_KPE, 16, 1.0), (None, 32, 1.0)]
           + [(O_DSK, 64, 1.0)] * 2
           + [(O_IXW, 8, 1.0), (None, 120, 1.0)]
           + [(None, 128, 1.0)])
    return bf, f32


B_SAQ, B_SAK, B_SAV, B_IXQ, B_DSV, B_IXK, B_Z, B_G, N_BF = 0, 1024, 1536, 2048, 2304, 2432, 2560, 4096, 7168
F_DSQ, F_CQ, F_CKV, F_KPE, F_KPER, F_DSK, F_IXW, N_F32 = 0, 1024, 1280, 1408, 1536, 1664, 1792, 2048


def _in_proj_kernel(x_ref, g_ref, w_ref, o_ref, h_ref):
    @pl.when(pl.program_id(1) == 0)
    def _():
        x = x_ref[...]
        ms = jnp.mean(x * x, axis=-1, keepdims=True)
        h_ref[...] = (x * lax.rsqrt(ms + EPS) * g_ref[...]).astype(BF16)

    o_ref[...] = jnp.dot(h_ref[...], w_ref[...], preferred_element_type=F32).astype(o_ref.dtype)


def _in_proj(x2, g, w, out_dtype, tm, tn):
    m, d = x2.shape
    n = w.shape[1]
    return pl.pallas_call(
        _in_proj_kernel,
        grid=(m // tm, n // tn),
        in_specs=[pl.BlockSpec((tm, d), lambda i, j: (i, 0)),
                  pl.BlockSpec((1, d), lambda i, j: (0, 0)),
                  pl.BlockSpec((d, tn), lambda i, j: (0, j))],
        out_specs=pl.BlockSpec((tm, tn), lambda i, j: (i, j)),
        out_shape=jax.ShapeDtypeStruct((m, n), out_dtype),
        scratch_shapes=[pltpu.VMEM((tm, d), BF16)],
        compiler_params=pltpu.CompilerParams(dimension_semantics=("parallel", "arbitrary"),
                                             vmem_limit_bytes=VMEM_LIMIT),
        name="in_proj",
    )(x2, g, w)


def _rms(x, width):
    return lax.rsqrt(jnp.sum(x * x, axis=-1, keepdims=True) * (1.0 / width) + EPS)


def _prologue_kernel(dsq_ref, cq_ref, ckv_ref, kpe_ref, kper_ref, dsk_ref, cos_ref, sin_ref,
                     wuq_ref, wuqr_ref, wuk_ref, wuv_ref,
                     qng_ref, kvng_ref, gq_ref, gqr_ref, gk_ref, gkr_ref, gdq_ref, gdk_ref,
                     qb_ref, kb_ref, vb_ref, dq_ref, dkk_ref):
    cos = cos_ref[...]
    sin = sin_ref[...]

    cq = cq_ref[...]
    cqn = (cq * _rms(cq, MLA_Q_LORA) * qng_ref[...]).astype(BF16)
    q = jnp.dot(cqn, wuq_ref[...], preferred_element_type=F32)
    qr = jnp.dot(cqn, wuqr_ref[...], preferred_element_type=F32)
    cg = cos * gq_ref[...]
    sg = sin * gqr_ref[...]
    for h in range(N_HEADS):
        sl = slice(h * LANES, (h + 1) * LANES)
        qh = q[:, sl]
        qb_ref[:, sl] = ((qh * cg + qr[:, sl] * sg) * _rms(qh, MLA_QK)).astype(BF16)

    ckv = ckv_ref[...]
    ckvn = (ckv * _rms(ckv, MLA_KV_LORA) * kvng_ref[...]).astype(BF16)
    kn = jnp.dot(ckvn, wuk_ref[...], preferred_element_type=F32)
    vb_ref[...] = jnp.dot(ckvn, wuv_ref[...], preferred_element_type=F32).astype(BF16)
    kpe = kpe_ref[...]
    cgk = cos * gk_ref[...]
    rot = kper_ref[...] * (sin * gkr_ref[...])
    for h in range(N_HEADS):
        sl = slice(h * LANES, (h + 1) * LANES)
        kf = kn[:, sl] + kpe
        kb_ref[:, sl] = ((kf * cgk + rot) * _rms(kf, MLA_QK)).astype(BF16)

    for h in range(N_HEADS):
        sl = slice(h * LANES, (h + 1) * LANES)
        xq = dsq_ref[:, sl]
        dq_ref[:, sl] = (xq * _rms(xq, HEAD_DIM) * gdq_ref[:, sl]).astype(BF16)
    xk = dsk_ref[...]
    dkk_ref[...] = (xk * _rms(xk, 2 * HEAD_DIM) * gdk_ref[...]).astype(BF16)


def _prologue(pf, cos_t, sin_t, lw, tm):
    m = pf.shape[0]

    def act(width, off):
        return pl.BlockSpec((tm, width), lambda i, _b=off // width: (i, _b))

    def whole(a):
        return pl.BlockSpec(a.shape, lambda i: (0,) * a.ndim)

    weights = [lw[k] for k in ("wuq", "wuqr", "wuk", "wuv", "qng", "kvng", "gq", "gqr", "gk", "gkr", "gdq", "gdk")]
    row = lambda width: pl.BlockSpec((tm, width), lambda i: (i, 0))
    return pl.pallas_call(
        _prologue_kernel,
        grid=(m // tm,),
        in_specs=[act(1024, F_DSQ), act(256, F_CQ), act(128, F_CKV), act(128, F_KPE), act(128, F_KPER),
                  act(128, F_DSK), row(LANES), row(LANES)] + [whole(w) for w in weights],
        out_specs=[row(1024), row(1024), row(512), row(1024), row(LANES)],
        out_shape=[jax.ShapeDtypeStruct((m, 1024), BF16), jax.ShapeDtypeStruct((m, 1024), BF16),
                   jax.ShapeDtypeStruct((m, 512), BF16), jax.ShapeDtypeStruct((m, 1024), BF16),
                   jax.ShapeDtypeStruct((m, LANES), BF16)],
        compiler_params=pltpu.CompilerParams(dimension_semantics=("parallel",), vmem_limit_bytes=VMEM_LIMIT),
        name="prologue",
    )(pf, pf, pf, pf, pf, pf, cos_t, sin_t, *weights)


def _sublane_all(x, op):
    for shift in (4, 2, 1):
        x = op(x, pltpu.roll(x, shift, 0))
    return x


def _sb_kernel(q_ref, k_ref, v_ref, o_ref, acc_ref, carry_ref, hl_ref, lb_ref, *, tq):
    t = LANES
    qi = pl.program_id(1)
    jj = lax.broadcasted_iota(I32, (t, 2 * t), 0)
    ss = lax.broadcasted_iota(I32, (t, 2 * t), 1)
    tri_ones = jnp.logical_or(ss >= t, jj > ss).astype(BF16)
    q_idx = qi * tq + lax.broadcasted_iota(I32, (tq, t), 0)
    k_lane = lax.broadcasted_iota(I32, (tq, t), 1)

    acc_ref[...] = jnp.zeros(acc_ref.shape, F32)
    carry_ref[...] = jnp.zeros(carry_ref.shape, F32)

    def cond(c):
        kb, cmax = c
        return jnp.logical_and(kb >= 0, cmax > -SB_UNDERFLOW)

    def body(c):
        kb, _ = c
        start = pl.multiple_of(kb * t, t)
        mask = (start + k_lane) < q_idx
        for h in range(N_HEADS):
            q = q_ref[0, :, h * LANES:(h + 1) * LANES]
            k = k_ref[0, pl.ds(start, t), (h // 2) * LANES:(h // 2 + 1) * LANES]
            z = lax.dot_general(q, k, NT_DIMS, preferred_element_type=F32)
            sp = jnp.maximum(z, 0.0) + jnp.log(1.0 + jnp.exp(-jnp.abs(z)))
            l1m = jnp.where(mask, -sp, 0.0)
            hi = l1m.astype(BF16)
            hl_ref[2 * h * tq:(2 * h + 1) * tq, :] = hi
            hl_ref[(2 * h + 1) * tq:(2 * h + 2) * tq, :] = (l1m - hi.astype(F32)).astype(BF16)
            lb_ref[h * tq:(h + 1) * tq, :] = z - sp
        sums = jnp.dot(hl_ref[...], tri_ones, preferred_element_type=F32)
        for h in range(N_HEADS):
            rows = slice(h * tq, (h + 1) * tq)
            both = sums[2 * h * tq:(2 * h + 1) * tq, :] + sums[(2 * h + 1) * tq:(2 * h + 2) * tq, :]
            carry = carry_ref[rows, :]
            a = jnp.where(mask, jnp.exp(lb_ref[rows, :] + carry + both[:, :t]), 0.0)
            v = v_ref[0, pl.ds(start, t), (h // 2) * LANES:(h // 2 + 1) * LANES]
            acc_ref[rows, :] += jnp.dot(a.astype(BF16), v, preferred_element_type=F32)
            carry_ref[rows, :] = carry + both[:, t:]
        return kb - 1, jnp.max(carry_ref[...])

    lax.while_loop(cond, body, (((qi + 1) * tq) // t - 1, jnp.float32(0.0)))

    lane = lax.broadcasted_iota(I32, (tq, LANES), 1)
    for p in range(N_HEADS // 2):
        even = acc_ref[2 * p * tq:(2 * p + 1) * tq, :]
        odd = acc_ref[(2 * p + 1) * tq:(2 * p + 2) * tq, :]
        o_ref[0, :, p * LANES:(p + 1) * LANES] = jnp.where(lane < HEAD_DIM, even, odd)


def _sb_attention(pb, b, s, tq):
    pb3 = pb.reshape(b, s, N_BF)
    kernel = functools.partial(_sb_kernel, tq=tq)
    return pl.pallas_call(
        kernel,
        grid=(b, s // tq),
        in_specs=[pl.BlockSpec((1, tq, N_HEADS * LANES), lambda bi, i: (bi, i, B_SAQ // (N_HEADS * LANES))),
                  pl.BlockSpec((1, s, BRANCH_WIDTH), lambda bi, i: (bi, 0, B_SAK // BRANCH_WIDTH)),
                  pl.BlockSpec((1, s, BRANCH_WIDTH), lambda bi, i: (bi, 0, B_SAV // BRANCH_WIDTH))],
        out_specs=pl.BlockSpec((1, tq, BRANCH_WIDTH), lambda bi, i: (bi, i, 0)),
        out_shape=jax.ShapeDtypeStruct((b, s, BRANCH_WIDTH), F32),
        scratch_shapes=[pltpu.VMEM((N_HEADS * tq, LANES), F32), pltpu.VMEM((N_HEADS * tq, LANES), F32),
                        pltpu.VMEM((2 * N_HEADS * tq, LANES), BF16), pltpu.VMEM((N_HEADS * tq, LANES), F32)],
        compiler_params=pltpu.CompilerParams(dimension_semantics=("parallel", "arbitrary"),
                                             vmem_limit_bytes=VMEM_LIMIT),
        name="sb_attention",
    )(pb3, pb3, pb3)


def _mla_kernel(q_ref, k_ref, vt_ref, o_ref, qbd_ref, p_ref, m_ref, l_ref, alpha_ref, acc_ref, *, tq, tc):
    t = LANES
    assert tq == t
    groups = tc // 8
    pair_w = 2 * tq
    qi = pl.program_id(1)
    n_chunks = (qi * tq) // tc + 1
    k_in_chunk = (lax.broadcasted_iota(I32, (groups, 8, tq), 0) * 8
                  + lax.broadcasted_iota(I32, (groups, 8, tq), 1))
    q_lane = qi * tq + lax.broadcasted_iota(I32, (groups, 8, tq), 2)

    zeros = jnp.zeros((tq, t), BF16)
    for p in range(N_HEADS // 2):
        q_even = q_ref[0, :, 2 * p * LANES:(2 * p + 1) * LANES]
        q_odd = q_ref[0, :, (2 * p + 1) * LANES:(2 * p + 2) * LANES]
        qbd_ref[p, 0:tq, :] = jnp.concatenate([q_even, zeros], axis=1)
        qbd_ref[p, tq:2 * tq, :] = jnp.concatenate([zeros, q_odd], axis=1)

    m_ref[...] = jnp.full(m_ref.shape, NEG_BIG, F32)
    l_ref[...] = jnp.zeros(l_ref.shape, F32)
    acc_ref[...] = jnp.zeros(acc_ref.shape, F32)

    def chunk(c, diagonal):
        start = pl.multiple_of(c * tc, tc)
        pair_cols = [slice(p * pair_w, (p + 1) * pair_w) for p in range(N_HEADS // 2)]
        s_pairs = [lax.dot_general(k_ref[0, pl.ds(start, tc), pair_cols[p]], qbd_ref[p], NT_DIMS,
                                   preferred_element_type=F32) for p in range(N_HEADS // 2)]
        for p in range(N_HEADS // 2):
            pcols = pair_cols[p]
            for i in range(2):
                cols = slice(p * pair_w + i * tq, p * pair_w + (i + 1) * tq)
                sb = s_pairs[p][:, i * tq:(i + 1) * tq].reshape(groups, 8, tq)
                if diagonal:
                    sb = jnp.where(start + k_in_chunk <= q_lane, sb, NEG_BIG)
                m_old = m_ref[:, cols]
                m_new = jnp.maximum(m_old, _sublane_all(jnp.max(sb, axis=0), jnp.maximum))
                prob = jnp.exp(sb - m_new[None])
                alpha = jnp.exp(m_old - m_new)
                l_ref[:, cols] = alpha * l_ref[:, cols] + _sublane_all(jnp.sum(prob, axis=0), jnp.add)
                p_ref[:, cols] = prob.reshape(tc, tq).astype(BF16)
                alpha_ref[:, cols] = alpha
                m_ref[:, cols] = m_new
            pv = jnp.dot(vt_ref[0, c, p], p_ref[:, pcols], preferred_element_type=F32)
            scaled = acc_ref[:, pcols].reshape(t // 8, 8, pair_w) * alpha_ref[:, pcols][None]
            acc_ref[:, pcols] = scaled.reshape(t, pair_w) + pv

    def body(c, carry):
        chunk(c, False)
        return carry

    lax.fori_loop(0, n_chunks - 1, body, 0)
    chunk(n_chunks - 1, True)

    row = lax.broadcasted_iota(I32, (t, tq), 0)
    for p in range(N_HEADS // 2):
        outs = []
        for i in range(2):
            cols = slice(p * pair_w + i * tq, p * pair_w + (i + 1) * tq)
            outs.append((acc_ref[:, cols].reshape(t // 8, 8, tq) / l_ref[:, cols][None]).reshape(t, tq))
        o_ref[0, :, p * LANES:(p + 1) * LANES] = jnp.transpose(jnp.where(row < HEAD_DIM, outs[0], outs[1]))


def _mla_attention(qb, kb, vbt, b, s, tq, tc):
    width = N_HEADS * tq
    kernel = functools.partial(_mla_kernel, tq=tq, tc=tc)
    return pl.pallas_call(
        kernel,
        grid=(b, s // tq),
        in_specs=[pl.BlockSpec((1, tq, N_HEADS * LANES), lambda bi, i: (bi, i, 0)),
                  pl.BlockSpec((1, s, N_HEADS * LANES), lambda bi, i: (bi, 0, 0)),
                  pl.BlockSpec((1, s // tc, N_HEADS // 2, LANES, tc), lambda bi, i: (bi, 0, 0, 0, 0))],
        out_specs=pl.BlockSpec((1, tq, BRANCH_WIDTH), lambda bi, i: (bi, i, 0)),
        out_shape=jax.ShapeDtypeStruct((b, s, BRANCH_WIDTH), F32),
        scratch_shapes=[pltpu.VMEM((N_HEADS // 2, 2 * tq, 2 * LANES), BF16),
                        pltpu.VMEM((tc, width), BF16),
                        pltpu.VMEM((8, width), F32),
                        pltpu.VMEM((8, width), F32),
                        pltpu.VMEM((8, width), F32),
                        pltpu.VMEM((LANES, width), F32)],
        compiler_params=pltpu.CompilerParams(dimension_semantics=("parallel", "arbitrary"),
                                             vmem_limit_bytes=VMEM_LIMIT),
        name="mla_attention",
    )(qb.reshape(b, s, -1), kb.reshape(b, s, -1), vbt)


def _sortable(x):
    bits = pltpu.bitcast(x + 0.0, I32)
    return bits ^ ((bits >> 31) & 0x7FFFFFFF)


def _dsa_kernel(rel_ref, tab_ref, ixq_ref, ixw_ref, ixk_ref, q_ref, kk_ref, vvt_ref, pos_ref, o_ref,
                keys_ref, qm_ref, qs_ref, wt_ref, p_ref, selb_ref, add_ref,
                m_ref, l_ref, alpha_ref, acc_ref, thr_ref, *, tq, tc, topk, seq):
    t = LANES
    assert tq == t
    groups = tc // 8
    width = N_HEADS * tq
    qi = pl.program_id(1)
    n_chunks = (qi * tq) // tc + 1
    head_cols = [slice(h * tq, (h + 1) * tq) for h in range(N_HEADS)]
    lane = lax.broadcasted_iota(I32, (tq, t), 1)
    q_lane = qi * tq + lax.broadcasted_iota(I32, (groups, 8, tq), 2)
    k_in_chunk = (lax.broadcasted_iota(I32, (groups, 8, tq), 0) * 8
                  + lax.broadcasted_iota(I32, (groups, 8, tq), 1))

    for h in range(N_HEADS):
        grp = ixq_ref[0, :, (h // 4) * LANES:(h // 4 + 1) * LANES]
        lo = (h % 4) * IDX_DIM
        qm_ref[h * tq:(h + 1) * tq, :] = jnp.where((lane >= lo) & (lane < lo + IDX_DIM), grp, jnp.zeros_like(grp))
        qs_ref[h * tq:(h + 1) * tq, :] = q_ref[0, :, h * LANES:(h + 1) * LANES]
    wt_ref[...] = jnp.transpose(ixw_ref[0] * ((IDX_DIM ** -0.5) * (N_HEADS ** -0.5)))

    def chunk3(ref, c):
        start = pl.multiple_of(c * tc, tc)
        return ref[pl.ds(start, tc), :].reshape(groups, 8, tq)

    def score_chunk(c, carry):
        start = pl.multiple_of(c * tc, tc)
        st = lax.dot_general(ixk_ref[0, pl.ds(start, tc), :], qm_ref[...], NT_DIMS, preferred_element_type=F32)
        sc = None
        for h in range(N_HEADS):
            term = jnp.maximum(st[:, head_cols[h]], 0.0) * wt_ref[h:h + 1, :]
            sc = term if sc is None else sc + term
        key = jnp.where(start + k_in_chunk <= q_lane, _sortable(sc).reshape(groups, 8, tq), INT_MIN)
        keys_ref[pl.ds(start, tc), :] = key.reshape(tc, tq)
        return carry

    lax.fori_loop(0, n_chunks, score_chunk, 0)

    def count(pred):
        def body(c, cnt):
            return cnt + jnp.sum(pred(chunk3(keys_ref, c), c * tc).astype(I32), axis=0)

        return _sublane_all(lax.fori_loop(0, n_chunks, body, jnp.zeros((8, tq), I32)), jnp.add)

    @pl.when((qi + 1) * tq <= topk)
    def _():
        thr_ref[0] = jnp.full((8, tq), INT_MIN, I32)
        thr_ref[1] = jnp.full((8, tq), -1, I32)

    @pl.when((qi + 1) * tq > topk)
    def _():
        ans = jnp.zeros((8, tq), I32)
        n_ge = jnp.full((8, tq), seq, I32)
        for bit in range(31, -1, -1):
            cand = ans | np.int32(INT_MIN if bit == 31 else 1 << bit)
            cand_s = cand ^ np.int32(INT_MIN)
            c = count(lambda keys, start, cand_s=cand_s: keys >= cand_s[None])
            take = c >= topk
            ans = jnp.where(take, cand, ans)
            n_ge = jnp.where(take, c, n_ge)
        tau = ans ^ np.int32(INT_MIN)
        thr_ref[0] = tau
        thr_ref[1] = jnp.full((8, tq), seq, I32)

        @pl.when(jnp.max(n_ge) > topk)
        def _():
            need = topk - count(lambda keys, start: keys > tau[None])
            cut = jnp.zeros((8, tq), I32)
            for bit in range(int(math.log2(seq)), -1, -1):
                cand = cut | np.int32(1 << bit)
                c = count(lambda keys, start, cand=cand: (keys == tau[None]) & (start + k_in_chunk < cand[None]))
                cut = jnp.where(c < need, cand, cut)
            thr_ref[1] = cut

    tau = thr_ref[0]
    cut = thr_ref[1]

    m_ref[...] = jnp.full(m_ref.shape, NEG_BIG, F32)
    l_ref[...] = jnp.zeros(l_ref.shape, F32)
    acc_ref[...] = jnp.zeros(acc_ref.shape, F32)
    pq = pos_ref[0, qi]
    pq_min = jnp.min(pq)
    last_bias = [rel_ref[REL_BUCKETS - 1, h] for h in range(N_HEADS)]

    def softmax_head(h, s_t, addend, bias):
        cols = head_cols[h]
        sb = s_t.reshape(groups, 8, tq) + addend
        m_old = m_ref[:, cols]
        m_new = jnp.maximum(m_old, _sublane_all(jnp.max(sb, axis=0), jnp.maximum) + bias)
        p = jnp.exp(sb - (m_new - bias)[None])
        alpha = jnp.exp(m_old - m_new)
        l_ref[:, cols] = alpha * l_ref[:, cols] + _sublane_all(jnp.sum(p, axis=0), jnp.add)
        p_ref[:, cols] = p.reshape(tc, tq).astype(BF16)
        alpha_ref[:, cols] = alpha
        m_ref[:, cols] = m_new

    def accumulate(c, half):
        cols = slice(half * (width // 2), (half + 1) * (width // 2))
        pv = jnp.dot(vvt_ref[0, c], p_ref[:, cols], preferred_element_type=F32)
        scaled = acc_ref[:, cols].reshape(t // 8, 8, width // 2) * alpha_ref[:, cols][None]
        acc_ref[:, cols] = scaled.reshape(t, width // 2) + pv

    def all_heads(c, addend_of, bias_of):
        start = pl.multiple_of(c * tc, tc)
        s_all = lax.dot_general(kk_ref[0, pl.ds(start, tc), :], qs_ref[...], NT_DIMS, preferred_element_type=F32)
        for half in range(2):
            for h in range(half * N_HEADS // 2, (half + 1) * N_HEADS // 2):
                softmax_head(h, s_all[:, head_cols[h]], addend_of(h), bias_of(h))
            accumulate(c, half)

    def attend_chunk(c, carry):
        keys = chunk3(keys_ref, c)
        sel = (keys > tau[None]) | ((keys == tau[None]) & (k_in_chunk <= (cut - c * tc)[None]))
        selb_ref[...] = jnp.where(sel, 0.0, NEG_BIG).reshape(tc, tq)
        first_tile = c * (tc // t)
        pk_max = pos_ref[0, first_tile]
        for j in range(1, tc // t):
            pk_max = jnp.maximum(pk_max, pos_ref[0, first_tile + j])
        chunk_far = pq_min - jnp.max(pk_max) >= FAR_DIST

        @pl.when(chunk_far)
        def _():
            all_heads(c, lambda h: selb_ref[...].reshape(groups, 8, tq), lambda h: last_bias[h])

        @pl.when(jnp.logical_not(chunk_far))
        def _():
            for j in range(tc // t):
                rows = slice(j * t, (j + 1) * t)
                pk = pos_ref[0, first_tile + j]
                tile_far = pq_min - jnp.max(pk) >= FAR_DIST

                @pl.when(tile_far)
                def _():
                    for h in range(N_HEADS):
                        add_ref[rows, head_cols[h]] = selb_ref[rows, :] + last_bias[h]

                @pl.when(jnp.logical_not(tile_far))
                def _():
                    pk_col = jnp.transpose(jnp.broadcast_to(pk, (t, t)))
                    dist = jnp.clip(pq - pk_col, 0, t - 1)
                    for h in range(N_HEADS):
                        table = jnp.broadcast_to(tab_ref[h:h + 1, :], (t, t))
                        bias = jnp.take_along_axis(table, dist, axis=1, mode="promise_in_bounds")
                        add_ref[rows, head_cols[h]] = selb_ref[rows, :] + bias

            all_heads(c, lambda h: add_ref[:, head_cols[h]].reshape(groups, 8, tq), lambda h: 0.0)

        return carry

    lax.fori_loop(0, n_chunks, attend_chunk, 0)

    for p in range(N_HEADS // 2):
        outs = []
        for h in (2 * p, 2 * p + 1):
            cols = head_cols[h]
            o_t = acc_ref[:, cols].reshape(t // 8, 8, tq) / l_ref[:, cols][None]
            outs.append(jnp.transpose(o_t.reshape(t, tq)))
        o_ref[0, :, p * LANES:(p + 1) * LANES] = jnp.where(lane < HEAD_DIM, outs[0], outs[1])


def _dsa_attention(rel_bias, dist_table, pb, pf, dq, dkk, vvt, pos_rows, b, s, tq, tc):
    topk = min(TOPK_MAX, s // 4)
    pb3 = pb.reshape(b, s, N_BF)
    pf3 = pf.reshape(b, s, N_F32)
    width = N_HEADS * tq
    kernel = functools.partial(_dsa_kernel, tq=tq, tc=tc, topk=topk, seq=s)
    return pl.pallas_call(
        kernel,
        grid=(b, s // tq),
        in_specs=[pl.BlockSpec(memory_space=pltpu.SMEM),
                  pl.BlockSpec(dist_table.shape, lambda bi, i: (0, 0)),
                  pl.BlockSpec((1, tq, 2 * LANES), lambda bi, i: (bi, i, B_IXQ // 256)),
                  pl.BlockSpec((1, tq, LANES), lambda bi, i: (bi, i, F_IXW // LANES)),
                  pl.BlockSpec((1, s, LANES), lambda bi, i: (bi, 0, B_IXK // LANES)),
                  pl.BlockSpec((1, tq, N_HEADS * LANES), lambda bi, i: (bi, i, 0)),
                  pl.BlockSpec((1, s, LANES), lambda bi, i: (bi, 0, 0)),
                  pl.BlockSpec((1, s // tc, LANES, tc), lambda bi, i: (bi, 0, 0, 0)),
                  pl.BlockSpec((1, s // LANES, 1, LANES), lambda bi, i: (bi, 0, 0, 0))],
        out_specs=pl.BlockSpec((1, tq, BRANCH_WIDTH), lambda bi, i: (bi, i, 0)),
        out_shape=jax.ShapeDtypeStruct((b, s, BRANCH_WIDTH), F32),
        scratch_shapes=[pltpu.VMEM((s, tq), I32),
                        pltpu.VMEM((width, LANES), BF16),
                        pltpu.VMEM((width, LANES), BF16),
                        pltpu.VMEM((LANES, tq), F32),
                        pltpu.VMEM((tc, width), BF16),
                        pltpu.VMEM((tc, tq), F32),
                        pltpu.VMEM((tc, width), F32),
                        pltpu.VMEM((8, width), F32),
                        pltpu.VMEM((8, width), F32),
                        pltpu.VMEM((8, width), F32),
                        pltpu.VMEM((LANES, width), F32),
                        pltpu.VMEM((2, 8, tq), I32)],
        compiler_params=pltpu.CompilerParams(dimension_semantics=("parallel", "arbitrary"),
                                             vmem_limit_bytes=VMEM_LIMIT),
        name="dsa_attention",
    )(rel_bias, dist_table, pb3, pf3, pb3, dq.reshape(b, s, -1), dkk.reshape(b, s, -1), vvt, pos_rows)


def _merge_kernel(x_ref, ya_ref, yb_ref, yc_ref, za_ref, zb_ref, zc_ref, ga_ref, gb_ref, gc_ref,
                  bias_ref, wbr_ref, wout_ref, o_ref):
    merged = None
    branches = ((ya_ref, za_ref, ga_ref), (yb_ref, zb_ref, gb_ref), (yc_ref, zc_ref, gc_ref))
    for n, (y_ref, z_ref, g_ref) in enumerate(branches):
        z = z_ref[...].astype(F32)
        branch = (y_ref[...] * (z * jax.nn.sigmoid(z))).astype(BF16)
        up = jnp.dot(branch, wbr_ref[n], preferred_element_type=F32)
        gate = jax.nn.sigmoid(g_ref[...].astype(F32) + bias_ref[n:n + 1, :])
        merged = gate * up if merged is None else merged + gate * up
    o_ref[...] = x_ref[...] + jnp.dot(merged.astype(BF16), wout_ref[...], preferred_element_type=F32)


def _merge(x2, ya, yb, yc, pb, gate_b, wbr, wout, tm):
    m, d = x2.shape
    row = lambda width: pl.BlockSpec((tm, width), lambda i: (i, 0))
    col = lambda width, off: pl.BlockSpec((tm, width), lambda i, _b=off // width: (i, _b))
    z_specs = [col(BRANCH_WIDTH, B_Z + n * BRANCH_WIDTH) for n in range(N_BRANCH)]
    g_specs = [col(d, B_G + n * d) for n in range(N_BRANCH)]
    return pl.pallas_call(
        _merge_kernel,
        grid=(m // tm,),
        in_specs=[row(d), row(BRANCH_WIDTH), row(BRANCH_WIDTH), row(BRANCH_WIDTH)] + z_specs + g_specs + [
            pl.BlockSpec(gate_b.shape, lambda i: (0, 0)),
            pl.BlockSpec(wbr.shape, lambda i: (0, 0, 0)),
            pl.BlockSpec(wout.shape, lambda i: (0, 0))],
        out_specs=row(d),
        out_shape=jax.ShapeDtypeStruct((m, d), F32),
        compiler_params=pltpu.CompilerParams(dimension_semantics=("parallel",), vmem_limit_bytes=VMEM_LIMIT),
        name="merge",
    )(x2, ya, yb, yc, pb, pb, pb, pb, pb, pb, gate_b, wbr, wout)


def _lane_table(vals_by_lane):
    t = jnp.zeros((LANES,), F32)
    for start, v in vals_by_lane.items():
        t = t.at[start:start + v.shape[0]].set(v)
    return t[None, :]


def _layer_weights(l, w_bf, w_f32, norm_g, mla_q_norm_g, mla_kv_norm_g, mla_w_uq, mla_w_ukv, mla_q_g, mla_k_g,
                   dsa_q_g, dsa_k_g, gate_b, w_branch, w_out):
    uq, ukv = mla_w_uq[l], mla_w_ukv[l]
    half = MLA_ROPE // 2
    q_pieces, qr_pieces, k_pieces, v_pieces = [], [], [], []
    for h in range(N_HEADS):
        o = h * MLA_QK
        q_pieces += [(o, MLA_QK, 1.0), (None, LANES - MLA_QK, 1.0)]
        qr_pieces += [(None, MLA_NOPE, 1.0), (o + MLA_NOPE + half, half, 1.0), (o + MLA_NOPE, half, 1.0),
                      (None, LANES - MLA_QK, 1.0)]
        k_pieces += [(h * 128, MLA_NOPE, 1.0), (None, LANES - MLA_NOPE, 1.0)]
        v_pieces += [(h * 128 + MLA_NOPE, 64, 1.0)]
    scale = MLA_QK ** -0.5
    qg, kg = mla_q_g[l], mla_k_g[l]
    rot_gain = lambda g: jnp.concatenate([g[MLA_NOPE + half:], g[MLA_NOPE:MLA_NOPE + half]])
    dq_g = jnp.concatenate([jnp.concatenate([dsa_q_g[l], jnp.zeros((HEAD_DIM,), F32)]) if h % 2 == 0 else
                            jnp.concatenate([jnp.zeros((HEAD_DIM,), F32), dsa_q_g[l]]) for h in range(N_HEADS)])
    return dict(
        norm_g=norm_g[l][None, :], w_bf=w_bf[l], w_f32=w_f32[l],
        wuq=_assemble(uq, q_pieces).astype(BF16), wuqr=_assemble(uq, qr_pieces).astype(BF16),
        wuk=_assemble(ukv, k_pieces).astype(BF16), wuv=_assemble(ukv, v_pieces).astype(BF16),
        qng=mla_q_norm_g[l][None, :], kvng=mla_kv_norm_g[l][None, :],
        gq=_lane_table({0: qg * scale}), gqr=_lane_table({MLA_NOPE: rot_gain(qg) * scale}),
        gk=_lane_table({0: kg}), gkr=_lane_table({MLA_NOPE: rot_gain(kg)}),
        gdq=(dq_g * HEAD_DIM ** -0.5)[None, :], gdk=jnp.concatenate([dsa_k_g[l], dsa_k_g[l]])[None, :],
        gate_b=gate_b[l], wbr=w_branch[l].astype(BF16), wout=w_out[l].astype(BF16))


def kernel(x, positions, norm_g, w_in, mla_q_norm_g, mla_kv_norm_g, mla_w_uq, mla_w_ukv, mla_q_g, mla_k_g,
           dsa_q_g, dsa_k_g, rel_bias, gate_b, w_branch, w_out):
    b, s, d = x.shape
    depth = w_in.shape[0]
    m = b * s
    assert d == D_MODEL and w_in.shape[-1] == D_IN and s % 512 == 0 and s & (s - 1) == 0

    half = MLA_ROPE // 2
    inv_freq = ROPE_THETA ** (-(jnp.arange(half, dtype=F32) * 2.0) / MLA_ROPE)
    ang = positions.astype(F32)[..., None] * inv_freq
    cos, sin = jnp.cos(ang).reshape(m, half), jnp.sin(ang).reshape(m, half)
    cos_t = jnp.concatenate([jnp.ones((m, MLA_NOPE), F32), cos, cos, jnp.ones((m, LANES - MLA_QK), F32)], axis=1)
    sin_t = jnp.concatenate([jnp.zeros((m, MLA_NOPE), F32), -sin, sin, jnp.zeros((m, LANES - MLA_QK), F32)], axis=1)

    bf_pieces, f32_pieces = _in_proj_layouts()
    w_in16 = w_in.astype(BF16)
    w_bf = _assemble(w_in16, bf_pieces)
    w_f32 = _assemble(w_in16, f32_pieces)

    tm = min(1024, m)
    tp = min(512, m)
    dsa_tc = 512
    mla_tc = 512
    pos_rows = positions.reshape(b, s // LANES, 1, LANES)
    dist_table = rel_bias[BUCKET_OF_DIST, :].T
    x2 = x.reshape(m, d)
    for l in range(depth):
        lw = _layer_weights(l, w_bf, w_f32, norm_g, mla_q_norm_g, mla_kv_norm_g, mla_w_uq, mla_w_ukv,
                            mla_q_g, mla_k_g, dsa_q_g, dsa_k_g, gate_b, w_branch, w_out)
        pb = _in_proj(x2, lw["norm_g"], lw["w_bf"], BF16, tm, 1024)
        pf = _in_proj(x2, lw["norm_g"], lw["w_f32"], F32, tm, 1024)
        qb, kb, vb, dq, dkk = _prologue(pf, cos_t, sin_t, lw, tp)
        ya = _sb_attention(pb, b, s, 128)
        vbt = vb.reshape(b, s // mla_tc, mla_tc, N_HEADS // 2, LANES).transpose(0, 1, 3, 4, 2)
        yb = _mla_attention(qb, kb, vbt, b, s, LANES, mla_tc)
        vvt = pb.reshape(b, s // dsa_tc, dsa_tc, N_BF)[..., B_DSV:B_DSV + LANES].transpose(0, 1, 3, 2)
        yc = _dsa_attention(rel_bias, dist_table, pb, pf, dq, dkk, vvt, pos_rows, b, s, LANES, dsa_tc)
        x2 = _merge(x2, ya.reshape(m, -1), yb.reshape(m, -1), yc.reshape(m, -1), pb, lw["gate_b"],
                    lw["wbr"], lw["wout"], tp)
    return x2.reshape(b, s, d)
```

```python
import functools
import math

import numpy as np
import jax
import jax.numpy as jnp
from jax import lax
from jax.experimental import pallas as pl
from jax.experimental.pallas import tpu as pltpu

F32 = jnp.float32
BF16 = jnp.bfloat16
I32 = jnp.int32

LANES = 128
EPS = 1e-6
D_MODEL = 1024
N_HEADS = 8
HEAD_DIM = 64
MLA_Q_LORA = 256
MLA_KV_LORA = 128
MLA_NOPE = 64
MLA_ROPE = 32
MLA_QK = MLA_NOPE + MLA_ROPE
ROPE_THETA = 10000.0
IDX_DIM = 32
TOPK_MAX = 256
REL_BUCKETS = 32
REL_MAX_DIST = 128
N_BRANCH = 3
BRANCH_WIDTH = 512

O_SAQ, O_SAK, O_SAV = 0, 512, 1024
O_CQ, O_CKV, O_KPE = 1536, 1792, 1920
O_DSQ, O_DSK, O_DSV = 1952, 2464, 2528
O_IXQ, O_IXK, O_IXW = 2592, 2848, 2880
O_Z, O_G = 2888, 4424
D_IN = 7496

NEG_BIG = -1e30
INT_MIN = -(2 ** 31)
SB_UNDERFLOW = 104.0
VMEM_LIMIT = 48 * 1024 * 1024

NT_DIMS = (((1,), (1,)), ((), ()))
V_ROWS = HEAD_DIM + 8


def _t5_buckets():
    exact = REL_BUCKETS // 2
    n = np.arange(0, 4 * REL_MAX_DIST)
    nf = np.maximum(n, 1).astype(np.float64)
    large = exact + (np.log(nf / exact) / math.log(REL_MAX_DIST / exact) * (REL_BUCKETS - exact)).astype(np.int64)
    bucket = np.where(n < exact, n, np.minimum(large, REL_BUCKETS - 1))
    assert np.all(np.diff(bucket) >= 0) and np.all(bucket[LANES - 1:] == REL_BUCKETS - 1)
    return bucket[:LANES]


BUCKET_OF_DIST = _t5_buckets()
FAR_DIST = int(np.argmax(BUCKET_OF_DIST == REL_BUCKETS - 1))


def _assemble(w, pieces):
    cols = []
    for start, width, scale in pieces:
        if start is None:
            cols.append(jnp.zeros(w.shape[:-1] + (width,), w.dtype))
        else:
            c = w[..., start:start + width]
            cols.append(c if scale == 1.0 else c * scale)
    return jnp.concatenate(cols, axis=-1)


def _head_padded(start, scale=1.0):
    pieces = []
    for h in range(N_HEADS):
        col = (start + h * HEAD_DIM, HEAD_DIM, scale)
        pad = (None, HEAD_DIM, 1.0)
        pieces += [col, pad] if h % 2 == 0 else [pad, col]
    return pieces


def _in_proj_layouts():
    bf = (_head_padded(O_SAQ, HEAD_DIM ** -0.5)
          + [(O_SAK, 512, 1.0), (O_SAV, 512, 1.0), (O_IXQ, 256, 1.0)]
          + [(O_DSV, 64, 1.0)] * 2
          + [(O_IXK, 32, 1.0)] * 4
          + [(O_Z, 1536, 1.0), (O_G, 3072, 1.0)])
    f32 = (_head_padded(O_DSQ)
           + [(O_CQ, 256, 1.0), (O_CKV, 128, 1.0)]
           + [(None, 64, 1.0), (O_KPE, 32, 1.0), (None, 32, 1.0)]
           + [(None, 64, 1.0), (O_KPE + 16, 16, 1.0), (O_KPE, 16, 1.0), (None, 32, 1.0)]
           + [(O_DSK, 64, 1.0)] * 2
           + [(O_IXW, 8, 1.0), (None, 120, 1.0)]
           + [(None, 128, 1.0)])
    return bf, f32


B_SAQ, B_SAK, B_SAV, B_IXQ, B_DSV, B_IXK, B_Z, B_G, N_BF = 0, 1024, 1536, 2048, 2304, 2432, 2560, 4096, 7168
F_DSQ, F_CQ, F_CKV, F_KPE, F_KPER, F_DSK, F_IXW, N_F32 = 0, 1024, 1280, 1408, 1536, 1664, 1792, 2048


def _in_proj_kernel(x_ref, g_ref, w_ref, o_ref, h_ref):
    @pl.when(pl.program_id(1) == 0)
    def _():
        x = x_ref[...]
        ms = jnp.mean(x * x, axis=-1, keepdims=True)
        h_ref[...] = (x * lax.rsqrt(ms + EPS) * g_ref[...]).astype(BF16)

    o_ref[...] = jnp.dot(h_ref[...], w_ref[...], preferred_element_type=F32).astype(o_ref.dtype)


def _in_proj(x2, g, w, out_dtype, tm, tn):
    m, d = x2.shape
    n = w.shape[1]
    return pl.pallas_call(
        _in_proj_kernel,
        grid=(m // tm, n // tn),
        in_specs=[pl.BlockSpec((tm, d), lambda i, j: (i, 0)),
                  pl.BlockSpec((1, d), lambda i, j: (0, 0)),
                  pl.BlockSpec((d, tn), lambda i, j: (0, j))],
        out_specs=pl.BlockSpec((tm, tn), lambda i, j: (i, j)),
        out_shape=jax.ShapeDtypeStruct((m, n), out_dtype),
        scratch_shapes=[pltpu.VMEM((tm, d), BF16)],
        compiler_params=pltpu.CompilerParams(dimension_semantics=("parallel", "arbitrary"),
                                             vmem_limit_bytes=VMEM_LIMIT),
        name="in_proj",
    )(x2, g, w)


def _rms(x, width):
    return lax.rsqrt(jnp.sum(x * x, axis=-1, keepdims=True) * (1.0 / width) + EPS)


def _prologue_kernel(dsq_ref, cq_ref, ckv_ref, kpe_ref, kper_ref, dsk_ref, cos_ref, sin_ref,
                     wuq_ref, wuqr_ref, wuk_ref, wuv_ref,
                     qng_ref, kvng_ref, gq_ref, gqr_ref, gk_ref, gkr_ref, gdq_ref, gdk_ref,
                     qb_ref, kb_ref, vb_ref, dq_ref, dkk_ref):
    cos = cos_ref[...]
    sin = sin_ref[...]

    cq = cq_ref[...]
    cqn = (cq * _rms(cq, MLA_Q_LORA) * qng_ref[...]).astype(BF16)
    q = jnp.dot(cqn, wuq_ref[...], preferred_element_type=F32)
    qr = jnp.dot(cqn, wuqr_ref[...], preferred_element_type=F32)
    cg = cos * gq_ref[...]
    sg = sin * gqr_ref[...]
    for h in range(N_HEADS):
        sl = slice(h * LANES, (h + 1) * LANES)
        qh = q[:, sl]
        qb_ref[:, sl] = ((qh * cg + qr[:, sl] * sg) * _rms(qh, MLA_QK)).astype(BF16)

    ckv = ckv_ref[...]
    ckvn = (ckv * _rms(ckv, MLA_KV_LORA) * kvng_ref[...]).astype(BF16)
    kn = jnp.dot(ckvn, wuk_ref[...], preferred_element_type=F32)
    vb_ref[...] = jnp.dot(ckvn, wuv_ref[...], preferred_element_type=F32).astype(BF16)
    kpe = kpe_ref[...]
    cgk = cos * gk_ref[...]
    rot = kper_ref[...] * (sin * gkr_ref[...])
    for h in range(N_HEADS):
        sl = slice(h * LANES, (h + 1) * LANES)
        kf = kn[:, sl] + kpe
        kb_ref[:, sl] = ((kf * cgk + rot) * _rms(kf, MLA_QK)).astype(BF16)

    for h in range(N_HEADS):
        sl = slice(h * LANES, (h + 1) * LANES)
        xq = dsq_ref[:, sl]
        dq_ref[:, sl] = (xq * _rms(xq, HEAD_DIM) * gdq_ref[:, sl]).astype(BF16)
    xk = dsk_ref[...]
    dkk_ref[...] = (xk * _rms(xk, 2 * HEAD_DIM) * gdk_ref[...]).astype(BF16)


def _prologue(pf, cos_t, sin_t, lw, tm):
    m = pf.shape[0]

    def act(width, off):
        return pl.BlockSpec((tm, width), lambda i, _b=off // width: (i, _b))

    def whole(a):
        return pl.BlockSpec(a.shape, lambda i: (0,) * a.ndim)

    weights = [lw[k] for k in ("wuq", "wuqr", "wuk", "wuv", "qng", "kvng", "gq", "gqr", "gk", "gkr", "gdq", "gdk")]
    row = lambda width: pl.BlockSpec((tm, width), lambda i: (i, 0))
    return pl.pallas_call(
        _prologue_kernel,
        grid=(m // tm,),
        in_specs=[act(1024, F_DSQ), act(256, F_CQ), act(128, F_CKV), act(128, F_KPE), act(128, F_KPER),
                  act(128, F_DSK), row(LANES), row(LANES)] + [whole(w) for w in weights],
        out_specs=[row(1024), row(1024), row(512), row(1024), row(LANES)],
        out_shape=[jax.ShapeDtypeStruct((m, 1024), BF16), jax.ShapeDtypeStruct((m, 1024), BF16),
                   jax.ShapeDtypeStruct((m, 512), BF16), jax.ShapeDtypeStruct((m, 1024), BF16),
                   jax.ShapeDtypeStruct((m, LANES), BF16)],
        compiler_params=pltpu.CompilerParams(dimension_semantics=("parallel",), vmem_limit_bytes=VMEM_LIMIT),
        name="prologue",
    )(pf, pf, pf, pf, pf, pf, cos_t, sin_t, *weights)


def _sublane_all(x, op):
    for shift in (4, 2, 1):
        x = op(x, pltpu.roll(x, shift, 0))
    return x


def _sb_kernel(q_ref, k_ref, v_ref, o_ref, acc_ref, carry_ref, hl_ref, lb_ref, *, tq):
    t = LANES
    qi = pl.program_id(1)
    jj = lax.broadcasted_iota(I32, (t, 2 * t), 0)
    ss = lax.broadcasted_iota(I32, (t, 2 * t), 1)
    tri_ones = jnp.logical_or(ss >= t, jj > ss).astype(BF16)
    q_idx = qi * tq + lax.broadcasted_iota(I32, (tq, t), 0)
    k_lane = lax.broadcasted_iota(I32, (tq, t), 1)

    acc_ref[...] = jnp.zeros(acc_ref.shape, F32)
    carry_ref[...] = jnp.zeros(carry_ref.shape, F32)

    def cond(c):
        kb, cmax = c
        return jnp.logical_and(kb >= 0, cmax > -SB_UNDERFLOW)

    def body(c):
        kb, _ = c
        start = pl.multiple_of(kb * t, t)
        mask = (start + k_lane) < q_idx
        for h in range(N_HEADS):
            q = q_ref[0, :, h * LANES:(h + 1) * LANES]
            k = k_ref[0, pl.ds(start, t), (h // 2) * LANES:(h // 2 + 1) * LANES]
            z = lax.dot_general(q, k, NT_DIMS, preferred_element_type=F32)
            sp = jnp.maximum(z, 0.0) + jnp.log(1.0 + jnp.exp(-jnp.abs(z)))
            l1m = jnp.where(mask, -sp, 0.0)
            hi = l1m.astype(BF16)
            hl_ref[2 * h * tq:(2 * h + 1) * tq, :] = hi
            hl_ref[(2 * h + 1) * tq:(2 * h + 2) * tq, :] = (l1m - hi.astype(F32)).astype(BF16)
            lb_ref[h * tq:(h + 1) * tq, :] = z - sp
        sums = jnp.dot(hl_ref[...], tri_ones, preferred_element_type=F32)
        for h in range(N_HEADS):
            rows = slice(h * tq, (h + 1) * tq)
            both = sums[2 * h * tq:(2 * h + 1) * tq, :] + sums[(2 * h + 1) * tq:(2 * h + 2) * tq, :]
            carry = carry_ref[rows, :]
            a = jnp.where(mask, jnp.exp(lb_ref[rows, :] + carry + both[:, :t]), 0.0)
            v = v_ref[0, pl.ds(start, t), (h // 2) * LANES:(h // 2 + 1) * LANES]
            acc_ref[rows, :] += jnp.dot(a.astype(BF16), v, preferred_element_type=F32)
            carry_ref[rows, :] = carry + both[:, t:]
        return kb - 1, jnp.max(carry_ref[...])

    lax.while_loop(cond, body, (((qi + 1) * tq) // t - 1, jnp.float32(0.0)))

    lane = lax.broadcasted_iota(I32, (tq, LANES), 1)
    for p in range(N_HEADS // 2):
        even = acc_ref[2 * p * tq:(2 * p + 1) * tq, :]
        odd = acc_ref[(2 * p + 1) * tq:(2 * p + 2) * tq, :]
        o_ref[0, :, p * LANES:(p + 1) * LANES] = jnp.where(lane < HEAD_DIM, even, odd)


def _sb_attention(pb, b, s, tq):
    pb3 = pb.reshape(b, s, N_BF)
    kernel = functools.partial(_sb_kernel, tq=tq)
    return pl.pallas_call(
        kernel,
        grid=(b, s // tq),
        in_specs=[pl.BlockSpec((1, tq, N_HEADS * LANES), lambda bi, i: (bi, i, B_SAQ // (N_HEADS * LANES))),
                  pl.BlockSpec((1, s, BRANCH_WIDTH), lambda bi, i: (bi, 0, B_SAK // BRANCH_WIDTH)),
                  pl.BlockSpec((1, s, BRANCH_WIDTH), lambda bi, i: (bi, 0, B_SAV // BRANCH_WIDTH))],
        out_specs=pl.BlockSpec((1, tq, BRANCH_WIDTH), lambda bi, i: (bi, i, 0)),
        out_shape=jax.ShapeDtypeStruct((b, s, BRANCH_WIDTH), F32),
        scratch_shapes=[pltpu.VMEM((N_HEADS * tq, LANES), F32),
                        pltpu.VMEM((N_HEADS * tq, LANES), F32),
                        pltpu.VMEM((2 * N_HEADS * tq, LANES), BF16),
                        pltpu.VMEM((N_HEADS * tq, LANES), F32)],
        compiler_params=pltpu.CompilerParams(dimension_semantics=("parallel", "arbitrary"),
                                             vmem_limit_bytes=VMEM_LIMIT),
        name="sb_attention",
    )(pb3, pb3, pb3)


def _mla_kernel(q_ref, k_ref, vt_ref, o_ref, qbd_ref, p_ref, m_ref, alpha_ref, acc_ref, *, tq, tc):
    t = LANES
    assert tq == t
    groups = tc // 8
    pair_w = 2 * tq
    qi = pl.program_id(1)
    n_chunks = (qi * tq) // tc + 1
    k_in_chunk = (lax.broadcasted_iota(I32, (groups, 8, tq), 0) * 8
                  + lax.broadcasted_iota(I32, (groups, 8, tq), 1))
    q_lane = qi * tq + lax.broadcasted_iota(I32, (groups, 8, tq), 2)

    zeros = jnp.zeros((tq, t), BF16)
    for p in range(N_HEADS // 2):
        q_even = q_ref[0, :, 2 * p * LANES:(2 * p + 1) * LANES]
        q_odd = q_ref[0, :, (2 * p + 1) * LANES:(2 * p + 2) * LANES]
        qbd_ref[p, 0:tq, :] = jnp.concatenate([q_even, zeros], axis=1)
        qbd_ref[p, tq:2 * tq, :] = jnp.concatenate([zeros, q_odd], axis=1)

    v_rows = acc_ref.shape[0]
    m_ref[...] = jnp.full(m_ref.shape, NEG_BIG, F32)
    acc_ref[...] = jnp.zeros(acc_ref.shape, F32)

    def chunk(c, diagonal):
        start = pl.multiple_of(c * tc, tc)
        pair_cols = [slice(p * pair_w, (p + 1) * pair_w) for p in range(N_HEADS // 2)]
        s_pairs = [lax.dot_general(k_ref[0, pl.ds(start, tc), pair_cols[p]], qbd_ref[p], NT_DIMS,
                                   preferred_element_type=F32) for p in range(N_HEADS // 2)]
        for p in range(N_HEADS // 2):
            pcols = pair_cols[p]
            for i in range(2):
                cols = slice(p * pair_w + i * tq, p * pair_w + (i + 1) * tq)
                sb = s_pairs[p][:, i * tq:(i + 1) * tq].reshape(groups, 8, tq)
                if diagonal:
                    sb = jnp.where(start + k_in_chunk <= q_lane, sb, NEG_BIG)
                m_old = m_ref[:, cols]
                m_new = jnp.maximum(m_old, _sublane_all(jnp.max(sb, axis=0), jnp.maximum))
                prob = jnp.exp(sb - m_new[None])
                alpha = jnp.exp(m_old - m_new)
                p_ref[:, cols] = prob.reshape(tc, tq).astype(BF16)
                alpha_ref[:, cols] = alpha
                m_ref[:, cols] = m_new
            pv = jnp.dot(vt_ref[0, c, p], p_ref[:, pcols], preferred_element_type=F32)
            scaled = acc_ref[:, pcols].reshape(v_rows // 8, 8, pair_w) * alpha_ref[:, pcols][None]
            acc_ref[:, pcols] = scaled.reshape(v_rows, pair_w) + pv

    def body(c, carry):
        chunk(c, False)
        return carry

    lax.fori_loop(0, n_chunks - 1, body, 0)
    chunk(n_chunks - 1, True)

    for p in range(N_HEADS // 2):
        outs = []
        for i in range(2):
            cols = slice(p * pair_w + i * tq, p * pair_w + (i + 1) * tq)
            outs.append(acc_ref[i * HEAD_DIM:(i + 1) * HEAD_DIM, cols] / acc_ref[t:t + 1, cols])
        o_ref[0, :, p * LANES:(p + 1) * LANES] = jnp.transpose(jnp.concatenate(outs, axis=0))


def _mla_attention(qb, kb, vbt, b, s, tq, tc):
    width = N_HEADS * tq
    kernel = functools.partial(_mla_kernel, tq=tq, tc=tc)
    return pl.pallas_call(
        kernel,
        grid=(b, s // tq),
        in_specs=[pl.BlockSpec((1, tq, N_HEADS * LANES), lambda bi, i: (bi, i, 0)),
                  pl.BlockSpec((1, s, N_HEADS * LANES), lambda bi, i: (bi, 0, 0)),
                  pl.BlockSpec((1, s // tc, N_HEADS // 2, LANES + 8, tc), lambda bi, i: (bi, 0, 0, 0, 0))],
        out_specs=pl.BlockSpec((1, tq, BRANCH_WIDTH), lambda bi, i: (bi, i, 0)),
        out_shape=jax.ShapeDtypeStruct((b, s, BRANCH_WIDTH), F32),
        scratch_shapes=[pltpu.VMEM((N_HEADS // 2, 2 * tq, 2 * LANES), BF16),
                        pltpu.VMEM((tc, width), BF16),
                        pltpu.VMEM((8, width), F32),
                        pltpu.VMEM((8, width), F32),
                        pltpu.VMEM((LANES + 8, width), F32)],
        compiler_params=pltpu.CompilerParams(dimension_semantics=("parallel", "arbitrary"),
                                             vmem_limit_bytes=VMEM_LIMIT),
        name="mla_attention",
    )(qb.reshape(b, s, -1), kb.reshape(b, s, -1), vbt)


def _sortable(x):
    bits = pltpu.bitcast(x + 0.0, I32)
    return bits ^ ((bits >> 31) & 0x7FFFFFFF)


def _dsa_kernel(rel_ref, tab_ref, ixq_ref, ixw_ref, ixk_ref, q_ref, kk_ref, vvt_ref, pos_ref, o_ref,
                keys_ref, qm_ref, qs_ref, wt_ref, p_ref, selb_ref, add_ref,
                m_ref, alpha_ref, acc_ref, thr_ref, *, tq, tc, topk, seq):
    t = LANES
    assert tq == t
    groups = tc // 8
    width = N_HEADS * tq
    qi = pl.program_id(1)
    n_chunks = (qi * tq) // tc + 1
    head_cols = [slice(h * tq, (h + 1) * tq) for h in range(N_HEADS)]
    lane = lax.broadcasted_iota(I32, (tq, t), 1)
    q_lane = qi * tq + lax.broadcasted_iota(I32, (groups, 8, tq), 2)
    k_in_chunk = (lax.broadcasted_iota(I32, (groups, 8, tq), 0) * 8
                  + lax.broadcasted_iota(I32, (groups, 8, tq), 1))

    for h in range(N_HEADS):
        grp = ixq_ref[0, :, (h // 4) * LANES:(h // 4 + 1) * LANES]
        lo = (h % 4) * IDX_DIM
        qm_ref[h * tq:(h + 1) * tq, :] = jnp.where((lane >= lo) & (lane < lo + IDX_DIM), grp, jnp.zeros_like(grp))
        qs_ref[h * tq:(h + 1) * tq, :] = q_ref[0, :, h * LANES:(h + 1) * LANES]
    wt_ref[...] = jnp.transpose(ixw_ref[0] * ((IDX_DIM ** -0.5) * (N_HEADS ** -0.5)))

    def chunk3(ref, c):
        start = pl.multiple_of(c * tc, tc)
        return ref[pl.ds(start, tc), :].reshape(groups, 8, tq)

    def score_chunk(c, carry):
        start = pl.multiple_of(c * tc, tc)
        st = lax.dot_general(ixk_ref[0, pl.ds(start, tc), :], qm_ref[...], NT_DIMS, preferred_element_type=F32)
        sc = None
        for h in range(N_HEADS):
            term = jnp.maximum(st[:, head_cols[h]], 0.0) * wt_ref[h:h + 1, :]
            sc = term if sc is None else sc + term
        key = jnp.where(start + k_in_chunk <= q_lane, _sortable(sc).reshape(groups, 8, tq), INT_MIN)
        keys_ref[pl.ds(start, tc), :] = key.reshape(tc, tq)
        return carry

    lax.fori_loop(0, n_chunks, score_chunk, 0)

    def count(pred):
        def body(c, cnt):
            return cnt + jnp.sum(pred(chunk3(keys_ref, c), c * tc).astype(I32), axis=0)

        return _sublane_all(lax.fori_loop(0, n_chunks, body, jnp.zeros((8, tq), I32)), jnp.add)

    @pl.when((qi + 1) * tq <= topk)
    def _():
        thr_ref[0] = jnp.full((8, tq), INT_MIN, I32)
        thr_ref[1] = jnp.full((8, tq), -1, I32)

    @pl.when((qi + 1) * tq > topk)
    def _():
        ans = jnp.zeros((8, tq), I32)
        n_ge = jnp.full((8, tq), seq, I32)
        for bit in range(31, -1, -1):
            cand = ans | np.int32(INT_MIN if bit == 31 else 1 << bit)
            cand_s = cand ^ np.int32(INT_MIN)
            c = count(lambda keys, start, cand_s=cand_s: keys >= cand_s[None])
            take = c >= topk
            ans = jnp.where(take, cand, ans)
            n_ge = jnp.where(take, c, n_ge)
        tau = ans ^ np.int32(INT_MIN)
        thr_ref[0] = tau
        thr_ref[1] = jnp.full((8, tq), seq, I32)

        @pl.when(jnp.max(n_ge) > topk)
        def _():
            need = topk - count(lambda keys, start: keys > tau[None])
            cut = jnp.zeros((8, tq), I32)
            for bit in range(int(math.log2(seq)), -1, -1):
                cand = cut | np.int32(1 << bit)
                c = count(lambda keys, start, cand=cand: (keys == tau[None]) & (start + k_in_chunk < cand[None]))
                cut = jnp.where(c < need, cand, cut)
            thr_ref[1] = cut

    tau = thr_ref[0]
    cut = thr_ref[1]

    v_rows = acc_ref.shape[0]
    m_ref[...] = jnp.full(m_ref.shape, NEG_BIG, F32)
    acc_ref[...] = jnp.zeros(acc_ref.shape, F32)
    pq = pos_ref[0, qi]
    pq_min = jnp.min(pq)
    last_bias = [rel_ref[REL_BUCKETS - 1, h] for h in range(N_HEADS)]

    def softmax_head(h, s_t, addend, bias):
        cols = head_cols[h]
        sb = s_t.reshape(groups, 8, tq) + addend
        m_old = m_ref[:, cols]
        m_new = jnp.maximum(m_old, _sublane_all(jnp.max(sb, axis=0), jnp.maximum) + bias)
        p = jnp.exp(sb - (m_new - bias)[None])
        alpha = jnp.exp(m_old - m_new)
        p_ref[:, cols] = p.reshape(tc, tq).astype(BF16)
        alpha_ref[:, cols] = alpha
        m_ref[:, cols] = m_new

    def accumulate(c, half):
        cols = slice(half * (width // 2), (half + 1) * (width // 2))
        pv = jnp.dot(vvt_ref[0, c], p_ref[:, cols], preferred_element_type=F32)
        scaled = acc_ref[:, cols].reshape(v_rows // 8, 8, width // 2) * alpha_ref[:, cols][None]
        acc_ref[:, cols] = scaled.reshape(v_rows, width // 2) + pv

    def all_heads(c, addend_of, bias_of):
        start = pl.multiple_of(c * tc, tc)
        s_all = lax.dot_general(kk_ref[0, pl.ds(start, tc), :], qs_ref[...], NT_DIMS, preferred_element_type=F32)
        for half in range(2):
            for h in range(half * N_HEADS // 2, (half + 1) * N_HEADS // 2):
                softmax_head(h, s_all[:, head_cols[h]], addend_of(h), bias_of(h))
            accumulate(c, half)

    def attend_chunk(c, carry):
        keys = chunk3(keys_ref, c)
        sel = (keys > tau[None]) | ((keys == tau[None]) & (k_in_chunk <= (cut - c * tc)[None]))
        selb_ref[...] = jnp.where(sel, 0.0, NEG_BIG).reshape(tc, tq)
        first_tile = c * (tc // t)
        pk_max = pos_ref[0, first_tile]
        for j in range(1, tc // t):
            pk_max = jnp.maximum(pk_max, pos_ref[0, first_tile + j])
        chunk_far = pq_min - jnp.max(pk_max) >= FAR_DIST

        @pl.when(chunk_far)
        def _():
            all_heads(c, lambda h: selb_ref[...].reshape(groups, 8, tq), lambda h: last_bias[h])

        @pl.when(jnp.logical_not(chunk_far))
        def _():
            for j in range(tc // t):
                rows = slice(j * t, (j + 1) * t)
                pk = pos_ref[0, first_tile + j]
                tile_far = jnp.logical_or(pq_min - jnp.max(pk) >= FAR_DIST, c * tc + j * t >= (qi + 1) * tq)

                @pl.when(tile_far)
                def _():
                    for h in range(N_HEADS):
                        add_ref[rows, head_cols[h]] = selb_ref[rows, :] + last_bias[h]

                @pl.when(jnp.logical_not(tile_far))
                def _():
                    pk_col = jnp.transpose(jnp.broadcast_to(pk, (t, t)))
                    dist = jnp.clip(pq - pk_col, 0, t - 1)
                    for h in range(N_HEADS):
                        table = jnp.broadcast_to(tab_ref[h:h + 1, :], (t, t))
                        bias = jnp.take_along_axis(table, dist, axis=1, mode="promise_in_bounds")
                        add_ref[rows, head_cols[h]] = selb_ref[rows, :] + bias

            all_heads(c, lambda h: add_ref[:, head_cols[h]].reshape(groups, 8, tq), lambda h: 0.0)

        return carry

    lax.fori_loop(0, n_chunks, attend_chunk, 0)

    for p in range(N_HEADS // 2):
        outs = []
        for h in (2 * p, 2 * p + 1):
            cols = head_cols[h]
            outs.append(acc_ref[0:HEAD_DIM, cols] / acc_ref[HEAD_DIM:HEAD_DIM + 1, cols])
        o_ref[0, :, p * LANES:(p + 1) * LANES] = jnp.transpose(jnp.concatenate(outs, axis=0))


def _dsa_attention(rel_bias, dist_table, pb, pf, dq, dkk, vvt, pos_rows, b, s, tq, tc):
    topk = min(TOPK_MAX, s // 4)
    pb3 = pb.reshape(b, s, N_BF)
    pf3 = pf.reshape(b, s, N_F32)
    width = N_HEADS * tq
    kernel = functools.partial(_dsa_kernel, tq=tq, tc=tc, topk=topk, seq=s)
    return pl.pallas_call(
        kernel,
        grid=(b, s // tq),
        in_specs=[pl.BlockSpec(memory_space=pltpu.SMEM),
                  pl.BlockSpec(dist_table.shape, lambda bi, i: (0, 0)),
                  pl.BlockSpec((1, tq, 2 * LANES), lambda bi, i: (bi, i, B_IXQ // 256)),
                  pl.BlockSpec((1, tq, LANES), lambda bi, i: (bi, i, F_IXW // LANES)),
                  pl.BlockSpec((1, s, LANES), lambda bi, i: (bi, 0, B_IXK // LANES)),
                  pl.BlockSpec((1, tq, N_HEADS * LANES), lambda bi, i: (bi, i, 0)),
                  pl.BlockSpec((1, s, LANES), lambda bi, i: (bi, 0, 0)),
                  pl.BlockSpec((1, s // tc, V_ROWS, tc), lambda bi, i: (bi, 0, 0, 0)),
                  pl.BlockSpec((1, s // LANES, 1, LANES), lambda bi, i: (bi, 0, 0, 0))],
        out_specs=pl.BlockSpec((1, tq, BRANCH_WIDTH), lambda bi, i: (bi, i, 0)),
        out_shape=jax.ShapeDtypeStruct((b, s, BRANCH_WIDTH), F32),
        scratch_shapes=[pltpu.VMEM((s, tq), I32),
                        pltpu.VMEM((width, LANES), BF16),
                        pltpu.VMEM((width, LANES), BF16),
                        pltpu.VMEM((LANES, tq), F32),
                        pltpu.VMEM((tc, width), BF16),
                        pltpu.VMEM((tc, tq), F32),
                        pltpu.VMEM((tc, width), F32),
                        pltpu.VMEM((8, width), F32),
                        pltpu.VMEM((8, width), F32),
                        pltpu.VMEM((V_ROWS, width), F32),
                        pltpu.VMEM((2, 8, tq), I32)],
        compiler_params=pltpu.CompilerParams(dimension_semantics=("parallel", "arbitrary"),
                                             vmem_limit_bytes=VMEM_LIMIT),
        name="dsa_attention",
    )(rel_bias, dist_table, pb3, pf3, pb3, dq.reshape(b, s, -1), dkk.reshape(b, s, -1), vvt, pos_rows)


def _merge_kernel(x_ref, ya_ref, yb_ref, yc_ref, za_ref, zb_ref, zc_ref, ga_ref, gb_ref, gc_ref,
                  bias_ref, wbr_ref, wout_ref, o_ref):
    merged = None
    branches = ((ya_ref, za_ref, ga_ref), (yb_ref, zb_ref, gb_ref), (yc_ref, zc_ref, gc_ref))
    for n, (y_ref, z_ref, g_ref) in enumerate(branches):
        z = z_ref[...].astype(F32)
        branch = (y_ref[...] * (z * jax.nn.sigmoid(z))).astype(BF16)
        up = jnp.dot(branch, wbr_ref[n], preferred_element_type=F32)
        gate = jax.nn.sigmoid(g_ref[...].astype(F32) + bias_ref[n:n + 1, :])
        merged = gate * up if merged is None else merged + gate * up
    o_ref[...] = x_ref[...] + jnp.dot(merged.astype(BF16), wout_ref[...], preferred_element_type=F32)


def _merge(x2, ya, yb, yc, pb, gate_b, wbr, wout, tm):
    m, d = x2.shape
    row = lambda width: pl.BlockSpec((tm, width), lambda i: (i, 0))
    col = lambda width, off: pl.BlockSpec((tm, width), lambda i, _b=off // width: (i, _b))
    z_specs = [col(BRANCH_WIDTH, B_Z + n * BRANCH_WIDTH) for n in range(N_BRANCH)]
    g_specs = [col(d, B_G + n * d) for n in range(N_BRANCH)]
    return pl.pallas_call(
        _merge_kernel,
        grid=(m // tm,),
        in_specs=[row(d), row(BRANCH_WIDTH), row(BRANCH_WIDTH), row(BRANCH_WIDTH)] + z_specs + g_specs + [
            pl.BlockSpec(gate_b.shape, lambda i: (0, 0)),
            pl.BlockSpec(wbr.shape, lambda i: (0, 0, 0)),
            pl.BlockSpec(wout.shape, lambda i: (0, 0))],
        out_specs=row(d),
        out_shape=jax.ShapeDtypeStruct((m, d), F32),
        compiler_params=pltpu.CompilerParams(dimension_semantics=("parallel",), vmem_limit_bytes=VMEM_LIMIT),
        name="merge",
    )(x2, ya, yb, yc, pb, pb, pb, pb, pb, pb, gate_b, wbr, wout)


def _lane_table(vals_by_lane):
    t = jnp.zeros((LANES,), F32)
    for start, v in vals_by_lane.items():
        t = t.at[start:start + v.shape[0]].set(v)
    return t[None, :]


def _layer_weights(l, w_bf, w_f32, norm_g, mla_q_norm_g, mla_kv_norm_g, mla_w_uq, mla_w_ukv, mla_q_g, mla_k_g,
                   dsa_q_g, dsa_k_g, gate_b, w_branch, w_out):
    uq, ukv = mla_w_uq[l], mla_w_ukv[l]
    half = MLA_ROPE // 2
    q_pieces, qr_pieces, k_pieces, v_pieces = [], [], [], []
    for h in range(N_HEADS):
        o = h * MLA_QK
        q_pieces += [(o, MLA_QK, 1.0), (None, LANES - MLA_QK, 1.0)]
        qr_pieces += [(None, MLA_NOPE, 1.0), (o + MLA_NOPE + half, half, 1.0), (o + MLA_NOPE, half, 1.0),
                      (None, LANES - MLA_QK, 1.0)]
        k_pieces += [(h * 128, MLA_NOPE, 1.0), (None, LANES - MLA_NOPE, 1.0)]
        v_pieces += [(h * 128 + MLA_NOPE, 64, 1.0)]
    scale = MLA_QK ** -0.5
    qg, kg = mla_q_g[l], mla_k_g[l]
    rot_gain = lambda g: jnp.concatenate([g[MLA_NOPE + half:], g[MLA_NOPE:MLA_NOPE + half]])
    dq_g = jnp.concatenate([jnp.concatenate([dsa_q_g[l], jnp.zeros((HEAD_DIM,), F32)]) if h % 2 == 0 else
                            jnp.concatenate([jnp.zeros((HEAD_DIM,), F32), dsa_q_g[l]]) for h in range(N_HEADS)])
    return dict(
        norm_g=norm_g[l][None, :], w_bf=w_bf[l], w_f32=w_f32[l],
        wuq=_assemble(uq, q_pieces).astype(BF16), wuqr=_assemble(uq, qr_pieces).astype(BF16),
        wuk=_assemble(ukv, k_pieces).astype(BF16), wuv=_assemble(ukv, v_pieces).astype(BF16),
        qng=mla_q_norm_g[l][None, :], kvng=mla_kv_norm_g[l][None, :],
        gq=_lane_table({0: qg * scale}), gqr=_lane_table({MLA_NOPE: rot_gain(qg) * scale}),
        gk=_lane_table({0: kg}), gkr=_lane_table({MLA_NOPE: rot_gain(kg)}),
        gdq=(dq_g * HEAD_DIM ** -0.5)[None, :], gdk=jnp.concatenate([dsa_k_g[l], dsa_k_g[l]])[None, :],
        gate_b=gate_b[l], wbr=w_branch[l].astype(BF16), wout=w_out[l].astype(BF16))


def kernel(x, positions, norm_g, w_in, mla_q_norm_g, mla_kv_norm_g, mla_w_uq, mla_w_ukv, mla_q_g, mla_k_g,
           dsa_q_g, dsa_k_g, rel_bias, gate_b, w_branch, w_out):
    b, s, d = x.shape
    depth = w_in.shape[0]
    m = b * s
    assert d == D_MODEL and w_in.shape[-1] == D_IN and s % 512 == 0 and s & (s - 1) == 0

    half = MLA_ROPE // 2
    inv_freq = ROPE_THETA ** (-(jnp.arange(half, dtype=F32) * 2.0) / MLA_ROPE)
    ang = positions.astype(F32)[..., None] * inv_freq
    cos, sin = jnp.cos(ang).reshape(m, half), jnp.sin(ang).reshape(m, half)
    cos_t = jnp.concatenate([jnp.ones((m, MLA_NOPE), F32), cos, cos, jnp.ones((m, LANES - MLA_QK), F32)], axis=1)
    sin_t = jnp.concatenate([jnp.zeros((m, MLA_NOPE), F32), -sin, sin, jnp.zeros((m, LANES - MLA_QK), F32)], axis=1)

    bf_pieces, f32_pieces = _in_proj_layouts()
    w_in16 = w_in.astype(BF16)
    w_bf = _assemble(w_in16, bf_pieces)
    w_f32 = _assemble(w_in16, f32_pieces)

    tm = min(1024, m)
    tp = min(512, m)
    dsa_tc = 512
    mla_tc = 512
    pos_rows = positions.reshape(b, s // LANES, 1, LANES)
    dist_table = rel_bias[BUCKET_OF_DIST, :].T
    x2 = x.reshape(m, d)
    for l in range(depth):
        lw = _layer_weights(l, w_bf, w_f32, norm_g, mla_q_norm_g, mla_kv_norm_g, mla_w_uq, mla_w_ukv,
                            mla_q_g, mla_k_g, dsa_q_g, dsa_k_g, gate_b, w_branch, w_out)
        pb = _in_proj(x2, lw["norm_g"], lw["w_bf"], BF16, min(2 * tm, m), 1024)
        pf = _in_proj(x2, lw["norm_g"], lw["w_f32"], F32, tm, 1024)
        qb, kb, vb, dq, dkk = _prologue(pf, cos_t, sin_t, lw, tp)
        ya = _sb_attention(pb, b, s, 128)
        vb5 = vb.reshape(b, s // mla_tc, mla_tc, N_HEADS // 2, LANES)
        ones_pad5 = jnp.concatenate([jnp.ones(vb5.shape[:-1] + (1,), BF16), jnp.zeros(vb5.shape[:-1] + (7,), BF16)], -1)
        vbt = jnp.concatenate([vb5, ones_pad5], axis=-1).transpose(0, 1, 3, 4, 2)
        yb = _mla_attention(qb, kb, vbt, b, s, LANES, mla_tc)
        vals = pb.reshape(b, s // dsa_tc, dsa_tc, N_BF)[..., B_DSV:B_DSV + HEAD_DIM]
        ones_pad = jnp.concatenate([jnp.ones(vals.shape[:-1] + (1,), BF16), jnp.zeros(vals.shape[:-1] + (7,), BF16)], -1)
        vvt = jnp.concatenate([vals, ones_pad], axis=-1).transpose(0, 1, 3, 2)
        yc = _dsa_attention(rel_bias, dist_table, pb, pf, dq, dkk, vvt, pos_rows, b, s, LANES, dsa_tc)
        x2 = _merge(x2, ya.reshape(m, -1), yb.reshape(m, -1), yc.reshape(m, -1), pb, lw["gate_b"],
                    lw["wbr"], lw["wout"], tp)
    return x2.reshape(b, s, d)
```

```python
import functools
import math

import numpy as np
import jax
import jax.numpy as jnp
from jax import lax
from jax.experimental import pallas as pl
from jax.experimental.pallas import tpu as pltpu

F32 = jnp.float32
BF16 = jnp.bfloat16
I32 = jnp.int32

LANES = 128
EPS = 1e-6
D_MODEL = 1024
N_HEADS = 8
HEAD_DIM = 64
MLA_Q_LORA = 256
MLA_KV_LORA = 128
MLA_NOPE = 64
MLA_ROPE = 32
MLA_QK = MLA_NOPE + MLA_ROPE
ROPE_THETA = 10000.0
IDX_DIM = 32
TOPK_MAX = 256
REL_BUCKETS = 32
REL_MAX_DIST = 128
N_BRANCH = 3
BRANCH_WIDTH = 512

O_SAQ, O_SAK, O_SAV = 0, 512, 1024
O_CQ, O_CKV, O_KPE = 1536, 1792, 1920
O_DSQ, O_DSK, O_DSV = 1952, 2464, 2528
O_IXQ, O_IXK, O_IXW = 2592, 2848, 2880
O_Z, O_G = 2888, 4424
D_IN = 7496

NEG_BIG = -1e30
INT_MIN = -(2 ** 31)
SB_UNDERFLOW = 104.0
VMEM_LIMIT = 48 * 1024 * 1024

NT_DIMS = (((1,), (1,)), ((), ()))
V_ROWS = HEAD_DIM + 8


def _t5_buckets():
    exact = REL_BUCKETS // 2
    n = np.arange(0, 4 * REL_MAX_DIST)
    nf = np.maximum(n, 1).astype(np.float64)
    large = exact + (np.log(nf / exact) / math.log(REL_MAX_DIST / exact) * (REL_BUCKETS - exact)).astype(np.int64)
    bucket = np.where(n < exact, n, np.minimum(large, REL_BUCKETS - 1))
    assert np.all(np.diff(bucket) >= 0) and np.all(bucket[LANES - 1:] == REL_BUCKETS - 1)
    return bucket[:LANES]


BUCKET_OF_DIST = _t5_buckets()
FAR_DIST = int(np.argmax(BUCKET_OF_DIST == REL_BUCKETS - 1))


def _assemble(w, pieces):
    cols = []
    for start, width, scale in pieces:
        if start is None:
            cols.append(jnp.zeros(w.shape[:-1] + (width,), w.dtype))
        else:
            c = w[..., start:start + width]
            cols.append(c if scale == 1.0 else c * scale)
    return jnp.concatenate(cols, axis=-1)


def _head_padded(start, scale=1.0):
    pieces = []
    for h in range(N_HEADS):
        col = (start + h * HEAD_DIM, HEAD_DIM, scale)
        pad = (None, HEAD_DIM, 1.0)
        pieces += [col, pad] if h % 2 == 0 else [pad, col]
    return pieces


def _in_proj_layouts():
    bf = (_head_padded(O_SAQ, HEAD_DIM ** -0.5)
          + [(O_SAK, 512, 1.0), (O_SAV, 512, 1.0), (O_IXQ, 256, 1.0)]
          + [(O_DSV, 64, 1.0)] * 2
          + [(O_IXK, 32, 1.0)] * 4
          + [(O_Z, 1536, 1.0), (O_G, 3072, 1.0)])
    f32 = (_head_padded(O_DSQ)
           + [(O_CQ, 256, 1.0), (O_CKV, 128, 1.0)]
           + [(None, 64, 1.0), (O_KPE, 32, 1.0), (None, 32, 1.0)]
           + [(None, 64, 1.0), (O_KPE + 16, 16, 1.0), (O_KPE, 16, 1.0), (None, 32, 1.0)]
           + [(O_DSK, 64, 1.0)] * 2
           + [(O_IXW, 8, 1.0), (None, 120, 1.0)]
           + [(None, 128, 1.0)])
    return bf, f32


B_SAQ, B_SAK, B_SAV, B_IXQ, B_DSV, B_IXK, B_Z, B_G, N_BF = 0, 1024, 1536, 2048, 2304, 2432, 2560, 4096, 7168
F_DSQ, F_CQ, F_CKV, F_KPE, F_KPER, F_DSK, F_IXW, N_F32 = 0, 1024, 1280, 1408, 1536, 1664, 1792, 2048


def _in_proj_kernel(x_ref, g_ref, w_ref, o_ref, h_ref):
    @pl.when(pl.program_id(1) == 0)
    def _():
        x = x_ref[...]
        ms = jnp.mean(x * x, axis=-1, keepdims=True)
        h_ref[...] = (x * lax.rsqrt(ms + EPS) * g_ref[...]).astype(BF16)

    o_ref[...] = jnp.dot(h_ref[...], w_ref[...], preferred_element_type=F32).astype(o_ref.dtype)


def _in_proj(x2, g, w, col0, n, out_dtype, tm, tn):
    m, d = x2.shape
    return pl.pallas_call(
        _in_proj_kernel,
        grid=(m // tm, n // tn),
        in_specs=[pl.BlockSpec((tm, d), lambda i, j: (i, 0)),
                  pl.BlockSpec((1, d), lambda i, j: (0, 0)),
                  pl.BlockSpec((d, tn), lambda i, j: (0, col0 // tn + j))],
        out_specs=pl.BlockSpec((tm, tn), lambda i, j: (i, j)),
        out_shape=jax.ShapeDtypeStruct((m, n), out_dtype),
        scratch_shapes=[pltpu.VMEM((tm, d), BF16)],
        compiler_params=pltpu.CompilerParams(dimension_semantics=("parallel", "arbitrary"),
                                             vmem_limit_bytes=VMEM_LIMIT),
        name="in_proj",
    )(x2, g, w)


def _rms(x, width):
    return lax.rsqrt(jnp.sum(x * x, axis=-1, keepdims=True) * (1.0 / width) + EPS)


def _prologue_kernel(dsq_ref, cq_ref, ckv_ref, kpe_ref, kper_ref, dsk_ref, cos_ref, sin_ref,
                     wuq_ref, wuqr_ref, wuk_ref, wuv_ref,
                     qng_ref, kvng_ref, gq_ref, gqr_ref, gk_ref, gkr_ref, gdq_ref, gdk_ref,
                     qb_ref, kb_ref, vb_ref, dq_ref, dkk_ref):
    cos = cos_ref[...]
    sin = sin_ref[...]

    cq = cq_ref[...]
    cqn = (cq * _rms(cq, MLA_Q_LORA) * qng_ref[...]).astype(BF16)
    q = jnp.dot(cqn, wuq_ref[...], preferred_element_type=F32)
    qr = jnp.dot(cqn, wuqr_ref[...], preferred_element_type=F32)
    cg = cos * gq_ref[...]
    sg = sin * gqr_ref[...]
    for h in range(N_HEADS):
        sl = slice(h * LANES, (h + 1) * LANES)
        qh = q[:, sl]
        qb_ref[:, sl] = ((qh * cg + qr[:, sl] * sg) * _rms(qh, MLA_QK)).astype(BF16)

    ckv = ckv_ref[...]
    ckvn = (ckv * _rms(ckv, MLA_KV_LORA) * kvng_ref[...]).astype(BF16)
    kn = jnp.dot(ckvn, wuk_ref[...], preferred_element_type=F32)
    vb_ref[...] = jnp.dot(ckvn, wuv_ref[...], preferred_element_type=F32).astype(BF16)
    kpe = kpe_ref[...]
    cgk = cos * gk_ref[...]
    rot = kper_ref[...] * (sin * gkr_ref[...])
    for h in range(N_HEADS):
        sl = slice(h * LANES, (h + 1) * LANES)
        kf = kn[:, sl] + kpe
        kb_ref[:, sl] = ((kf * cgk + rot) * _rms(kf, MLA_QK)).astype(BF16)

    for h in range(N_HEADS):
        sl = slice(h * LANES, (h + 1) * LANES)
        xq = dsq_ref[:, sl]
        dq_ref[:, sl] = (xq * _rms(xq, HEAD_DIM) * gdq_ref[:, sl]).astype(BF16)
    xk = dsk_ref[...]
    dkk_ref[...] = (xk * _rms(xk, 2 * HEAD_DIM) * gdk_ref[...]).astype(BF16)


def _prologue(pf, cos_t, sin_t, lw, tm):
    m = pf.shape[0]

    def act(width, off):
        return pl.BlockSpec((tm, width), lambda i, _b=off // width: (i, _b))

    def whole(a):
        return pl.BlockSpec(a.shape, lambda i: (0,) * a.ndim)

    weights = [lw[k] for k in ("wuq", "wuqr", "wuk", "wuv", "qng", "kvng", "gq", "gqr", "gk", "gkr", "gdq", "gdk")]
    row = lambda width: pl.BlockSpec((tm, width), lambda i: (i, 0))
    return pl.pallas_call(
        _prologue_kernel,
        grid=(m // tm,),
        in_specs=[act(1024, F_DSQ), act(256, F_CQ), act(128, F_CKV), act(128, F_KPE), act(128, F_KPER),
                  act(128, F_DSK), row(LANES), row(LANES)] + [whole(w) for w in weights],
        out_specs=[row(1024), row(1024), row(512), row(1024), row(LANES)],
        out_shape=[jax.ShapeDtypeStruct((m, 1024), BF16), jax.ShapeDtypeStruct((m, 1024), BF16),
                   jax.ShapeDtypeStruct((m, 512), BF16), jax.ShapeDtypeStruct((m, 1024), BF16),
                   jax.ShapeDtypeStruct((m, LANES), BF16)],
        compiler_params=pltpu.CompilerParams(dimension_semantics=("parallel",), vmem_limit_bytes=VMEM_LIMIT),
        name="prologue",
    )(pf, pf, pf, pf, pf, pf, cos_t, sin_t, *weights)


def _sublane_all(x, op):
    for shift in (4, 2, 1):
        x = op(x, pltpu.roll(x, shift, 0))
    return x


def _sb_kernel(q_ref, k_ref, v_ref, o_ref, acc_ref, carry_ref, hl_ref, lb_ref, *, tq):
    t = LANES
    qi = pl.program_id(1)
    jj = lax.broadcasted_iota(I32, (t, 2 * t), 0)
    ss = lax.broadcasted_iota(I32, (t, 2 * t), 1)
    tri_ones = jnp.logical_or(ss >= t, jj > ss).astype(BF16)
    q_idx = qi * tq + lax.broadcasted_iota(I32, (tq, t), 0)
    k_lane = lax.broadcasted_iota(I32, (tq, t), 1)

    acc_ref[...] = jnp.zeros(acc_ref.shape, F32)
    carry_ref[...] = jnp.zeros(carry_ref.shape, F32)

    def cond(c):
        kb, cmax = c
        return jnp.logical_and(kb >= 0, cmax > -SB_UNDERFLOW)

    def body(c):
        kb, _ = c
        start = pl.multiple_of(kb * t, t)
        mask = (start + k_lane) < q_idx
        for h in range(N_HEADS):
            q = q_ref[0, :, h * LANES:(h + 1) * LANES]
            k = k_ref[0, pl.ds(start, t), (h // 2) * LANES:(h // 2 + 1) * LANES]
            z = lax.dot_general(q, k, NT_DIMS, preferred_element_type=F32)
            sp = jnp.maximum(z, 0.0) + jnp.log(1.0 + jnp.exp(-jnp.abs(z)))
            l1m = jnp.where(mask, -sp, 0.0)
            hi = l1m.astype(BF16)
            hl_ref[2 * h * tq:(2 * h + 1) * tq, :] = hi
            hl_ref[(2 * h + 1) * tq:(2 * h + 2) * tq, :] = (l1m - hi.astype(F32)).astype(BF16)
            lb_ref[h * tq:(h + 1) * tq, :] = z - sp
        sums = jnp.dot(hl_ref[...], tri_ones, preferred_element_type=F32)
        for h in range(N_HEADS):
            rows = slice(h * tq, (h + 1) * tq)
            both = sums[2 * h * tq:(2 * h + 1) * tq, :] + sums[(2 * h + 1) * tq:(2 * h + 2) * tq, :]
            carry = carry_ref[rows, :]
            a = jnp.where(mask, jnp.exp(lb_ref[rows, :] + carry + both[:, :t]), 0.0)
            v = v_ref[0, pl.ds(start, t), (h // 2) * LANES:(h // 2 + 1) * LANES]
            acc_ref[rows, :] += jnp.dot(a.astype(BF16), v, preferred_element_type=F32)
            carry_ref[rows, :] = carry + both[:, t:]
        return kb - 1, jnp.max(carry_ref[...])

    lax.while_loop(cond, body, (((qi + 1) * tq) // t - 1, jnp.float32(0.0)))

    lane = lax.broadcasted_iota(I32, (tq, LANES), 1)
    for p in range(N_HEADS // 2):
        even = acc_ref[2 * p * tq:(2 * p + 1) * tq, :]
        odd = acc_ref[(2 * p + 1) * tq:(2 * p + 2) * tq, :]
        o_ref[0, :, p * LANES:(p + 1) * LANES] = jnp.where(lane < HEAD_DIM, even, odd)


def _sb_attention(pb, b, s, tq):
    pb3 = pb.reshape(b, s, N_BF)
    kernel = functools.partial(_sb_kernel, tq=tq)
    return pl.pallas_call(
        kernel,
        grid=(b, s // tq),
        in_specs=[pl.BlockSpec((1, tq, N_HEADS * LANES), lambda bi, i: (bi, i, B_SAQ // (N_HEADS * LANES))),
                  pl.BlockSpec((1, s, BRANCH_WIDTH), lambda bi, i: (bi, 0, B_SAK // BRANCH_WIDTH)),
                  pl.BlockSpec((1, s, BRANCH_WIDTH), lambda bi, i: (bi, 0, B_SAV // BRANCH_WIDTH))],
        out_specs=pl.BlockSpec((1, tq, BRANCH_WIDTH), lambda bi, i: (bi, i, 0)),
        out_shape=jax.ShapeDtypeStruct((b, s, BRANCH_WIDTH), F32),
        scratch_shapes=[pltpu.VMEM((N_HEADS * tq, LANES), F32),
                        pltpu.VMEM((N_HEADS * tq, LANES), F32),
                        pltpu.VMEM((2 * N_HEADS * tq, LANES), BF16),
                        pltpu.VMEM((N_HEADS * tq, LANES), F32)],
        compiler_params=pltpu.CompilerParams(dimension_semantics=("parallel", "arbitrary"),
                                             vmem_limit_bytes=VMEM_LIMIT),
        name="sb_attention",
    )(pb3, pb3, pb3)


def _mla_kernel(q_ref, k_ref, vt_ref, o_ref, qbd_ref, p_ref, m_ref, alpha_ref, acc_ref, *, tq, tc):
    t = LANES
    assert tq == t
    groups = tc // 8
    pair_w = 2 * tq
    qi = pl.program_id(1)
    n_chunks = (qi * tq) // tc + 1
    k_in_chunk = (lax.broadcasted_iota(I32, (groups, 8, tq), 0) * 8
                  + lax.broadcasted_iota(I32, (groups, 8, tq), 1))
    q_lane = qi * tq + lax.broadcasted_iota(I32, (groups, 8, tq), 2)

    zeros = jnp.zeros((tq, t), BF16)
    for p in range(N_HEADS // 2):
        q_even = q_ref[0, :, 2 * p * LANES:(2 * p + 1) * LANES]
        q_odd = q_ref[0, :, (2 * p + 1) * LANES:(2 * p + 2) * LANES]
        qbd_ref[p, 0:tq, :] = jnp.concatenate([q_even, zeros], axis=1)
        qbd_ref[p, tq:2 * tq, :] = jnp.concatenate([zeros, q_odd], axis=1)

    v_rows = acc_ref.shape[0]
    m_ref[...] = jnp.full(m_ref.shape, NEG_BIG, F32)
    acc_ref[...] = jnp.zeros(acc_ref.shape, F32)

    def chunk(c, diagonal):
        start = pl.multiple_of(c * tc, tc)
        pair_cols = [slice(p * pair_w, (p + 1) * pair_w) for p in range(N_HEADS // 2)]
        s_pairs = [lax.dot_general(k_ref[0, pl.ds(start, tc), pair_cols[p]], qbd_ref[p], NT_DIMS,
                                   preferred_element_type=F32) for p in range(N_HEADS // 2)]
        for p in range(N_HEADS // 2):
            pcols = pair_cols[p]
            for i in range(2):
                cols = slice(p * pair_w + i * tq, p * pair_w + (i + 1) * tq)
                sb = s_pairs[p][:, i * tq:(i + 1) * tq].reshape(groups, 8, tq)
                if diagonal:
                    sb = jnp.where(start + k_in_chunk <= q_lane, sb, NEG_BIG)
                m_old = m_ref[:, cols]
                m_new = jnp.maximum(m_old, _sublane_all(jnp.max(sb, axis=0), jnp.maximum))
                prob = jnp.exp(sb - m_new[None])
                alpha = jnp.exp(m_old - m_new)
                p_ref[:, cols] = prob.reshape(tc, tq).astype(BF16)
                alpha_ref[:, cols] = alpha
                m_ref[:, cols] = m_new
            pv = jnp.dot(vt_ref[0, c, p], p_ref[:, pcols], preferred_element_type=F32)
            scaled = acc_ref[:, pcols].reshape(v_rows // 8, 8, pair_w) * alpha_ref[:, pcols][None]
            acc_ref[:, pcols] = scaled.reshape(v_rows, pair_w) + pv

    def body(c, carry):
        chunk(c, False)
        return carry

    lax.fori_loop(0, n_chunks - 1, body, 0)
    chunk(n_chunks - 1, True)

    for p in range(N_HEADS // 2):
        outs = []
        for i in range(2):
            cols = slice(p * pair_w + i * tq, p * pair_w + (i + 1) * tq)
            outs.append(acc_ref[i * HEAD_DIM:(i + 1) * HEAD_DIM, cols] / acc_ref[t:t + 1, cols])
        o_ref[0, :, p * LANES:(p + 1) * LANES] = jnp.transpose(jnp.concatenate(outs, axis=0))


def _mla_attention(qb, kb, vbt, b, s, tq, tc):
    width = N_HEADS * tq
    kernel = functools.partial(_mla_kernel, tq=tq, tc=tc)
    return pl.pallas_call(
        kernel,
        grid=(b, s // tq),
        in_specs=[pl.BlockSpec((1, tq, N_HEADS * LANES), lambda bi, i: (bi, i, 0)),
                  pl.BlockSpec((1, s, N_HEADS * LANES), lambda bi, i: (bi, 0, 0)),
                  pl.BlockSpec((1, s // tc, N_HEADS // 2, LANES + 8, tc), lambda bi, i: (bi, 0, 0, 0, 0))],
        out_specs=pl.BlockSpec((1, tq, BRANCH_WIDTH), lambda bi, i: (bi, i, 0)),
        out_shape=jax.ShapeDtypeStruct((b, s, BRANCH_WIDTH), F32),
        scratch_shapes=[pltpu.VMEM((N_HEADS // 2, 2 * tq, 2 * LANES), BF16),
                        pltpu.VMEM((tc, width), BF16),
                        pltpu.VMEM((8, width), F32),
                        pltpu.VMEM((8, width), F32),
                        pltpu.VMEM((LANES + 8, width), F32)],
        compiler_params=pltpu.CompilerParams(dimension_semantics=("parallel", "arbitrary"),
                                             vmem_limit_bytes=VMEM_LIMIT),
        name="mla_attention",
    )(qb.reshape(b, s, -1), kb.reshape(b, s, -1), vbt)


def _sortable(x):
    bits = pltpu.bitcast(x + 0.0, I32)
    return bits ^ ((bits >> 31) & 0x7FFFFFFF)


def _dsa_kernel(rel_ref, tab_ref, ixq_ref, ixw_ref, ixk_ref, q_ref, kk_ref, vvt_ref, pos_ref, o_ref,
                keys_ref, qm_ref, qs_ref, wt_ref, p_ref, selb_ref, add_ref,
                m_ref, alpha_ref, acc_ref, thr_ref, *, tq, tc, topk, seq):
    t = LANES
    assert tq == t
    groups = tc // 8
    width = N_HEADS * tq
    qi = pl.program_id(1)
    n_chunks = (qi * tq) // tc + 1
    head_cols = [slice(h * tq, (h + 1) * tq) for h in range(N_HEADS)]
    lane = lax.broadcasted_iota(I32, (tq, t), 1)
    q_lane = qi * tq + lax.broadcasted_iota(I32, (groups, 8, tq), 2)
    k_in_chunk = (lax.broadcasted_iota(I32, (groups, 8, tq), 0) * 8
                  + lax.broadcasted_iota(I32, (groups, 8, tq), 1))

    for h in range(N_HEADS):
        grp = ixq_ref[0, :, (h // 4) * LANES:(h // 4 + 1) * LANES]
        lo = (h % 4) * IDX_DIM
        qm_ref[h * tq:(h + 1) * tq, :] = jnp.where((lane >= lo) & (lane < lo + IDX_DIM), grp, jnp.zeros_like(grp))
        qs_ref[h * tq:(h + 1) * tq, :] = q_ref[0, :, h * LANES:(h + 1) * LANES]
    wt_ref[...] = jnp.transpose(ixw_ref[0] * ((IDX_DIM ** -0.5) * (N_HEADS ** -0.5)))

    def chunk3(ref, c):
        start = pl.multiple_of(c * tc, tc)
        return ref[pl.ds(start, tc), :].reshape(groups, 8, tq)

    def score_chunk(c, carry):
        start = pl.multiple_of(c * tc, tc)
        st = lax.dot_general(ixk_ref[0, pl.ds(start, tc), :], qm_ref[...], NT_DIMS, preferred_element_type=F32)
        sc = None
        for h in range(N_HEADS):
            term = jnp.maximum(st[:, head_cols[h]], 0.0) * wt_ref[h:h + 1, :]
            sc = term if sc is None else sc + term
        key = jnp.where(start + k_in_chunk <= q_lane, _sortable(sc).reshape(groups, 8, tq), INT_MIN)
        keys_ref[pl.ds(start, tc), :] = key.reshape(tc, tq)
        return carry

    lax.fori_loop(0, n_chunks, score_chunk, 0)

    def count(pred):
        def one(c, cnt):
            return cnt + jnp.sum(pred(chunk3(keys_ref, c), c * tc).astype(I32), axis=0)

        def two(i, cnt):
            return one(2 * i + 1, one(2 * i, cnt))

        cnt = lax.fori_loop(0, n_chunks // 2, two, jnp.zeros((8, tq), I32))
        cnt = lax.fori_loop(2 * (n_chunks // 2), n_chunks, one, cnt)
        return _sublane_all(cnt, jnp.add)

    @pl.when((qi + 1) * tq <= topk)
    def _():
        thr_ref[0] = jnp.full((8, tq), INT_MIN, I32)
        thr_ref[1] = jnp.full((8, tq), -1, I32)

    @pl.when((qi + 1) * tq > topk)
    def _():
        ans = jnp.zeros((8, tq), I32)
        n_ge = jnp.full((8, tq), seq, I32)
        for bit in range(31, -1, -1):
            cand = ans | np.int32(INT_MIN if bit == 31 else 1 << bit)
            cand_s = cand ^ np.int32(INT_MIN)
            c = count(lambda keys, start, cand_s=cand_s: keys >= cand_s[None])
            take = c >= topk
            ans = jnp.where(take, cand, ans)
            n_ge = jnp.where(take, c, n_ge)
        tau = ans ^ np.int32(INT_MIN)
        thr_ref[0] = tau
        thr_ref[1] = jnp.full((8, tq), seq, I32)

        @pl.when(jnp.max(n_ge) > topk)
        def _():
            need = topk - count(lambda keys, start: keys > tau[None])
            cut = jnp.zeros((8, tq), I32)
            for bit in range(int(math.log2(seq)), -1, -1):
                cand = cut | np.int32(1 << bit)
                c = count(lambda keys, start, cand=cand: (keys == tau[None]) & (start + k_in_chunk < cand[None]))
                cut = jnp.where(c < need, cand, cut)
            thr_ref[1] = cut

    tau = thr_ref[0]
    cut = thr_ref[1]

    v_rows = acc_ref.shape[0]
    m_ref[...] = jnp.full(m_ref.shape, NEG_BIG, F32)
    acc_ref[...] = jnp.zeros(acc_ref.shape, F32)
    pq = pos_ref[0, qi]
    pq_min = jnp.min(pq)
    last_bias = [rel_ref[REL_BUCKETS - 1, h] for h in range(N_HEADS)]

    def softmax_head(h, s_t, addend, bias):
        cols = head_cols[h]
        sb = s_t.reshape(groups, 8, tq) + addend
        m_old = m_ref[:, cols]
        m_new = jnp.maximum(m_old, _sublane_all(jnp.max(sb, axis=0), jnp.maximum) + bias)
        p = jnp.exp(sb - (m_new - bias)[None])
        alpha = jnp.exp(m_old - m_new)
        p_ref[:, cols] = p.reshape(tc, tq).astype(BF16)
        alpha_ref[:, cols] = alpha
        m_ref[:, cols] = m_new

    def accumulate(c, half):
        cols = slice(half * (width // 2), (half + 1) * (width // 2))
        pv = jnp.dot(vvt_ref[0, c], p_ref[:, cols], preferred_element_type=F32)
        scaled = acc_ref[:, cols].reshape(v_rows // 8, 8, width // 2) * alpha_ref[:, cols][None]
        acc_ref[:, cols] = scaled.reshape(v_rows, width // 2) + pv

    def all_heads(c, addend_of, bias_of):
        start = pl.multiple_of(c * tc, tc)
        s_all = lax.dot_general(kk_ref[0, pl.ds(start, tc), :], qs_ref[...], NT_DIMS, preferred_element_type=F32)
        for half in range(2):
            for h in range(half * N_HEADS // 2, (half + 1) * N_HEADS // 2):
                softmax_head(h, s_all[:, head_cols[h]], addend_of(h), bias_of(h))
            accumulate(c, half)

    def attend_chunk(c, carry):
        keys = chunk3(keys_ref, c)
        sel = (keys > tau[None]) | ((keys == tau[None]) & (k_in_chunk <= (cut - c * tc)[None]))
        selb_ref[...] = jnp.where(sel, 0.0, NEG_BIG).reshape(tc, tq)
        first_tile = c * (tc // t)
        pk_max = pos_ref[0, first_tile]
        for j in range(1, tc // t):
            pk_max = jnp.maximum(pk_max, pos_ref[0, first_tile + j])
        chunk_far = pq_min - jnp.max(pk_max) >= FAR_DIST

        @pl.when(chunk_far)
        def _():
            all_heads(c, lambda h: selb_ref[...].reshape(groups, 8, tq), lambda h: last_bias[h])

        @pl.when(jnp.logical_not(chunk_far))
        def _():
            for j in range(tc // t):
                rows = slice(j * t, (j + 1) * t)
                pk = pos_ref[0, first_tile + j]
                tile_far = jnp.logical_or(pq_min - jnp.max(pk) >= FAR_DIST, c * tc + j * t >= (qi + 1) * tq)

                @pl.when(tile_far)
                def _():
                    for h in range(N_HEADS):
                        add_ref[rows, head_cols[h]] = selb_ref[rows, :] + last_bias[h]

                @pl.when(jnp.logical_not(tile_far))
                def _():
                    pk_col = jnp.transpose(jnp.broadcast_to(pk, (t, t)))
                    dist = jnp.clip(pq - pk_col, 0, t - 1)
                    for h in range(N_HEADS):
                        table = jnp.broadcast_to(tab_ref[h:h + 1, :], (t, t))
                        bias = jnp.take_along_axis(table, dist, axis=1, mode="promise_in_bounds")
                        add_ref[rows, head_cols[h]] = selb_ref[rows, :] + bias

            all_heads(c, lambda h: add_ref[:, head_cols[h]].reshape(groups, 8, tq), lambda h: 0.0)

        return carry

    lax.fori_loop(0, n_chunks, attend_chunk, 0)

    for p in range(N_HEADS // 2):
        outs = []
        for h in (2 * p, 2 * p + 1):
            cols = head_cols[h]
            outs.append(acc_ref[0:HEAD_DIM, cols] / acc_ref[HEAD_DIM:HEAD_DIM + 1, cols])
        o_ref[0, :, p * LANES:(p + 1) * LANES] = jnp.transpose(jnp.concatenate(outs, axis=0))


def _dsa_attention(rel_bias, dist_table, pb, pf, dq, dkk, vvt, pos_rows, b, s, tq, tc):
    topk = min(TOPK_MAX, s // 4)
    pb3 = pb.reshape(b, s, N_BF)
    pf3 = pf.reshape(b, s, N_F32)
    width = N_HEADS * tq
    kernel = functools.partial(_dsa_kernel, tq=tq, tc=tc, topk=topk, seq=s)
    return pl.pallas_call(
        kernel,
        grid=(b, s // tq),
        in_specs=[pl.BlockSpec(memory_space=pltpu.SMEM),
                  pl.BlockSpec(dist_table.shape, lambda bi, i: (0, 0)),
                  pl.BlockSpec((1, tq, 2 * LANES), lambda bi, i: (bi, i, B_IXQ // 256)),
                  pl.BlockSpec((1, tq, LANES), lambda bi, i: (bi, i, F_IXW // LANES)),
                  pl.BlockSpec((1, s, LANES), lambda bi, i: (bi, 0, B_IXK // LANES)),
                  pl.BlockSpec((1, tq, N_HEADS * LANES), lambda bi, i: (bi, i, 0)),
                  pl.BlockSpec((1, s, LANES), lambda bi, i: (bi, 0, 0)),
                  pl.BlockSpec((1, s // tc, V_ROWS, tc), lambda bi, i: (bi, 0, 0, 0)),
                  pl.BlockSpec((1, s // LANES, 1, LANES), lambda bi, i: (bi, 0, 0, 0))],
        out_specs=pl.BlockSpec((1, tq, BRANCH_WIDTH), lambda bi, i: (bi, i, 0)),
        out_shape=jax.ShapeDtypeStruct((b, s, BRANCH_WIDTH), F32),
        scratch_shapes=[pltpu.VMEM((s, tq), I32),
                        pltpu.VMEM((width, LANES), BF16),
                        pltpu.VMEM((width, LANES), BF16),
                        pltpu.VMEM((LANES, tq), F32),
                        pltpu.VMEM((tc, width), BF16),
                        pltpu.VMEM((tc, tq), F32),
                        pltpu.VMEM((tc, width), F32),
                        pltpu.VMEM((8, width), F32),
                        pltpu.VMEM((8, width), F32),
                        pltpu.VMEM((V_ROWS, width), F32),
                        pltpu.VMEM((2, 8, tq), I32)],
        compiler_params=pltpu.CompilerParams(dimension_semantics=("parallel", "arbitrary"),
                                             vmem_limit_bytes=VMEM_LIMIT),
        name="dsa_attention",
    )(rel_bias, dist_table, pb3, pf3, pb3, dq.reshape(b, s, -1), dkk.reshape(b, s, -1), vvt, pos_rows)


def _merge_kernel(x_ref, ya_ref, yb_ref, yc_ref, za_ref, zb_ref, zc_ref, ga_ref, gb_ref, gc_ref,
                  bias_ref, wbr_ref, wout_ref, o_ref):
    merged = None
    branches = ((ya_ref, za_ref, ga_ref), (yb_ref, zb_ref, gb_ref), (yc_ref, zc_ref, gc_ref))
    for n, (y_ref, z_ref, g_ref) in enumerate(branches):
        z = z_ref[...].astype(F32)
        branch = (y_ref[...] * (z * jax.nn.sigmoid(z))).astype(BF16)
        up = jnp.dot(branch, wbr_ref[n], preferred_element_type=F32)
        gate = jax.nn.sigmoid(g_ref[...].astype(F32) + bias_ref[n:n + 1, :])
        merged = gate * up if merged is None else merged + gate * up
    o_ref[...] = x_ref[...] + jnp.dot(merged.astype(BF16), wout_ref[...], preferred_element_type=F32)


def _merge(x2, ya, yb, yc, pb, gate_b, wbr, wout, tm):
    m, d = x2.shape
    row = lambda width: pl.BlockSpec((tm, width), lambda i: (i, 0))
    col = lambda width, off: pl.BlockSpec((tm, width), lambda i, _b=off // width: (i, _b))
    z_specs = [col(BRANCH_WIDTH, B_Z + n * BRANCH_WIDTH) for n in range(N_BRANCH)]
    g_specs = [col(d, B_G + n * d) for n in range(N_BRANCH)]
    return pl.pallas_call(
        _merge_kernel,
        grid=(m // tm,),
        in_specs=[row(d), row(BRANCH_WIDTH), row(BRANCH_WIDTH), row(BRANCH_WIDTH)] + z_specs + g_specs + [
            pl.BlockSpec(gate_b.shape, lambda i: (0, 0)),
            pl.BlockSpec(wbr.shape, lambda i: (0, 0, 0)),
            pl.BlockSpec(wout.shape, lambda i: (0, 0))],
        out_specs=row(d),
        out_shape=jax.ShapeDtypeStruct((m, d), F32),
        compiler_params=pltpu.CompilerParams(dimension_semantics=("parallel",), vmem_limit_bytes=VMEM_LIMIT),
        name="merge",
    )(x2, ya, yb, yc, pb, pb, pb, pb, pb, pb, gate_b, wbr, wout)


def _lane_table(vals_by_lane):
    t = jnp.zeros((LANES,), F32)
    for start, v in vals_by_lane.items():
        t = t.at[start:start + v.shape[0]].set(v)
    return t[None, :]


def _layer_weights(l, w_all, norm_g, mla_q_norm_g, mla_kv_norm_g, mla_w_uq, mla_w_ukv, mla_q_g, mla_k_g,
                   dsa_q_g, dsa_k_g, gate_b, w_branch, w_out):
    uq, ukv = mla_w_uq[l], mla_w_ukv[l]
    half = MLA_ROPE // 2
    q_pieces, qr_pieces, k_pieces, v_pieces = [], [], [], []
    for h in range(N_HEADS):
        o = h * MLA_QK
        q_pieces += [(o, MLA_QK, 1.0), (None, LANES - MLA_QK, 1.0)]
        qr_pieces += [(None, MLA_NOPE, 1.0), (o + MLA_NOPE + half, half, 1.0), (o + MLA_NOPE, half, 1.0),
                      (None, LANES - MLA_QK, 1.0)]
        k_pieces += [(h * 128, MLA_NOPE, 1.0), (None, LANES - MLA_NOPE, 1.0)]
        v_pieces += [(h * 128 + MLA_NOPE, 64, 1.0)]
    scale = MLA_QK ** -0.5
    qg, kg = mla_q_g[l], mla_k_g[l]
    rot_gain = lambda g: jnp.concatenate([g[MLA_NOPE + half:], g[MLA_NOPE:MLA_NOPE + half]])
    dq_g = jnp.concatenate([jnp.concatenate([dsa_q_g[l], jnp.zeros((HEAD_DIM,), F32)]) if h % 2 == 0 else
                            jnp.concatenate([jnp.zeros((HEAD_DIM,), F32), dsa_q_g[l]]) for h in range(N_HEADS)])
    return dict(
        norm_g=norm_g[l][None, :], w_all=w_all[l],
        wuq=_assemble(uq, q_pieces).astype(BF16), wuqr=_assemble(uq, qr_pieces).astype(BF16),
        wuk=_assemble(ukv, k_pieces).astype(BF16), wuv=_assemble(ukv, v_pieces).astype(BF16),
        qng=mla_q_norm_g[l][None, :], kvng=mla_kv_norm_g[l][None, :],
        gq=_lane_table({0: qg * scale}), gqr=_lane_table({MLA_NOPE: rot_gain(qg) * scale}),
        gk=_lane_table({0: kg}), gkr=_lane_table({MLA_NOPE: rot_gain(kg)}),
        gdq=(dq_g * HEAD_DIM ** -0.5)[None, :], gdk=jnp.concatenate([dsa_k_g[l], dsa_k_g[l]])[None, :],
        gate_b=gate_b[l], wbr=w_branch[l].astype(BF16), wout=w_out[l].astype(BF16))


def kernel(x, positions, norm_g, w_in, mla_q_norm_g, mla_kv_norm_g, mla_w_uq, mla_w_ukv, mla_q_g, mla_k_g,
           dsa_q_g, dsa_k_g, rel_bias, gate_b, w_branch, w_out):
    b, s, d = x.shape
    depth = w_in.shape[0]
    m = b * s
    assert d == D_MODEL and w_in.shape[-1] == D_IN and s % 512 == 0 and s & (s - 1) == 0

    half = MLA_ROPE // 2
    inv_freq = ROPE_THETA ** (-(jnp.arange(half, dtype=F32) * 2.0) / MLA_ROPE)
    ang = positions.astype(F32)[..., None] * inv_freq
    cos, sin = jnp.cos(ang).reshape(m, half), jnp.sin(ang).reshape(m, half)
    cos_t = jnp.concatenate([jnp.ones((m, MLA_NOPE), F32), cos, cos, jnp.ones((m, LANES - MLA_QK), F32)], axis=1)
    sin_t = jnp.concatenate([jnp.zeros((m, MLA_NOPE), F32), -sin, sin, jnp.zeros((m, LANES - MLA_QK), F32)], axis=1)

    bf_pieces, f32_pieces = _in_proj_layouts()
    w_in16 = w_in.astype(BF16)
    w_all = _assemble(w_in16, bf_pieces + f32_pieces)

    tm = min(1024, m)
    tp = min(512, m)
    dsa_tc = 512
    mla_tc = 512
    pos_rows = positions.reshape(b, s // LANES, 1, LANES)
    dist_table = rel_bias[BUCKET_OF_DIST, :].T
    x2 = x.reshape(m, d)
    for l in range(depth):
        lw = _layer_weights(l, w_all, norm_g, mla_q_norm_g, mla_kv_norm_g, mla_w_uq, mla_w_ukv,
                            mla_q_g, mla_k_g, dsa_q_g, dsa_k_g, gate_b, w_branch, w_out)
        pb = _in_proj(x2, lw["norm_g"], lw["w_all"], 0, N_BF, BF16, min(2 * tm, m), 1024)
        pf = _in_proj(x2, lw["norm_g"], lw["w_all"], N_BF, N_F32, F32, tm, 1024)
        qb, kb, vb, dq, dkk = _prologue(pf, cos_t, sin_t, lw, tp)
        ya = _sb_attention(pb, b, s, 128)
        vb5 = vb.reshape(b, s // mla_tc, mla_tc, N_HEADS // 2, LANES)
        ones_pad5 = jnp.concatenate([jnp.ones(vb5.shape[:-1] + (1,), BF16), jnp.zeros(vb5.shape[:-1] + (7,), BF16)], -1)
        vbt = jnp.concatenate([vb5, ones_pad5], axis=-1).transpose(0, 1, 3, 4, 2)
        yb = _mla_attention(qb, kb, vbt, b, s, LANES, mla_tc)
        vals = pb.reshape(b, s // dsa_tc, dsa_tc, N_BF)[..., B_DSV:B_DSV + HEAD_DIM]
        ones_pad = jnp.concatenate([jnp.ones(vals.shape[:-1] + (1,), BF16), jnp.zeros(vals.shape[:-1] + (7,), BF16)], -1)
        vvt = jnp.concatenate([vals, ones_pad], axis=-1).transpose(0, 1, 3, 2)
        yc = _dsa_attention(rel_bias, dist_table, pb, pf, dq, dkk, vvt, pos_rows, b, s, LANES, dsa_tc)
        x2 = _merge(x2, ya.reshape(m, -1), yb.reshape(m, -1), yc.reshape(m, -1), pb, lw["gate_b"],
                    lw["wbr"], lw["wout"], tp)
    return x2.reshape(b, s, d)
```

```python
import functools
import math

import numpy as np
import jax
import jax.numpy as jnp
from jax import lax
from jax.experimental import pallas as pl
from jax.experimental.pallas import tpu as pltpu

F32 = jnp.float32
BF16 = jnp.bfloat16
I32 = jnp.int32

LANES = 128
EPS = 1e-6
D_MODEL = 1024
N_HEADS = 8
HEAD_DIM = 64
MLA_Q_LORA = 256
MLA_KV_LORA = 128
MLA_NOPE = 64
MLA_ROPE = 32
MLA_QK = MLA_NOPE + MLA_ROPE
ROPE_THETA = 10000.0
IDX_DIM = 32
TOPK_MAX = 256
REL_BUCKETS = 32
REL_MAX_DIST = 128
N_BRANCH = 3
BRANCH_WIDTH = 512

O_SAQ, O_SAK, O_SAV = 0, 512, 1024
O_CQ, O_CKV, O_KPE = 1536, 1792, 1920
O_DSQ, O_DSK, O_DSV = 1952, 2464, 2528
O_IXQ, O_IXK, O_IXW = 2592, 2848, 2880
O_Z, O_G = 2888, 4424
D_IN = 7496

NEG_BIG = -1e30
INT_MIN = -(2 ** 31)
SB_UNDERFLOW = 104.0
VMEM_LIMIT = 48 * 1024 * 1024

NT_DIMS = (((1,), (1,)), ((), ()))
V_ROWS = HEAD_DIM + 8


def _t5_buckets():
    exact = REL_BUCKETS // 2
    n = np.arange(0, 4 * REL_MAX_DIST)
    nf = np.maximum(n, 1).astype(np.float64)
    large = exact + (np.log(nf / exact) / math.log(REL_MAX_DIST / exact) * (REL_BUCKETS - exact)).astype(np.int64)
    bucket = np.where(n < exact, n, np.minimum(large, REL_BUCKETS - 1))
    assert np.all(np.diff(bucket) >= 0) and np.all(bucket[LANES - 1:] == REL_BUCKETS - 1)
    return bucket[:LANES]


BUCKET_OF_DIST = _t5_buckets()
FAR_DIST = int(np.argmax(BUCKET_OF_DIST == REL_BUCKETS - 1))


def _assemble(w, pieces):
    cols = []
    for start, width, scale in pieces:
        if start is None:
            cols.append(jnp.zeros(w.shape[:-1] + (width,), w.dtype))
        else:
            c = w[..., start:start + width]
            cols.append(c if scale == 1.0 else c * scale)
    return jnp.concatenate(cols, axis=-1)


def _head_padded(start, scale=1.0):
    pieces = []
    for h in range(N_HEADS):
        col = (start + h * HEAD_DIM, HEAD_DIM, scale)
        pad = (None, HEAD_DIM, 1.0)
        pieces += [col, pad] if h % 2 == 0 else [pad, col]
    return pieces


def _in_proj_layouts():
    bf = (_head_padded(O_SAQ, HEAD_DIM ** -0.5)
          + [(O_SAK, 512, 1.0), (O_SAV, 512, 1.0), (O_IXQ, 256, 1.0)]
          + [(O_DSV, 64, 1.0)] * 2
          + [(O_IXK, 32, 1.0)] * 4
          + [(O_Z, 1536, 1.0), (O_G, 3072, 1.0)])
    f32 = (_head_padded(O_DSQ)
           + [(O_CQ, 256, 1.0), (O_CKV, 128, 1.0)]
           + [(None, 64, 1.0), (O_KPE, 32, 1.0), (None, 32, 1.0)]
           + [(None, 64, 1.0), (O_KPE + 16, 16, 1.0), (O_KPE, 16, 1.0), (None, 32, 1.0)]
           + [(O_DSK, 64, 1.0)] * 2
           + [(O_IXW, 8, 1.0), (None, 120, 1.0)]
           + [(O_DSV, 64, 1.0), (None, 64, 1.0)])
    return bf, f32


B_SAQ, B_SAK, B_SAV, B_IXQ, B_DSV, B_IXK, B_Z, B_G, N_BF = 0, 1024, 1536, 2048, 2304, 2432, 2560, 4096, 7168
F_DSQ, F_CQ, F_CKV, F_KPE, F_KPER, F_DSK, F_IXW, F_DSV, N_F32 = 0, 1024, 1280, 1408, 1536, 1664, 1792, 1920, 2048


def _in_proj_kernel(x_ref, g_ref, w_ref, o_ref, h_ref):
    @pl.when(pl.program_id(1) == 0)
    def _():
        x = x_ref[...]
        ms = jnp.mean(x * x, axis=-1, keepdims=True)
        h_ref[...] = (x * lax.rsqrt(ms + EPS) * g_ref[...]).astype(BF16)

    o_ref[...] = jnp.dot(h_ref[...], w_ref[...], preferred_element_type=F32).astype(o_ref.dtype)


def _in_proj(x2, g, w, col0, n, out_dtype, tm, tn):
    m, d = x2.shape
    return pl.pallas_call(
        _in_proj_kernel,
        grid=(m // tm, n // tn),
        in_specs=[pl.BlockSpec((tm, d), lambda i, j: (i, 0)),
                  pl.BlockSpec((1, d), lambda i, j: (0, 0)),
                  pl.BlockSpec((d, tn), lambda i, j: (0, col0 // tn + j))],
        out_specs=pl.BlockSpec((tm, tn), lambda i, j: (i, j)),
        out_shape=jax.ShapeDtypeStruct((m, n), out_dtype),
        scratch_shapes=[pltpu.VMEM((tm, d), BF16)],
        compiler_params=pltpu.CompilerParams(dimension_semantics=("parallel", "arbitrary"),
                                             vmem_limit_bytes=VMEM_LIMIT),
        name="in_proj",
    )(x2, g, w)


def _rms(x, width):
    return lax.rsqrt(jnp.sum(x * x, axis=-1, keepdims=True) * (1.0 / width) + EPS)


def _prologue_kernel(x_ref, ng_ref, wf_ref, cos_ref, sin_ref,
                     wuq_ref, wuqr_ref, wuk_ref, wuv_ref,
                     qng_ref, kvng_ref, gq_ref, gqr_ref, gk_ref, gkr_ref, gdq_ref, gdk_ref,
                     qb_ref, kb_ref, vbt_ref, dq_ref, dkk_ref, ixw_ref, vvt_ref):
    x = x_ref[...]
    hidden = (x * lax.rsqrt(jnp.mean(x * x, axis=-1, keepdims=True) + EPS) * ng_ref[...]).astype(BF16)
    pf = jnp.dot(hidden, wf_ref[...], preferred_element_type=F32)
    ixw_ref[...] = pf[:, F_IXW:F_IXW + LANES]
    cos = cos_ref[...]
    sin = sin_ref[...]

    cq = pf[:, F_CQ:F_CQ + MLA_Q_LORA]
    cqn = (cq * _rms(cq, MLA_Q_LORA) * qng_ref[...]).astype(BF16)
    q = jnp.dot(cqn, wuq_ref[...], preferred_element_type=F32)
    qr = jnp.dot(cqn, wuqr_ref[...], preferred_element_type=F32)
    cg = cos * gq_ref[...]
    sg = sin * gqr_ref[...]
    for h in range(N_HEADS):
        sl = slice(h * LANES, (h + 1) * LANES)
        qh = q[:, sl]
        qb_ref[:, sl] = ((qh * cg + qr[:, sl] * sg) * _rms(qh, MLA_QK)).astype(BF16)

    ckv = pf[:, F_CKV:F_CKV + MLA_KV_LORA]
    ckvn = (ckv * _rms(ckv, MLA_KV_LORA) * kvng_ref[...]).astype(BF16)
    kn = jnp.dot(ckvn, wuk_ref[...], preferred_element_type=F32)
    tm = x.shape[0]
    ones_rows = (lax.broadcasted_iota(I32, (8, tm), 0) == 0).astype(BF16)
    v = jnp.dot(ckvn, wuv_ref[...], preferred_element_type=F32)
    for p in range(N_HEADS // 2):
        vbt_ref[0, p, 0:LANES, :] = jnp.transpose(v[:, p * LANES:(p + 1) * LANES]).astype(BF16)
        vbt_ref[0, p, LANES:LANES + 8, :] = ones_rows
    vvt_ref[0, 0:HEAD_DIM, :] = jnp.transpose(pf[:, F_DSV:F_DSV + LANES])[0:HEAD_DIM].astype(BF16)
    vvt_ref[0, HEAD_DIM:V_ROWS, :] = ones_rows
    kpe = pf[:, F_KPE:F_KPE + LANES]
    cgk = cos * gk_ref[...]
    rot = pf[:, F_KPER:F_KPER + LANES] * (sin * gkr_ref[...])
    for h in range(N_HEADS):
        sl = slice(h * LANES, (h + 1) * LANES)
        kf = kn[:, sl] + kpe
        kb_ref[:, sl] = ((kf * cgk + rot) * _rms(kf, MLA_QK)).astype(BF16)

    for h in range(N_HEADS):
        sl = slice(h * LANES, (h + 1) * LANES)
        xq = pf[:, F_DSQ + h * LANES:F_DSQ + (h + 1) * LANES]
        dq_ref[:, sl] = (xq * _rms(xq, HEAD_DIM) * gdq_ref[:, sl]).astype(BF16)
    xk = pf[:, F_DSK:F_DSK + LANES]
    dkk_ref[...] = (xk * _rms(xk, 2 * HEAD_DIM) * gdk_ref[...]).astype(BF16)


def _prologue(x2, cos_t, sin_t, lw, tm):
    m, d = x2.shape

    def whole(a):
        return pl.BlockSpec(a.shape, lambda i: (0,) * a.ndim)

    weights = [lw[k] for k in ("wuq", "wuqr", "wuk", "wuv", "qng", "kvng", "gq", "gqr", "gk", "gkr", "gdq", "gdk")]
    row = lambda width: pl.BlockSpec((tm, width), lambda i: (i, 0))
    return pl.pallas_call(
        _prologue_kernel,
        grid=(m // tm,),
        in_specs=[row(d), whole(lw["norm_g"]), pl.BlockSpec((d, N_F32), lambda i: (0, 0)), row(LANES), row(LANES)]
        + [whole(w) for w in weights],
        out_specs=[row(1024), row(1024),
                   pl.BlockSpec((1, N_HEADS // 2, LANES + 8, tm), lambda i: (i, 0, 0, 0)),
                   row(1024), row(LANES), row(LANES),
                   pl.BlockSpec((1, V_ROWS, tm), lambda i: (i, 0, 0))],
        out_shape=[jax.ShapeDtypeStruct((m, 1024), BF16), jax.ShapeDtypeStruct((m, 1024), BF16),
                   jax.ShapeDtypeStruct((m // tm, N_HEADS // 2, LANES + 8, tm), BF16),
                   jax.ShapeDtypeStruct((m, 1024), BF16),
                   jax.ShapeDtypeStruct((m, LANES), BF16), jax.ShapeDtypeStruct((m, LANES), F32),
                   jax.ShapeDtypeStruct((m // tm, V_ROWS, tm), BF16)],
        compiler_params=pltpu.CompilerParams(dimension_semantics=("parallel",), vmem_limit_bytes=VMEM_LIMIT),
        name="prologue",
    )(x2, lw["norm_g"], lw["w_all"], cos_t, sin_t, *weights)


def _sublane_all(x, op):
    for shift in (4, 2, 1):
        x = op(x, pltpu.roll(x, shift, 0))
    return x


def _sb_kernel(q_ref, k_ref, v_ref, o_ref, acc_ref, carry_ref, hl_ref, lb_ref, *, tq):
    t = LANES
    qi = pl.program_id(1)
    jj = lax.broadcasted_iota(I32, (t, 2 * t), 0)
    ss = lax.broadcasted_iota(I32, (t, 2 * t), 1)
    tri_ones = jnp.logical_or(ss >= t, jj > ss).astype(BF16)
    q_idx = qi * tq + lax.broadcasted_iota(I32, (tq, t), 0)
    k_lane = lax.broadcasted_iota(I32, (tq, t), 1)

    acc_ref[...] = jnp.zeros(acc_ref.shape, F32)
    carry_ref[...] = jnp.zeros(carry_ref.shape, F32)

    def cond(c):
        kb, cmax = c
        return jnp.logical_and(kb >= 0, cmax > -SB_UNDERFLOW)

    def body(c):
        kb, _ = c
        start = pl.multiple_of(kb * t, t)
        mask = (start + k_lane) < q_idx
        for h in range(N_HEADS):
            q = q_ref[0, :, h * LANES:(h + 1) * LANES]
            k = k_ref[0, pl.ds(start, t), (h // 2) * LANES:(h // 2 + 1) * LANES]
            z = lax.dot_general(q, k, NT_DIMS, preferred_element_type=F32)
            sp = jnp.maximum(z, 0.0) + jnp.log(1.0 + jnp.exp(-jnp.abs(z)))
            l1m = jnp.where(mask, -sp, 0.0)
            hi = l1m.astype(BF16)
            hl_ref[2 * h * tq:(2 * h + 1) * tq, :] = hi
            hl_ref[(2 * h + 1) * tq:(2 * h + 2) * tq, :] = (l1m - hi.astype(F32)).astype(BF16)
            lb_ref[h * tq:(h + 1) * tq, :] = z - sp
        sums = jnp.dot(hl_ref[...], tri_ones, preferred_element_type=F32)
        for h in range(N_HEADS):
            rows = slice(h * tq, (h + 1) * tq)
            both = sums[2 * h * tq:(2 * h + 1) * tq, :] + sums[(2 * h + 1) * tq:(2 * h + 2) * tq, :]
            carry = carry_ref[rows, :]
            a = jnp.where(mask, jnp.exp(lb_ref[rows, :] + carry + both[:, :t]), 0.0)
            v = v_ref[0, pl.ds(start, t), (h // 2) * LANES:(h // 2 + 1) * LANES]
            acc_ref[rows, :] += jnp.dot(a.astype(BF16), v, preferred_element_type=F32)
            carry_ref[rows, :] = carry + both[:, t:]
        return kb - 1, jnp.max(carry_ref[...])

    lax.while_loop(cond, body, (((qi + 1) * tq) // t - 1, jnp.float32(0.0)))

    lane = lax.broadcasted_iota(I32, (tq, LANES), 1)
    for p in range(N_HEADS // 2):
        even = acc_ref[2 * p * tq:(2 * p + 1) * tq, :]
        odd = acc_ref[(2 * p + 1) * tq:(2 * p + 2) * tq, :]
        o_ref[0, :, p * LANES:(p + 1) * LANES] = jnp.where(lane < HEAD_DIM, even, odd)


def _sb_attention(pb, b, s, tq):
    pb3 = pb.reshape(b, s, N_BF)
    kernel = functools.partial(_sb_kernel, tq=tq)
    return pl.pallas_call(
        kernel,
        grid=(b, s // tq),
        in_specs=[pl.BlockSpec((1, tq, N_HEADS * LANES), lambda bi, i: (bi, i, B_SAQ // (N_HEADS * LANES))),
                  pl.BlockSpec((1, s, BRANCH_WIDTH), lambda bi, i: (bi, 0, B_SAK // BRANCH_WIDTH)),
                  pl.BlockSpec((1, s, BRANCH_WIDTH), lambda bi, i: (bi, 0, B_SAV // BRANCH_WIDTH))],
        out_specs=pl.BlockSpec((1, tq, BRANCH_WIDTH), lambda bi, i: (bi, i, 0)),
        out_shape=jax.ShapeDtypeStruct((b, s, BRANCH_WIDTH), F32),
        scratch_shapes=[pltpu.VMEM((N_HEADS * tq, LANES), F32),
                        pltpu.VMEM((N_HEADS * tq, LANES), F32),
                        pltpu.VMEM((2 * N_HEADS * tq, LANES), BF16),
                        pltpu.VMEM((N_HEADS * tq, LANES), F32)],
        compiler_params=pltpu.CompilerParams(dimension_semantics=("parallel", "arbitrary"),
                                             vmem_limit_bytes=VMEM_LIMIT),
        name="sb_attention",
    )(pb3, pb3, pb3)


def _mla_kernel(q_ref, k_ref, vt_ref, o_ref, qbd_ref, p_ref, m_ref, alpha_ref, acc_ref, *, tq, tc):
    t = LANES
    assert tq == t
    groups = tc // 8
    pair_w = 2 * tq
    qi = pl.program_id(1)
    n_chunks = (qi * tq) // tc + 1
    k_in_chunk = (lax.broadcasted_iota(I32, (groups, 8, tq), 0) * 8
                  + lax.broadcasted_iota(I32, (groups, 8, tq), 1))
    q_lane = qi * tq + lax.broadcasted_iota(I32, (groups, 8, tq), 2)

    zeros = jnp.zeros((tq, t), BF16)
    for p in range(N_HEADS // 2):
        q_even = q_ref[0, :, 2 * p * LANES:(2 * p + 1) * LANES]
        q_odd = q_ref[0, :, (2 * p + 1) * LANES:(2 * p + 2) * LANES]
        qbd_ref[p, 0:tq, :] = jnp.concatenate([q_even, zeros], axis=1)
        qbd_ref[p, tq:2 * tq, :] = jnp.concatenate([zeros, q_odd], axis=1)

    v_rows = acc_ref.shape[0]
    m_ref[...] = jnp.full(m_ref.shape, NEG_BIG, F32)
    acc_ref[...] = jnp.zeros(acc_ref.shape, F32)

    def chunk(c, diagonal):
        start = pl.multiple_of(c * tc, tc)
        pair_cols = [slice(p * pair_w, (p + 1) * pair_w) for p in range(N_HEADS // 2)]
        s_pairs = [lax.dot_general(k_ref[0, pl.ds(start, tc), pair_cols[p]], qbd_ref[p], NT_DIMS,
                                   preferred_element_type=F32) for p in range(N_HEADS // 2)]
        for p in range(N_HEADS // 2):
            pcols = pair_cols[p]
            for i in range(2):
                cols = slice(p * pair_w + i * tq, p * pair_w + (i + 1) * tq)
                sb = s_pairs[p][:, i * tq:(i + 1) * tq].reshape(groups, 8, tq)
                if diagonal:
                    sb = jnp.where(start + k_in_chunk <= q_lane, sb, NEG_BIG)
                m_old = m_ref[:, cols]
                m_new = jnp.maximum(m_old, _sublane_all(jnp.max(sb, axis=0), jnp.maximum))
                prob = jnp.exp(sb - m_new[None])
                alpha = jnp.exp(m_old - m_new)
                p_ref[:, cols] = prob.reshape(tc, tq).astype(BF16)
                alpha_ref[:, cols] = alpha
                m_ref[:, cols] = m_new
            pv = jnp.dot(vt_ref[0, c, p], p_ref[:, pcols], preferred_element_type=F32)
            scaled = acc_ref[:, pcols].reshape(v_rows // 8, 8, pair_w) * alpha_ref[:, pcols][None]
            acc_ref[:, pcols] = scaled.reshape(v_rows, pair_w) + pv

    def body(c, carry):
        chunk(c, False)
        return carry

    lax.fori_loop(0, n_chunks - 1, body, 0)
    chunk(n_chunks - 1, True)

    for p in range(N_HEADS // 2):
        outs = []
        for i in range(2):
            cols = slice(p * pair_w + i * tq, p * pair_w + (i + 1) * tq)
            outs.append(acc_ref[i * HEAD_DIM:(i + 1) * HEAD_DIM, cols] / acc_ref[t:t + 1, cols])
        o_ref[0, :, p * LANES:(p + 1) * LANES] = jnp.transpose(jnp.concatenate(outs, axis=0))


def _mla_attention(qb, kb, vbt, b, s, tq, tc):
    width = N_HEADS * tq
    kernel = functools.partial(_mla_kernel, tq=tq, tc=tc)
    return pl.pallas_call(
        kernel,
        grid=(b, s // tq),
        in_specs=[pl.BlockSpec((1, tq, N_HEADS * LANES), lambda bi, i: (bi, i, 0)),
                  pl.BlockSpec((1, s, N_HEADS * LANES), lambda bi, i: (bi, 0, 0)),
                  pl.BlockSpec((1, s // tc, N_HEADS // 2, LANES + 8, tc), lambda bi, i: (bi, 0, 0, 0, 0))],
        out_specs=pl.BlockSpec((1, tq, BRANCH_WIDTH), lambda bi, i: (bi, i, 0)),
        out_shape=jax.ShapeDtypeStruct((b, s, BRANCH_WIDTH), F32),
        scratch_shapes=[pltpu.VMEM((N_HEADS // 2, 2 * tq, 2 * LANES), BF16),
                        pltpu.VMEM((tc, width), BF16),
                        pltpu.VMEM((8, width), F32),
                        pltpu.VMEM((8, width), F32),
                        pltpu.VMEM((LANES + 8, width), F32)],
        compiler_params=pltpu.CompilerParams(dimension_semantics=("parallel", "arbitrary"),
                                             vmem_limit_bytes=VMEM_LIMIT),
        name="mla_attention",
    )(qb.reshape(b, s, -1), kb.reshape(b, s, -1), vbt)


def _sortable(x):
    bits = pltpu.bitcast(x + 0.0, I32)
    return bits ^ ((bits >> 31) & 0x7FFFFFFF)


def _dsa_kernel(rel_ref, tab_ref, ixq_ref, ixw_ref, ixk_ref, q_ref, kk_ref, vvt_ref, pos_ref, o_ref,
                keys_ref, qm_ref, qs_ref, wt_ref, p_ref, selb_ref, add_ref,
                m_ref, alpha_ref, acc_ref, thr_ref, *, tq, tc, topk, seq):
    t = LANES
    assert tq == t
    groups = tc // 8
    width = N_HEADS * tq
    qi = pl.program_id(1)
    n_chunks = (qi * tq) // tc + 1
    head_cols = [slice(h * tq, (h + 1) * tq) for h in range(N_HEADS)]
    lane = lax.broadcasted_iota(I32, (tq, t), 1)
    q_lane = qi * tq + lax.broadcasted_iota(I32, (groups, 8, tq), 2)
    k_in_chunk = (lax.broadcasted_iota(I32, (groups, 8, tq), 0) * 8
                  + lax.broadcasted_iota(I32, (groups, 8, tq), 1))

    for h in range(N_HEADS):
        grp = ixq_ref[0, :, (h // 4) * LANES:(h // 4 + 1) * LANES]
        lo = (h % 4) * IDX_DIM
        qm_ref[h * tq:(h + 1) * tq, :] = jnp.where((lane >= lo) & (lane < lo + IDX_DIM), grp, jnp.zeros_like(grp))
        qs_ref[h * tq:(h + 1) * tq, :] = q_ref[0, :, h * LANES:(h + 1) * LANES]
    wt_ref[...] = jnp.transpose(ixw_ref[0] * ((IDX_DIM ** -0.5) * (N_HEADS ** -0.5)))

    def chunk3(ref, c):
        start = pl.multiple_of(c * tc, tc)
        return ref[pl.ds(start, tc), :].reshape(groups, 8, tq)

    def score_chunk(c, carry):
        start = pl.multiple_of(c * tc, tc)
        st = lax.dot_general(ixk_ref[0, pl.ds(start, tc), :], qm_ref[...], NT_DIMS, preferred_element_type=F32)
        sc = None
        for h in range(N_HEADS):
            term = jnp.maximum(st[:, head_cols[h]], 0.0) * wt_ref[h:h + 1, :]
            sc = term if sc is None else sc + term
        key = jnp.where(start + k_in_chunk <= q_lane, _sortable(sc).reshape(groups, 8, tq), INT_MIN)
        keys_ref[pl.ds(start, tc), :] = key.reshape(tc, tq)
        return carry

    lax.fori_loop(0, n_chunks, score_chunk, 0)

    def count(pred):
        def one(c, cnt):
            return cnt + jnp.sum(pred(chunk3(keys_ref, c), c * tc).astype(I32), axis=0)

        def two(i, cnt):
            return one(2 * i + 1, one(2 * i, cnt))

        cnt = lax.fori_loop(0, n_chunks // 2, two, jnp.zeros((8, tq), I32))
        cnt = lax.fori_loop(2 * (n_chunks // 2), n_chunks, one, cnt)
        return _sublane_all(cnt, jnp.add)

    @pl.when((qi + 1) * tq <= topk)
    def _():
        thr_ref[0] = jnp.full((8, tq), INT_MIN, I32)
        thr_ref[1] = jnp.full((8, tq), -1, I32)

    @pl.when((qi + 1) * tq > topk)
    def _():
        ans = jnp.zeros((8, tq), I32)
        n_ge = jnp.full((8, tq), seq, I32)
        for bit in range(31, -1, -1):
            cand = ans | np.int32(INT_MIN if bit == 31 else 1 << bit)
            cand_s = cand ^ np.int32(INT_MIN)
            c = count(lambda keys, start, cand_s=cand_s: keys >= cand_s[None])
            take = c >= topk
            ans = jnp.where(take, cand, ans)
            n_ge = jnp.where(take, c, n_ge)
        tau = ans ^ np.int32(INT_MIN)
        thr_ref[0] = tau
        thr_ref[1] = jnp.full((8, tq), seq, I32)

        @pl.when(jnp.max(n_ge) > topk)
        def _():
            need = topk - count(lambda keys, start: keys > tau[None])
            cut = jnp.zeros((8, tq), I32)
            for bit in range(int(math.log2(seq)), -1, -1):
                cand = cut | np.int32(1 << bit)
                c = count(lambda keys, start, cand=cand: (keys == tau[None]) & (start + k_in_chunk < cand[None]))
                cut = jnp.where(c < need, cand, cut)
            thr_ref[1] = cut

    tau = thr_ref[0]
    cut = thr_ref[1]

    v_rows = acc_ref.shape[0]
    m_ref[...] = jnp.full(m_ref.shape, NEG_BIG, F32)
    acc_ref[...] = jnp.zeros(acc_ref.shape, F32)
    pq = pos_ref[0, qi]
    pq_min = jnp.min(pq)
    last_bias = [rel_ref[REL_BUCKETS - 1, h] for h in range(N_HEADS)]

    def softmax_head(h, s_t, addend, bias):
        cols = head_cols[h]
        sb = s_t.reshape(groups, 8, tq) + addend
        m_old = m_ref[:, cols]
        m_new = jnp.maximum(m_old, _sublane_all(jnp.max(sb, axis=0), jnp.maximum) + bias)
        p = jnp.exp(sb - (m_new - bias)[None])
        alpha = jnp.exp(m_old - m_new)
        p_ref[:, cols] = p.reshape(tc, tq).astype(BF16)
        alpha_ref[:, cols] = alpha
        m_ref[:, cols] = m_new

    def accumulate(c, half):
        cols = slice(half * (width // 2), (half + 1) * (width // 2))
        pv = jnp.dot(vvt_ref[0, c], p_ref[:, cols], preferred_element_type=F32)
        scaled = acc_ref[:, cols].reshape(v_rows // 8, 8, width // 2) * alpha_ref[:, cols][None]
        acc_ref[:, cols] = scaled.reshape(v_rows, width // 2) + pv

    def all_heads(c, addend_of, bias_of):
        start = pl.multiple_of(c * tc, tc)
        s_all = lax.dot_general(kk_ref[0, pl.ds(start, tc), :], qs_ref[...], NT_DIMS, preferred_element_type=F32)
        for half in range(2):
            for h in range(half * N_HEADS // 2, (half + 1) * N_HEADS // 2):
                softmax_head(h, s_all[:, head_cols[h]], addend_of(h), bias_of(h))
            accumulate(c, half)

    def attend_chunk(c, carry):
        keys = chunk3(keys_ref, c)
        sel = (keys > tau[None]) | ((keys == tau[None]) & (k_in_chunk <= (cut - c * tc)[None]))
        selb_ref[...] = jnp.where(sel, 0.0, NEG_BIG).reshape(tc, tq)
        first_tile = c * (tc // t)
        pk_max = pos_ref[0, first_tile]
        for j in range(1, tc // t):
            pk_max = jnp.maximum(pk_max, pos_ref[0, first_tile + j])
        chunk_far = pq_min - jnp.max(pk_max) >= FAR_DIST

        @pl.when(chunk_far)
        def _():
            all_heads(c, lambda h: selb_ref[...].reshape(groups, 8, tq), lambda h: last_bias[h])

        @pl.when(jnp.logical_not(chunk_far))
        def _():
            for j in range(tc // t):
                rows = slice(j * t, (j + 1) * t)
                pk = pos_ref[0, first_tile + j]
                tile_far = jnp.logical_or(pq_min - jnp.max(pk) >= FAR_DIST, c * tc + j * t >= (qi + 1) * tq)

                @pl.when(tile_far)
                def _():
                    for h in range(N_HEADS):
                        add_ref[rows, head_cols[h]] = selb_ref[rows, :] + last_bias[h]

                @pl.when(jnp.logical_not(tile_far))
                def _():
                    pk_col = jnp.transpose(jnp.broadcast_to(pk, (t, t)))
                    dist = jnp.clip(pq - pk_col, 0, t - 1)
                    for h in range(N_HEADS):
                        table = jnp.broadcast_to(tab_ref[h:h + 1, :], (t, t))
                        bias = jnp.take_along_axis(table, dist, axis=1, mode="promise_in_bounds")
                        add_ref[rows, head_cols[h]] = selb_ref[rows, :] + bias

            all_heads(c, lambda h: add_ref[:, head_cols[h]].reshape(groups, 8, tq), lambda h: 0.0)

        return carry

    lax.fori_loop(0, n_chunks, attend_chunk, 0)

    for p in range(N_HEADS // 2):
        outs = []
        for h in (2 * p, 2 * p + 1):
            cols = head_cols[h]
            outs.append(acc_ref[0:HEAD_DIM, cols] / acc_ref[HEAD_DIM:HEAD_DIM + 1, cols])
        o_ref[0, :, p * LANES:(p + 1) * LANES] = jnp.transpose(jnp.concatenate(outs, axis=0))


def _dsa_attention(rel_bias, dist_table, pb, ixw, dq, dkk, vvt, pos_rows, b, s, tq, tc):
    topk = min(TOPK_MAX, s // 4)
    pb3 = pb.reshape(b, s, N_BF)
    width = N_HEADS * tq
    kernel = functools.partial(_dsa_kernel, tq=tq, tc=tc, topk=topk, seq=s)
    return pl.pallas_call(
        kernel,
        grid=(b, s // tq),
        in_specs=[pl.BlockSpec(memory_space=pltpu.SMEM),
                  pl.BlockSpec(dist_table.shape, lambda bi, i: (0, 0)),
                  pl.BlockSpec((1, tq, 2 * LANES), lambda bi, i: (bi, i, B_IXQ // 256)),
                  pl.BlockSpec((1, tq, LANES), lambda bi, i: (bi, i, 0)),
                  pl.BlockSpec((1, s, LANES), lambda bi, i: (bi, 0, B_IXK // LANES)),
                  pl.BlockSpec((1, tq, N_HEADS * LANES), lambda bi, i: (bi, i, 0)),
                  pl.BlockSpec((1, s, LANES), lambda bi, i: (bi, 0, 0)),
                  pl.BlockSpec((1, s // tc, V_ROWS, tc), lambda bi, i: (bi, 0, 0, 0)),
                  pl.BlockSpec((1, s // LANES, 1, LANES), lambda bi, i: (bi, 0, 0, 0))],
        out_specs=pl.BlockSpec((1, tq, BRANCH_WIDTH), lambda bi, i: (bi, i, 0)),
        out_shape=jax.ShapeDtypeStruct((b, s, BRANCH_WIDTH), F32),
        scratch_shapes=[pltpu.VMEM((s, tq), I32),
                        pltpu.VMEM((width, LANES), BF16),
                        pltpu.VMEM((width, LANES), BF16),
                        pltpu.VMEM((LANES, tq), F32),
                        pltpu.VMEM((tc, width), BF16),
                        pltpu.VMEM((tc, tq), F32),
                        pltpu.VMEM((tc, width), F32),
                        pltpu.VMEM((8, width), F32),
                        pltpu.VMEM((8, width), F32),
                        pltpu.VMEM((V_ROWS, width), F32),
                        pltpu.VMEM((2, 8, tq), I32)],
        compiler_params=pltpu.CompilerParams(dimension_semantics=("parallel", "arbitrary"),
                                             vmem_limit_bytes=VMEM_LIMIT),
        name="dsa_attention",
    )(rel_bias, dist_table, pb3, ixw.reshape(b, s, LANES), pb3, dq.reshape(b, s, -1), dkk.reshape(b, s, -1), vvt, pos_rows)


def _merge_kernel(x_ref, ya_ref, yb_ref, yc_ref, za_ref, zb_ref, zc_ref, ga_ref, gb_ref, gc_ref,
                  bias_ref, wbr_ref, wout_ref, o_ref):
    merged = None
    branches = ((ya_ref, za_ref, ga_ref), (yb_ref, zb_ref, gb_ref), (yc_ref, zc_ref, gc_ref))
    for n, (y_ref, z_ref, g_ref) in enumerate(branches):
        z = z_ref[...].astype(F32)
        branch = (y_ref[...] * (z * jax.nn.sigmoid(z))).astype(BF16)
        up = jnp.dot(branch, wbr_ref[n], preferred_element_type=F32)
        gate = jax.nn.sigmoid(g_ref[...].astype(F32) + bias_ref[n:n + 1, :])
        merged = gate * up if merged is None else merged + gate * up
    o_ref[...] = x_ref[...] + jnp.dot(merged.astype(BF16), wout_ref[...], preferred_element_type=F32)


def _merge(x2, ya, yb, yc, pb, gate_b, wbr, wout, tm):
    m, d = x2.shape
    row = lambda width: pl.BlockSpec((tm, width), lambda i: (i, 0))
    col = lambda width, off: pl.BlockSpec((tm, width), lambda i, _b=off // width: (i, _b))
    z_specs = [col(BRANCH_WIDTH, B_Z + n * BRANCH_WIDTH) for n in range(N_BRANCH)]
    g_specs = [col(d, B_G + n * d) for n in range(N_BRANCH)]
    return pl.pallas_call(
        _merge_kernel,
        grid=(m // tm,),
        in_specs=[row(d), row(BRANCH_WIDTH), row(BRANCH_WIDTH), row(BRANCH_WIDTH)] + z_specs + g_specs + [
            pl.BlockSpec(gate_b.shape, lambda i: (0, 0)),
            pl.BlockSpec(wbr.shape, lambda i: (0, 0, 0)),
            pl.BlockSpec(wout.shape, lambda i: (0, 0))],
        out_specs=row(d),
        out_shape=jax.ShapeDtypeStruct((m, d), F32),
        compiler_params=pltpu.CompilerParams(dimension_semantics=("parallel",), vmem_limit_bytes=VMEM_LIMIT),
        name="merge",
    )(x2, ya, yb, yc, pb, pb, pb, pb, pb, pb, gate_b, wbr, wout)


def _lane_table(vals_by_lane):
    t = jnp.zeros((LANES,), F32)
    for start, v in vals_by_lane.items():
        t = t.at[start:start + v.shape[0]].set(v)
    return t[None, :]


def _layer_weights(l, w_all, norm_g, mla_q_norm_g, mla_kv_norm_g, mla_w_uq, mla_w_ukv, mla_q_g, mla_k_g,
                   dsa_q_g, dsa_k_g, gate_b, w_branch, w_out):
    uq, ukv = mla_w_uq[l], mla_w_ukv[l]
    half = MLA_ROPE // 2
    q_pieces, qr_pieces, k_pieces, v_pieces = [], [], [], []
    for h in range(N_HEADS):
        o = h * MLA_QK
        q_pieces += [(o, MLA_QK, 1.0), (None, LANES - MLA_QK, 1.0)]
        qr_pieces += [(None, MLA_NOPE, 1.0), (o + MLA_NOPE + half, half, 1.0), (o + MLA_NOPE, half, 1.0),
                      (None, LANES - MLA_QK, 1.0)]
        k_pieces += [(h * 128, MLA_NOPE, 1.0), (None, LANES - MLA_NOPE, 1.0)]
        v_pieces += [(h * 128 + MLA_NOPE, 64, 1.0)]
    scale = MLA_QK ** -0.5
    qg, kg = mla_q_g[l], mla_k_g[l]
    rot_gain = lambda g: jnp.concatenate([g[MLA_NOPE + half:], g[MLA_NOPE:MLA_NOPE + half]])
    dq_g = jnp.concatenate([jnp.concatenate([dsa_q_g[l], jnp.zeros((HEAD_DIM,), F32)]) if h % 2 == 0 else
                            jnp.concatenate([jnp.zeros((HEAD_DIM,), F32), dsa_q_g[l]]) for h in range(N_HEADS)])
    return dict(
        norm_g=norm_g[l][None, :], w_all=w_all[l],
        wuq=_assemble(uq, q_pieces).astype(BF16), wuqr=_assemble(uq, qr_pieces).astype(BF16),
        wuk=_assemble(ukv, k_pieces).astype(BF16), wuv=_assemble(ukv, v_pieces).astype(BF16),
        qng=mla_q_norm_g[l][None, :], kvng=mla_kv_norm_g[l][None, :],
        gq=_lane_table({0: qg * scale}), gqr=_lane_table({MLA_NOPE: rot_gain(qg) * scale}),
        gk=_lane_table({0: kg}), gkr=_lane_table({MLA_NOPE: rot_gain(kg)}),
        gdq=(dq_g * HEAD_DIM ** -0.5)[None, :], gdk=jnp.concatenate([dsa_k_g[l], dsa_k_g[l]])[None, :],
        gate_b=gate_b[l], wbr=w_branch[l].astype(BF16), wout=w_out[l].astype(BF16))


def kernel(x, positions, norm_g, w_in, mla_q_norm_g, mla_kv_norm_g, mla_w_uq, mla_w_ukv, mla_q_g, mla_k_g,
           dsa_q_g, dsa_k_g, rel_bias, gate_b, w_branch, w_out):
    b, s, d = x.shape
    depth = w_in.shape[0]
    m = b * s
    assert d == D_MODEL and w_in.shape[-1] == D_IN and s % 512 == 0 and s & (s - 1) == 0

    half = MLA_ROPE // 2
    inv_freq = ROPE_THETA ** (-(jnp.arange(half, dtype=F32) * 2.0) / MLA_ROPE)
    ang = positions.astype(F32)[..., None] * inv_freq
    cos, sin = jnp.cos(ang).reshape(m, half), jnp.sin(ang).reshape(m, half)
    cos_t = jnp.concatenate([jnp.ones((m, MLA_NOPE), F32), cos, cos, jnp.ones((m, LANES - MLA_QK), F32)], axis=1)
    sin_t = jnp.concatenate([jnp.zeros((m, MLA_NOPE), F32), -sin, sin, jnp.zeros((m, LANES - MLA_QK), F32)], axis=1)

    bf_pieces, f32_pieces = _in_proj_layouts()
    w_in16 = w_in.astype(BF16)
    w_all = _assemble(w_in16, f32_pieces + bf_pieces)

    tm = min(1024, m)
    tc = 512
    tp = min(512, m)
    pos_rows = positions.reshape(b, s // LANES, 1, LANES)
    dist_table = rel_bias[BUCKET_OF_DIST, :].T
    x2 = x.reshape(m, d)
    for l in range(depth):
        lw = _layer_weights(l, w_all, norm_g, mla_q_norm_g, mla_kv_norm_g, mla_w_uq, mla_w_ukv,
                            mla_q_g, mla_k_g, dsa_q_g, dsa_k_g, gate_b, w_branch, w_out)
        pb = _in_proj(x2, lw["norm_g"], lw["w_all"], N_F32, N_BF, BF16, min(2 * tm, m), 1024)
        qb, kb, vbt, dq, dkk, ixw, vvt = _prologue(x2, cos_t, sin_t, lw, tc)
        ya = _sb_attention(pb, b, s, 128)
        yb = _mla_attention(qb, kb, vbt.reshape(b, s // tc, N_HEADS // 2, LANES + 8, tc), b, s, LANES, tc)
        vvt = vvt.reshape(b, s // tc, V_ROWS, tc)
        yc = _dsa_attention(rel_bias, dist_table, pb, ixw, dq, dkk, vvt, pos_rows, b, s, LANES, tc)
        x2 = _merge(x2, ya.reshape(m, -1), yb.reshape(m, -1), yc.reshape(m, -1), pb, lw["gate_b"],
                    lw["wbr"], lw["wout"], tp)
    return x2.reshape(b, s, d)
```

```python
import functools
import math

import numpy as np
import jax
import jax.numpy as jnp
from jax import lax
from jax.experimental import pallas as pl
from jax.experimental.pallas import tpu as pltpu

F32 = jnp.float32
BF16 = jnp.bfloat16
I32 = jnp.int32

LANES = 128
EPS = 1e-6
D_MODEL = 1024
N_HEADS = 8
HEAD_DIM = 64
MLA_Q_LORA = 256
MLA_KV_LORA = 128
MLA_NOPE = 64
MLA_ROPE = 32
MLA_QK = MLA_NOPE + MLA_ROPE
ROPE_THETA = 10000.0
IDX_DIM = 32
TOPK_MAX = 256
REL_BUCKETS = 32
REL_MAX_DIST = 128
N_BRANCH = 3
BRANCH_WIDTH = 512

O_SAQ, O_SAK, O_SAV = 0, 512, 1024
O_CQ, O_CKV, O_KPE = 1536, 1792, 1920
O_DSQ, O_DSK, O_DSV = 1952, 2464, 2528
O_IXQ, O_IXK, O_IXW = 2592, 2848, 2880
O_Z, O_G = 2888, 4424
D_IN = 7496

NEG_BIG = -1e30
INT_MIN = -(2 ** 31)
SB_UNDERFLOW = 104.0
VMEM_LIMIT = 48 * 1024 * 1024

NT_DIMS = (((1,), (1,)), ((), ()))
V_ROWS = HEAD_DIM + 8


def _t5_buckets():
    exact = REL_BUCKETS // 2
    n = np.arange(0, 4 * REL_MAX_DIST)
    nf = np.maximum(n, 1).astype(np.float64)
    large = exact + (np.log(nf / exact) / math.log(REL_MAX_DIST / exact) * (REL_BUCKETS - exact)).astype(np.int64)
    bucket = np.where(n < exact, n, np.minimum(large, REL_BUCKETS - 1))
    assert np.all(np.diff(bucket) >= 0) and np.all(bucket[LANES - 1:] == REL_BUCKETS - 1)
    return bucket[:LANES]


BUCKET_OF_DIST = _t5_buckets()
FAR_DIST = int(np.argmax(BUCKET_OF_DIST == REL_BUCKETS - 1))


def _assemble(w, pieces):
    cols = []
    for start, width, scale in pieces:
        if start is None:
            cols.append(jnp.zeros(w.shape[:-1] + (width,), w.dtype))
        else:
            c = w[..., start:start + width]
            cols.append(c if scale == 1.0 else c * scale)
    return jnp.concatenate(cols, axis=-1)


def _head_padded(start, scale=1.0):
    pieces = []
    for h in range(N_HEADS):
        col = (start + h * HEAD_DIM, HEAD_DIM, scale)
        pad = (None, HEAD_DIM, 1.0)
        pieces += [col, pad] if h % 2 == 0 else [pad, col]
    return pieces


def _in_proj_layouts():
    bf = (_head_padded(O_SAQ, HEAD_DIM ** -0.5)
          + [(O_SAK, 512, 1.0), (O_SAV, 512, 1.0), (O_IXQ, 256, 1.0)]
          + [(O_DSV, 64, 1.0)] * 2
          + [(O_IXK, 32, 1.0)] * 4
          + [(O_Z, 1536, 1.0), (O_G, 3072, 1.0)])
    f32 = (_head_padded(O_DSQ)
           + [(O_CQ, 256, 1.0), (O_CKV, 128, 1.0)]
           + [(None, 64, 1.0), (O_KPE, 32, 1.0), (None, 32, 1.0)]
           + [(None, 64, 1.0), (O_KPE + 16, 16, 1.0), (O_KPE, 16, 1.0), (None, 32, 1.0)]
           + [(O_DSK, 64, 1.0)] * 2
           + [(O_IXW, 8, 1.0), (None, 120, 1.0)]
           + [(O_DSV, 64, 1.0), (None, 64, 1.0)])
    return bf, f32


B_SAQ, B_SAK, B_SAV, B_IXQ, B_DSV, B_IXK, B_Z, B_G, N_BF = 0, 1024, 1536, 2048, 2304, 2432, 2560, 4096, 7168
F_DSQ, F_CQ, F_CKV, F_KPE, F_KPER, F_DSK, F_IXW, F_DSV, N_F32 = 0, 1024, 1280, 1408, 1536, 1664, 1792, 1920, 2048


def _in_proj_kernel(x_ref, g_ref, w_ref, o_ref, h_ref):
    @pl.when(pl.program_id(1) == 0)
    def _():
        x = x_ref[...]
        ms = jnp.mean(x * x, axis=-1, keepdims=True)
        h_ref[...] = (x * lax.rsqrt(ms + EPS) * g_ref[...]).astype(BF16)

    o_ref[...] = jnp.dot(h_ref[...], w_ref[...], preferred_element_type=F32).astype(o_ref.dtype)


def _in_proj(x2, g, w, col0, n, out_dtype, tm, tn):
    m, d = x2.shape
    return pl.pallas_call(
        _in_proj_kernel,
        grid=(m // tm, n // tn),
        in_specs=[pl.BlockSpec((tm, d), lambda i, j: (i, 0)),
                  pl.BlockSpec((1, d), lambda i, j: (0, 0)),
                  pl.BlockSpec((d, tn), lambda i, j: (0, col0 // tn + j))],
        out_specs=pl.BlockSpec((tm, tn), lambda i, j: (i, j)),
        out_shape=jax.ShapeDtypeStruct((m, n), out_dtype),
        scratch_shapes=[pltpu.VMEM((tm, d), BF16)],
        compiler_params=pltpu.CompilerParams(dimension_semantics=("parallel", "arbitrary"),
                                             vmem_limit_bytes=VMEM_LIMIT),
        name="in_proj",
    )(x2, g, w)


def _rms(x, width):
    return lax.rsqrt(jnp.sum(x * x, axis=-1, keepdims=True) * (1.0 / width) + EPS)


def _prologue_kernel(x_ref, ng_ref, wf_ref, cos_ref, sin_ref,
                     wuq_ref, wuqr_ref, wuk_ref, wuv_ref,
                     qng_ref, kvng_ref, gq_ref, gqr_ref, gk_ref, gkr_ref, gdq_ref, gdk_ref,
                     qb_ref, kb_ref, vbt_ref, dq_ref, dkk_ref, ixw_ref, vvt_ref):
    x = x_ref[...]
    hidden = (x * lax.rsqrt(jnp.mean(x * x, axis=-1, keepdims=True) + EPS) * ng_ref[...]).astype(BF16)
    pf = jnp.dot(hidden, wf_ref[...], preferred_element_type=F32)
    ixw_ref[...] = pf[:, F_IXW:F_IXW + LANES]
    cos = cos_ref[...]
    sin = sin_ref[...]

    cq = pf[:, F_CQ:F_CQ + MLA_Q_LORA]
    cqn = (cq * _rms(cq, MLA_Q_LORA) * qng_ref[...]).astype(BF16)
    q = jnp.dot(cqn, wuq_ref[...], preferred_element_type=F32)
    qr = jnp.dot(cqn, wuqr_ref[...], preferred_element_type=F32)
    cg = cos * gq_ref[...]
    sg = sin * gqr_ref[...]
    for h in range(N_HEADS):
        sl = slice(h * LANES, (h + 1) * LANES)
        qh = q[:, sl]
        qb_ref[:, sl] = ((qh * cg + qr[:, sl] * sg) * _rms(qh, MLA_QK)).astype(BF16)

    ckv = pf[:, F_CKV:F_CKV + MLA_KV_LORA]
    ckvn = (ckv * _rms(ckv, MLA_KV_LORA) * kvng_ref[...]).astype(BF16)
    kn = jnp.dot(ckvn, wuk_ref[...], preferred_element_type=F32)
    tm = x.shape[0]
    ones_rows = (lax.broadcasted_iota(I32, (8, tm), 0) == 0).astype(BF16)
    v = jnp.dot(ckvn, wuv_ref[...], preferred_element_type=F32)
    for p in range(N_HEADS // 2):
        vbt_ref[0, p, 0:LANES, :] = jnp.transpose(v[:, p * LANES:(p + 1) * LANES]).astype(BF16)
        vbt_ref[0, p, LANES:LANES + 8, :] = ones_rows
    sub = vvt_ref.shape[3]
    for j in range(tm // sub):
        vals = pf[j * sub:(j + 1) * sub, F_DSV:F_DSV + LANES]
        vvt_ref[0, j, 0:HEAD_DIM, :] = jnp.transpose(vals)[0:HEAD_DIM].astype(BF16)
        vvt_ref[0, j, HEAD_DIM:V_ROWS, :] = (lax.broadcasted_iota(I32, (8, sub), 0) == 0).astype(BF16)
    kpe = pf[:, F_KPE:F_KPE + LANES]
    cgk = cos * gk_ref[...]
    rot = pf[:, F_KPER:F_KPER + LANES] * (sin * gkr_ref[...])
    for h in range(N_HEADS):
        sl = slice(h * LANES, (h + 1) * LANES)
        kf = kn[:, sl] + kpe
        kb_ref[:, sl] = ((kf * cgk + rot) * _rms(kf, MLA_QK)).astype(BF16)

    for h in range(N_HEADS):
        sl = slice(h * LANES, (h + 1) * LANES)
        xq = pf[:, F_DSQ + h * LANES:F_DSQ + (h + 1) * LANES]
        dq_ref[:, sl] = (xq * _rms(xq, HEAD_DIM) * gdq_ref[:, sl]).astype(BF16)
    xk = pf[:, F_DSK:F_DSK + LANES]
    dkk_ref[...] = (xk * _rms(xk, 2 * HEAD_DIM) * gdk_ref[...]).astype(BF16)


def _prologue(x2, cos_t, sin_t, lw, tm, sparse_tc):
    m, d = x2.shape
    assert tm % sparse_tc == 0

    def whole(a):
        return pl.BlockSpec(a.shape, lambda i: (0,) * a.ndim)

    weights = [lw[k] for k in ("wuq", "wuqr", "wuk", "wuv", "qng", "kvng", "gq", "gqr", "gk", "gkr", "gdq", "gdk")]
    row = lambda width: pl.BlockSpec((tm, width), lambda i: (i, 0))
    return pl.pallas_call(
        _prologue_kernel,
        grid=(m // tm,),
        in_specs=[row(d), whole(lw["norm_g"]), pl.BlockSpec((d, N_F32), lambda i: (0, 0)), row(LANES), row(LANES)]
        + [whole(w) for w in weights],
        out_specs=[row(1024), row(1024),
                   pl.BlockSpec((1, N_HEADS // 2, LANES + 8, tm), lambda i: (i, 0, 0, 0)),
                   row(1024), row(LANES), row(LANES),
                   pl.BlockSpec((1, tm // sparse_tc, V_ROWS, sparse_tc), lambda i: (i, 0, 0, 0))],
        out_shape=[jax.ShapeDtypeStruct((m, 1024), BF16), jax.ShapeDtypeStruct((m, 1024), BF16),
                   jax.ShapeDtypeStruct((m // tm, N_HEADS // 2, LANES + 8, tm), BF16),
                   jax.ShapeDtypeStruct((m, 1024), BF16),
                   jax.ShapeDtypeStruct((m, LANES), BF16), jax.ShapeDtypeStruct((m, LANES), F32),
                   jax.ShapeDtypeStruct((m // tm, tm // sparse_tc, V_ROWS, sparse_tc), BF16)],
        compiler_params=pltpu.CompilerParams(dimension_semantics=("parallel",), vmem_limit_bytes=VMEM_LIMIT),
        name="prologue",
    )(x2, lw["norm_g"], lw["w_all"], cos_t, sin_t, *weights)


def _sublane_all(x, op):
    for shift in (4, 2, 1):
        x = op(x, pltpu.roll(x, shift, 0))
    return x


def _sb_kernel(q_ref, k_ref, v_ref, o_ref, acc_ref, carry_ref, hl_ref, lb_ref, *, tq):
    t = LANES
    qi = pl.program_id(1)
    jj = lax.broadcasted_iota(I32, (t, 2 * t), 0)
    ss = lax.broadcasted_iota(I32, (t, 2 * t), 1)
    tri_ones = jnp.logical_or(ss >= t, jj > ss).astype(BF16)
    q_idx = qi * tq + lax.broadcasted_iota(I32, (tq, t), 0)
    k_lane = lax.broadcasted_iota(I32, (tq, t), 1)

    acc_ref[...] = jnp.zeros(acc_ref.shape, F32)
    carry_ref[...] = jnp.zeros(carry_ref.shape, F32)

    def cond(c):
        kb, cmax = c
        return jnp.logical_and(kb >= 0, cmax > -SB_UNDERFLOW)

    def body(c):
        kb, _ = c
        start = pl.multiple_of(kb * t, t)
        mask = (start + k_lane) < q_idx
        for h in range(N_HEADS):
            q = q_ref[0, :, h * LANES:(h + 1) * LANES]
            k = k_ref[0, pl.ds(start, t), (h // 2) * LANES:(h // 2 + 1) * LANES]
            z = lax.dot_general(q, k, NT_DIMS, preferred_element_type=F32)
            sp = jnp.maximum(z, 0.0) + jnp.log(1.0 + jnp.exp(-jnp.abs(z)))
            l1m = jnp.where(mask, -sp, 0.0)
            hi = l1m.astype(BF16)
            hl_ref[2 * h * tq:(2 * h + 1) * tq, :] = hi
            hl_ref[(2 * h + 1) * tq:(2 * h + 2) * tq, :] = (l1m - hi.astype(F32)).astype(BF16)
            lb_ref[h * tq:(h + 1) * tq, :] = z - sp
        sums = jnp.dot(hl_ref[...], tri_ones, preferred_element_type=F32)
        for h in range(N_HEADS):
            rows = slice(h * tq, (h + 1) * tq)
            both = sums[2 * h * tq:(2 * h + 1) * tq, :] + sums[(2 * h + 1) * tq:(2 * h + 2) * tq, :]
            carry = carry_ref[rows, :]
            a = jnp.where(mask, jnp.exp(lb_ref[rows, :] + carry + both[:, :t]), 0.0)
            v = v_ref[0, pl.ds(start, t), (h // 2) * LANES:(h // 2 + 1) * LANES]
            acc_ref[rows, :] += jnp.dot(a.astype(BF16), v, preferred_element_type=F32)
            carry_ref[rows, :] = carry + both[:, t:]
        return kb - 1, jnp.max(carry_ref[...])

    lax.while_loop(cond, body, (((qi + 1) * tq) // t - 1, jnp.float32(0.0)))

    lane = lax.broadcasted_iota(I32, (tq, LANES), 1)
    for p in range(N_HEADS // 2):
        even = acc_ref[2 * p * tq:(2 * p + 1) * tq, :]
        odd = acc_ref[(2 * p + 1) * tq:(2 * p + 2) * tq, :]
        o_ref[0, :, p * LANES:(p + 1) * LANES] = jnp.where(lane < HEAD_DIM, even, odd)


def _sb_attention(pb, b, s, tq):
    pb3 = pb.reshape(b, s, N_BF)
    kernel = functools.partial(_sb_kernel, tq=tq)
    return pl.pallas_call(
        kernel,
        grid=(b, s // tq),
        in_specs=[pl.BlockSpec((1, tq, N_HEADS * LANES), lambda bi, i: (bi, i, B_SAQ // (N_HEADS * LANES))),
                  pl.BlockSpec((1, s, BRANCH_WIDTH), lambda bi, i: (bi, 0, B_SAK // BRANCH_WIDTH)),
                  pl.BlockSpec((1, s, BRANCH_WIDTH), lambda bi, i: (bi, 0, B_SAV // BRANCH_WIDTH))],
        out_specs=pl.BlockSpec((1, tq, BRANCH_WIDTH), lambda bi, i: (bi, i, 0)),
        out_shape=jax.ShapeDtypeStruct((b, s, BRANCH_WIDTH), F32),
        scratch_shapes=[pltpu.VMEM((N_HEADS * tq, LANES), F32),
                        pltpu.VMEM((N_HEADS * tq, LANES), F32),
                        pltpu.VMEM((2 * N_HEADS * tq, LANES), BF16),
                        pltpu.VMEM((N_HEADS * tq, LANES), F32)],
        compiler_params=pltpu.CompilerParams(dimension_semantics=("parallel", "arbitrary"),
                                             vmem_limit_bytes=VMEM_LIMIT),
        name="sb_attention",
    )(pb3, pb3, pb3)


def _mla_kernel(q_ref, k_ref, vt_ref, o_ref, qbd_ref, p_ref, m_ref, alpha_ref, acc_ref, *, tq, tc):
    t = LANES
    assert tq == t
    groups = tc // 8
    pair_w = 2 * tq
    qi = pl.program_id(1)
    n_chunks = (qi * tq) // tc + 1
    k_in_chunk = (lax.broadcasted_iota(I32, (groups, 8, tq), 0) * 8
                  + lax.broadcasted_iota(I32, (groups, 8, tq), 1))
    q_lane = qi * tq + lax.broadcasted_iota(I32, (groups, 8, tq), 2)

    zeros = jnp.zeros((tq, t), BF16)
    for p in range(N_HEADS // 2):
        q_even = q_ref[0, :, 2 * p * LANES:(2 * p + 1) * LANES]
        q_odd = q_ref[0, :, (2 * p + 1) * LANES:(2 * p + 2) * LANES]
        qbd_ref[p, 0:tq, :] = jnp.concatenate([q_even, zeros], axis=1)
        qbd_ref[p, tq:2 * tq, :] = jnp.concatenate([zeros, q_odd], axis=1)

    v_rows = acc_ref.shape[0]
    m_ref[...] = jnp.full(m_ref.shape, NEG_BIG, F32)
    acc_ref[...] = jnp.zeros(acc_ref.shape, F32)

    def chunk(c, diagonal):
        start = pl.multiple_of(c * tc, tc)
        pair_cols = [slice(p * pair_w, (p + 1) * pair_w) for p in range(N_HEADS // 2)]
        s_pairs = [lax.dot_general(k_ref[0, pl.ds(start, tc), pair_cols[p]], qbd_ref[p], NT_DIMS,
                                   preferred_element_type=F32) for p in range(N_HEADS // 2)]
        for p in range(N_HEADS // 2):
            pcols = pair_cols[p]
            for i in range(2):
                cols = slice(p * pair_w + i * tq, p * pair_w + (i + 1) * tq)
                sb = s_pairs[p][:, i * tq:(i + 1) * tq].reshape(groups, 8, tq)
                if diagonal:
                    sb = jnp.where(start + k_in_chunk <= q_lane, sb, NEG_BIG)
                m_old = m_ref[:, cols]
                m_new = jnp.maximum(m_old, _sublane_all(jnp.max(sb, axis=0), jnp.maximum))
                prob = jnp.exp(sb - m_new[None])
                alpha = jnp.exp(m_old - m_new)
                p_ref[:, cols] = prob.reshape(tc, tq).astype(BF16)
                alpha_ref[:, cols] = alpha
                m_ref[:, cols] = m_new
            pv = jnp.dot(vt_ref[0, c, p], p_ref[:, pcols], preferred_element_type=F32)
            scaled = acc_ref[:, pcols].reshape(v_rows // 8, 8, pair_w) * alpha_ref[:, pcols][None]
            acc_ref[:, pcols] = scaled.reshape(v_rows, pair_w) + pv

    def body(c, carry):
        chunk(c, False)
        return carry

    lax.fori_loop(0, n_chunks - 1, body, 0)
    chunk(n_chunks - 1, True)

    for p in range(N_HEADS // 2):
        outs = []
        for i in range(2):
            cols = slice(p * pair_w + i * tq, p * pair_w + (i + 1) * tq)
            outs.append(acc_ref[i * HEAD_DIM:(i + 1) * HEAD_DIM, cols] / acc_ref[t:t + 1, cols])
        o_ref[0, :, p * LANES:(p + 1) * LANES] = jnp.transpose(jnp.concatenate(outs, axis=0))


def _mla_attention(qb, kb, vbt, b, s, tq, tc):
    width = N_HEADS * tq
    kernel = functools.partial(_mla_kernel, tq=tq, tc=tc)
    return pl.pallas_call(
        kernel,
        grid=(b, s // tq),
        in_specs=[pl.BlockSpec((1, tq, N_HEADS * LANES), lambda bi, i: (bi, i, 0)),
                  pl.BlockSpec((1, s, N_HEADS * LANES), lambda bi, i: (bi, 0, 0)),
                  pl.BlockSpec((1, s // tc, N_HEADS // 2, LANES + 8, tc), lambda bi, i: (bi, 0, 0, 0, 0))],
        out_specs=pl.BlockSpec((1, tq, BRANCH_WIDTH), lambda bi, i: (bi, i, 0)),
        out_shape=jax.ShapeDtypeStruct((b, s, BRANCH_WIDTH), F32),
        scratch_shapes=[pltpu.VMEM((N_HEADS // 2, 2 * tq, 2 * LANES), BF16),
                        pltpu.VMEM((tc, width), BF16),
                        pltpu.VMEM((8, width), F32),
                        pltpu.VMEM((8, width), F32),
                        pltpu.VMEM((LANES + 8, width), F32)],
        compiler_params=pltpu.CompilerParams(dimension_semantics=("parallel", "arbitrary"),
                                             vmem_limit_bytes=VMEM_LIMIT),
        name="mla_attention",
    )(qb.reshape(b, s, -1), kb.reshape(b, s, -1), vbt)


def _sortable(x):
    bits = pltpu.bitcast(x + 0.0, I32)
    return bits ^ ((bits >> 31) & 0x7FFFFFFF)


def _dsa_kernel(rel_ref, tab_ref, ixq_ref, ixw_ref, ixk_ref, q_ref, kk_ref, vvt_ref, pos_ref, o_ref,
                keys_ref, qm_ref, qs_ref, wt_ref, p_ref, selb_ref, add_ref,
                m_ref, alpha_ref, acc_ref, thr_ref, *, tq, tc, topk, seq):
    t = LANES
    assert tq == t
    groups = tc // 8
    width = N_HEADS * tq
    qi = pl.program_id(1)
    n_chunks = (qi * tq) // tc + 1
    head_cols = [slice(h * tq, (h + 1) * tq) for h in range(N_HEADS)]
    lane = lax.broadcasted_iota(I32, (tq, t), 1)
    q_lane = qi * tq + lax.broadcasted_iota(I32, (groups, 8, tq), 2)
    k_in_chunk = (lax.broadcasted_iota(I32, (groups, 8, tq), 0) * 8
                  + lax.broadcasted_iota(I32, (groups, 8, tq), 1))

    for h in range(N_HEADS):
        grp = ixq_ref[0, :, (h // 4) * LANES:(h // 4 + 1) * LANES]
        lo = (h % 4) * IDX_DIM
        qm_ref[h * tq:(h + 1) * tq, :] = jnp.where((lane >= lo) & (lane < lo + IDX_DIM), grp, jnp.zeros_like(grp))
        qs_ref[h * tq:(h + 1) * tq, :] = q_ref[0, :, h * LANES:(h + 1) * LANES]
    wt_ref[...] = jnp.transpose(ixw_ref[0] * ((IDX_DIM ** -0.5) * (N_HEADS ** -0.5)))

    def chunk3(ref, c):
        start = pl.multiple_of(c * tc, tc)
        return ref[pl.ds(start, tc), :].reshape(groups, 8, tq)

    def score_chunk(c, carry):
        start = pl.multiple_of(c * tc, tc)
        st = lax.dot_general(ixk_ref[0, pl.ds(start, tc), :], qm_ref[...], NT_DIMS, preferred_element_type=F32)
        sc = None
        for h in range(N_HEADS):
            term = jnp.maximum(st[:, head_cols[h]], 0.0) * wt_ref[h:h + 1, :]
            sc = term if sc is None else sc + term
        key = jnp.where(start + k_in_chunk <= q_lane, _sortable(sc).reshape(groups, 8, tq), INT_MIN)
        keys_ref[pl.ds(start, tc), :] = key.reshape(tc, tq)
        return carry

    lax.fori_loop(0, n_chunks, score_chunk, 0)

    def count(pred):
        def one(c, cnt):
            return cnt + jnp.sum(pred(chunk3(keys_ref, c), c * tc).astype(I32), axis=0)

        def two(i, cnt):
            return one(2 * i + 1, one(2 * i, cnt))

        cnt = lax.fori_loop(0, n_chunks // 2, two, jnp.zeros((8, tq), I32))
        cnt = lax.fori_loop(2 * (n_chunks // 2), n_chunks, one, cnt)
        return _sublane_all(cnt, jnp.add)

    @pl.when((qi + 1) * tq <= topk)
    def _():
        thr_ref[0] = jnp.full((8, tq), INT_MIN, I32)
        thr_ref[1] = jnp.full((8, tq), -1, I32)

    @pl.when((qi + 1) * tq > topk)
    def _():
        ans = jnp.zeros((8, tq), I32)
        n_ge = jnp.full((8, tq), seq, I32)
        for bit in range(31, -1, -1):
            cand = ans | np.int32(INT_MIN if bit == 31 else 1 << bit)
            cand_s = cand ^ np.int32(INT_MIN)
            c = count(lambda keys, start, cand_s=cand_s: keys >= cand_s[None])
            take = c >= topk
            ans = jnp.where(take, cand, ans)
            n_ge = jnp.where(take, c, n_ge)
        tau = ans ^ np.int32(INT_MIN)
        thr_ref[0] = tau
        thr_ref[1] = jnp.full((8, tq), seq, I32)

        @pl.when(jnp.max(n_ge) > topk)
        def _():
            need = topk - count(lambda keys, start: keys > tau[None])
            cut = jnp.zeros((8, tq), I32)
            for bit in range(int(math.log2(seq)), -1, -1):
                cand = cut | np.int32(1 << bit)
                c = count(lambda keys, start, cand=cand: (keys == tau[None]) & (start + k_in_chunk < cand[None]))
                cut = jnp.where(c < need, cand, cut)
            thr_ref[1] = cut

    tau = thr_ref[0]
    cut = thr_ref[1]

    v_rows = acc_ref.shape[0]
    m_ref[...] = jnp.full(m_ref.shape, NEG_BIG, F32)
    acc_ref[...] = jnp.zeros(acc_ref.shape, F32)
    pq = pos_ref[0, qi]
    pq_min = jnp.min(pq)
    last_bias = [rel_ref[REL_BUCKETS - 1, h] for h in range(N_HEADS)]

    def softmax_head(h, s_t, addend, bias):
        cols = head_cols[h]
        sb = s_t.reshape(groups, 8, tq) + addend
        m_old = m_ref[:, cols]
        m_new = jnp.maximum(m_old, _sublane_all(jnp.max(sb, axis=0), jnp.maximum) + bias)
        p = jnp.exp(sb - (m_new - bias)[None])
        alpha = jnp.exp(m_old - m_new)
        p_ref[:, cols] = p.reshape(tc, tq).astype(BF16)
        alpha_ref[:, cols] = alpha
        m_ref[:, cols] = m_new

    def accumulate(c, half):
        cols = slice(half * (width // 2), (half + 1) * (width // 2))
        pv = jnp.dot(vvt_ref[0, c], p_ref[:, cols], preferred_element_type=F32)
        scaled = acc_ref[:, cols].reshape(v_rows // 8, 8, width // 2) * alpha_ref[:, cols][None]
        acc_ref[:, cols] = scaled.reshape(v_rows, width // 2) + pv

    def all_heads(c, addend_of, bias_of):
        start = pl.multiple_of(c * tc, tc)
        s_all = lax.dot_general(kk_ref[0, pl.ds(start, tc), :], qs_ref[...], NT_DIMS, preferred_element_type=F32)
        for half in range(2):
            for h in range(half * N_HEADS // 2, (half + 1) * N_HEADS // 2):
                softmax_head(h, s_all[:, head_cols[h]], addend_of(h), bias_of(h))
            accumulate(c, half)

    def attend_chunk(c, carry):
        keys = chunk3(keys_ref, c)
        sel = (keys > tau[None]) | ((keys == tau[None]) & (k_in_chunk <= (cut - c * tc)[None]))
        selb_ref[...] = jnp.where(sel, 0.0, NEG_BIG).reshape(tc, tq)
        first_tile = c * (tc // t)
        pk_max = pos_ref[0, first_tile]
        for j in range(1, tc // t):
            pk_max = jnp.maximum(pk_max, pos_ref[0, first_tile + j])
        chunk_far = pq_min - jnp.max(pk_max) >= FAR_DIST

        @pl.when(chunk_far)
        def _():
            all_heads(c, lambda h: selb_ref[...].reshape(groups, 8, tq), lambda h: last_bias[h])

        @pl.when(jnp.logical_not(chunk_far))
        def _():
            for j in range(tc // t):
                rows = slice(j * t, (j + 1) * t)
                pk = pos_ref[0, first_tile + j]
                tile_far = jnp.logical_or(pq_min - jnp.max(pk) >= FAR_DIST, c * tc + j * t >= (qi + 1) * tq)

                @pl.when(tile_far)
                def _():
                    for h in range(N_HEADS):
                        add_ref[rows, head_cols[h]] = selb_ref[rows, :] + last_bias[h]

                @pl.when(jnp.logical_not(tile_far))
                def _():
                    pk_col = jnp.transpose(jnp.broadcast_to(pk, (t, t)))
                    dist = jnp.clip(pq - pk_col, 0, t - 1)
                    for h in range(N_HEADS):
                        table = jnp.broadcast_to(tab_ref[h:h + 1, :], (t, t))
                        bias = jnp.take_along_axis(table, dist, axis=1, mode="promise_in_bounds")
                        add_ref[rows, head_cols[h]] = selb_ref[rows, :] + bias

            all_heads(c, lambda h: add_ref[:, head_cols[h]].reshape(groups, 8, tq), lambda h: 0.0)

        return carry

    lax.fori_loop(0, n_chunks, attend_chunk, 0)

    for p in range(N_HEADS // 2):
        outs = []
        for h in (2 * p, 2 * p + 1):
            cols = head_cols[h]
            outs.append(acc_ref[0:HEAD_DIM, cols] / acc_ref[HEAD_DIM:HEAD_DIM + 1, cols])
        o_ref[0, :, p * LANES:(p + 1) * LANES] = jnp.transpose(jnp.concatenate(outs, axis=0))


def _dsa_attention(rel_bias, dist_table, pb, ixw, dq, dkk, vvt, pos_rows, b, s, tq, tc):
    topk = min(TOPK_MAX, s // 4)
    pb3 = pb.reshape(b, s, N_BF)
    width = N_HEADS * tq
    kernel = functools.partial(_dsa_kernel, tq=tq, tc=tc, topk=topk, seq=s)
    return pl.pallas_call(
        kernel,
        grid=(b, s // tq),
        in_specs=[pl.BlockSpec(memory_space=pltpu.SMEM),
                  pl.BlockSpec(dist_table.shape, lambda bi, i: (0, 0)),
                  pl.BlockSpec((1, tq, 2 * LANES), lambda bi, i: (bi, i, B_IXQ // 256)),
                  pl.BlockSpec((1, tq, LANES), lambda bi, i: (bi, i, 0)),
                  pl.BlockSpec((1, s, LANES), lambda bi, i: (bi, 0, B_IXK // LANES)),
                  pl.BlockSpec((1, tq, N_HEADS * LANES), lambda bi, i: (bi, i, 0)),
                  pl.BlockSpec((1, s, LANES), lambda bi, i: (bi, 0, 0)),
                  pl.BlockSpec((1, s // tc, V_ROWS, tc), lambda bi, i: (bi, 0, 0, 0)),
                  pl.BlockSpec((1, s // LANES, 1, LANES), lambda bi, i: (bi, 0, 0, 0))],
        out_specs=pl.BlockSpec((1, tq, BRANCH_WIDTH), lambda bi, i: (bi, i, 0)),
        out_shape=jax.ShapeDtypeStruct((b, s, BRANCH_WIDTH), F32),
        scratch_shapes=[pltpu.VMEM((s, tq), I32),
                        pltpu.VMEM((width, LANES), BF16),
                        pltpu.VMEM((width, LANES), BF16),
                        pltpu.VMEM((LANES, tq), F32),
                        pltpu.VMEM((tc, width), BF16),
                        pltpu.VMEM((tc, tq), F32),
                        pltpu.VMEM((tc, width), F32),
                        pltpu.VMEM((8, width), F32),
                        pltpu.VMEM((8, width), F32),
                        pltpu.VMEM((V_ROWS, width), F32),
                        pltpu.VMEM((2, 8, tq), I32)],
        compiler_params=pltpu.CompilerParams(dimension_semantics=("parallel", "arbitrary"),
                                             vmem_limit_bytes=VMEM_LIMIT),
        name="dsa_attention",
    )(rel_bias, dist_table, pb3, ixw.reshape(b, s, LANES), pb3, dq.reshape(b, s, -1), dkk.reshape(b, s, -1), vvt, pos_rows)


def _merge_kernel(x_ref, ya_ref, yb_ref, yc_ref, za_ref, zb_ref, zc_ref, ga_ref, gb_ref, gc_ref,
                  bias_ref, wbr_ref, wout_ref, o_ref):
    merged = None
    branches = ((ya_ref, za_ref, ga_ref), (yb_ref, zb_ref, gb_ref), (yc_ref, zc_ref, gc_ref))
    for n, (y_ref, z_ref, g_ref) in enumerate(branches):
        z = z_ref[...].astype(F32)
        branch = (y_ref[...] * (z * jax.nn.sigmoid(z))).astype(BF16)
        up = jnp.dot(branch, wbr_ref[n], preferred_element_type=F32)
        gate = jax.nn.sigmoid(g_ref[...].astype(F32) + bias_ref[n:n + 1, :])
        merged = gate * up if merged is None else merged + gate * up
    o_ref[...] = x_ref[...] + jnp.dot(merged.astype(BF16), wout_ref[...], preferred_element_type=F32)


def _merge(x2, ya, yb, yc, pb, gate_b, wbr, wout, tm):
    m, d = x2.shape
    row = lambda width: pl.BlockSpec((tm, width), lambda i: (i, 0))
    col = lambda width, off: pl.BlockSpec((tm, width), lambda i, _b=off // width: (i, _b))
    z_specs = [col(BRANCH_WIDTH, B_Z + n * BRANCH_WIDTH) for n in range(N_BRANCH)]
    g_specs = [col(d, B_G + n * d) for n in range(N_BRANCH)]
    return pl.pallas_call(
        _merge_kernel,
        grid=(m // tm,),
        in_specs=[row(d), row(BRANCH_WIDTH), row(BRANCH_WIDTH), row(BRANCH_WIDTH)] + z_specs + g_specs + [
            pl.BlockSpec(gate_b.shape, lambda i: (0, 0)),
            pl.BlockSpec(wbr.shape, lambda i: (0, 0, 0)),
            pl.BlockSpec(wout.shape, lambda i: (0, 0))],
        out_specs=row(d),
        out_shape=jax.ShapeDtypeStruct((m, d), F32),
        compiler_params=pltpu.CompilerParams(dimension_semantics=("parallel",), vmem_limit_bytes=VMEM_LIMIT),
        name="merge",
    )(x2, ya, yb, yc, pb, pb, pb, pb, pb, pb, gate_b, wbr, wout)


def _lane_table(vals_by_lane):
    t = jnp.zeros((LANES,), F32)
    for start, v in vals_by_lane.items():
        t = t.at[start:start + v.shape[0]].set(v)
    return t[None, :]


def _layer_weights(l, w_all, norm_g, mla_q_norm_g, mla_kv_norm_g, mla_w_uq, mla_w_ukv, mla_q_g, mla_k_g,
                   dsa_q_g, dsa_k_g, gate_b, w_branch, w_out):
    uq, ukv = mla_w_uq[l], mla_w_ukv[l]
    half = MLA_ROPE // 2
    q_pieces, qr_pieces, k_pieces, v_pieces = [], [], [], []
    for h in range(N_HEADS):
        o = h * MLA_QK
        q_pieces += [(o, MLA_QK, 1.0), (None, LANES - MLA_QK, 1.0)]
        qr_pieces += [(None, MLA_NOPE, 1.0), (o + MLA_NOPE + half, half, 1.0), (o + MLA_NOPE, half, 1.0),
                      (None, LANES - MLA_QK, 1.0)]
        k_pieces += [(h * 128, MLA_NOPE, 1.0), (None, LANES - MLA_NOPE, 1.0)]
        v_pieces += [(h * 128 + MLA_NOPE, 64, 1.0)]
    scale = MLA_QK ** -0.5
    qg, kg = mla_q_g[l], mla_k_g[l]
    rot_gain = lambda g: jnp.concatenate([g[MLA_NOPE + half:], g[MLA_NOPE:MLA_NOPE + half]])
    dq_g = jnp.concatenate([jnp.concatenate([dsa_q_g[l], jnp.zeros((HEAD_DIM,), F32)]) if h % 2 == 0 else
                            jnp.concatenate([jnp.zeros((HEAD_DIM,), F32), dsa_q_g[l]]) for h in range(N_HEADS)])
    return dict(
        norm_g=norm_g[l][None, :], w_all=w_all[l],
        wuq=_assemble(uq, q_pieces).astype(BF16), wuqr=_assemble(uq, qr_pieces).astype(BF16),
        wuk=_assemble(ukv, k_pieces).astype(BF16), wuv=_assemble(ukv, v_pieces).astype(BF16),
        qng=mla_q_norm_g[l][None, :], kvng=mla_kv_norm_g[l][None, :],
        gq=_lane_table({0: qg * scale}), gqr=_lane_table({MLA_NOPE: rot_gain(qg) * scale}),
        gk=_lane_table({0: kg}), gkr=_lane_table({MLA_NOPE: rot_gain(kg)}),
        gdq=(dq_g * HEAD_DIM ** -0.5)[None, :], gdk=jnp.concatenate([dsa_k_g[l], dsa_k_g[l]])[None, :],
        gate_b=gate_b[l], wbr=w_branch[l].astype(BF16), wout=w_out[l].astype(BF16))


def kernel(x, positions, norm_g, w_in, mla_q_norm_g, mla_kv_norm_g, mla_w_uq, mla_w_ukv, mla_q_g, mla_k_g,
           dsa_q_g, dsa_k_g, rel_bias, gate_b, w_branch, w_out):
    b, s, d = x.shape
    depth = w_in.shape[0]
    m = b * s
    assert d == D_MODEL and w_in.shape[-1] == D_IN and s % 512 == 0 and s & (s - 1) == 0

    half = MLA_ROPE // 2
    inv_freq = ROPE_THETA ** (-(jnp.arange(half, dtype=F32) * 2.0) / MLA_ROPE)
    ang = positions.astype(F32)[..., None] * inv_freq
    cos, sin = jnp.cos(ang).reshape(m, half), jnp.sin(ang).reshape(m, half)
    cos_t = jnp.concatenate([jnp.ones((m, MLA_NOPE), F32), cos, cos, jnp.ones((m, LANES - MLA_QK), F32)], axis=1)
    sin_t = jnp.concatenate([jnp.zeros((m, MLA_NOPE), F32), -sin, sin, jnp.zeros((m, LANES - MLA_QK), F32)], axis=1)

    bf_pieces, f32_pieces = _in_proj_layouts()
    w_in16 = w_in.astype(BF16)
    w_all = _assemble(w_in16, f32_pieces + bf_pieces)

    tm = min(1024, m)
    mla_tc = min(1024, s)
    dsa_tc = 512
    tp = min(512, m)
    pos_rows = positions.reshape(b, s // LANES, 1, LANES)
    dist_table = rel_bias[BUCKET_OF_DIST, :].T
    x2 = x.reshape(m, d)
    for l in range(depth):
        lw = _layer_weights(l, w_all, norm_g, mla_q_norm_g, mla_kv_norm_g, mla_w_uq, mla_w_ukv,
                            mla_q_g, mla_k_g, dsa_q_g, dsa_k_g, gate_b, w_branch, w_out)
        pb = _in_proj(x2, lw["norm_g"], lw["w_all"], N_F32, N_BF, BF16, min(2 * tm, m), 1024)
        qb, kb, vbt, dq, dkk, ixw, vvt = _prologue(x2, cos_t, sin_t, lw, mla_tc, dsa_tc)
        ya = _sb_attention(pb, b, s, 128)
        vbt = vbt.reshape(b, s // mla_tc, N_HEADS // 2, LANES + 8, mla_tc)
        yb = _mla_attention(qb, kb, vbt, b, s, LANES, mla_tc)
        vvt = vvt.reshape(b, s // dsa_tc, V_ROWS, dsa_tc)
        yc = _dsa_attention(rel_bias, dist_table, pb, ixw, dq, dkk, vvt, pos_rows, b, s, LANES, dsa_tc)
        x2 = _merge(x2, ya.reshape(m, -1), yb.reshape(m, -1), yc.reshape(m, -1), pb, lw["gate_b"],
                    lw["wbr"], lw["wout"], tp)
    return x2.reshape(b, s, d)
```

```python
import functools
import math

import numpy as np
import jax
import jax.numpy as jnp
from jax import lax
from jax.experimental import pallas as pl
from jax.experimental.pallas import tpu as pltpu

F32 = jnp.float32
BF16 = jnp.bfloat16
I32 = jnp.int32

LANES = 128
EPS = 1e-6
D_MODEL = 1024
N_HEADS = 8
HEAD_DIM = 64
MLA_Q_LORA = 256
MLA_KV_LORA = 128
MLA_NOPE = 64
MLA_ROPE = 32
MLA_QK = MLA_NOPE + MLA_ROPE
ROPE_THETA = 10000.0
IDX_DIM = 32
TOPK_MAX = 256
REL_BUCKETS = 32
REL_MAX_DIST = 128
N_BRANCH = 3
BRANCH_WIDTH = 512

O_SAQ, O_SAK, O_SAV = 0, 512, 1024
O_CQ, O_CKV, O_KPE = 1536, 1792, 1920
O_DSQ, O_DSK, O_DSV = 1952, 2464, 2528
O_IXQ, O_IXK, O_IXW = 2592, 2848, 2880
O_Z, O_G = 2888, 4424
D_IN = 7496

NEG_BIG = -1e30
INT_MIN = -(2 ** 31)
SB_UNDERFLOW = 104.0
VMEM_LIMIT = 48 * 1024 * 1024

NT_DIMS = (((1,), (1,)), ((), ()))
V_ROWS = HEAD_DIM + 8


def _t5_buckets():
    exact = REL_BUCKETS // 2
    n = np.arange(0, 4 * REL_MAX_DIST)
    nf = np.maximum(n, 1).astype(np.float64)
    large = exact + (np.log(nf / exact) / math.log(REL_MAX_DIST / exact) * (REL_BUCKETS - exact)).astype(np.int64)
    bucket = np.where(n < exact, n, np.minimum(large, REL_BUCKETS - 1))
    assert np.all(np.diff(bucket) >= 0) and np.all(bucket[LANES - 1:] == REL_BUCKETS - 1)
    return bucket[:LANES]


BUCKET_OF_DIST = _t5_buckets()
FAR_DIST = int(np.argmax(BUCKET_OF_DIST == REL_BUCKETS - 1))


def _assemble(w, pieces):
    cols = []
    for start, width, scale in pieces:
        if start is None:
            cols.append(jnp.zeros(w.shape[:-1] + (width,), w.dtype))
        else:
            c = w[..., start:start + width]
            cols.append(c if scale == 1.0 else c * scale)
    return jnp.concatenate(cols, axis=-1)


def _head_padded(start, scale=1.0):
    pieces = []
    for h in range(N_HEADS):
        col = (start + h * HEAD_DIM, HEAD_DIM, scale)
        pad = (None, HEAD_DIM, 1.0)
        pieces += [col, pad] if h % 2 == 0 else [pad, col]
    return pieces


def _in_proj_layouts():
    bf = (_head_padded(O_SAQ, HEAD_DIM ** -0.5)
          + [(O_SAK, 512, 1.0), (O_SAV, 512, 1.0), (O_IXQ, 256, 1.0)]
          + [(None, 128, 1.0)]
          + [(O_IXK, 32, 1.0)] * 4
          + [(O_Z, 1536, 1.0), (O_G, 3072, 1.0)])
    f32 = (_head_padded(O_DSQ)
           + [(O_CQ, 256, 1.0), (O_CKV, 128, 1.0)]
           + [(None, 64, 1.0), (O_KPE, 32, 1.0), (None, 32, 1.0)]
           + [(None, 64, 1.0), (O_KPE + 16, 16, 1.0), (O_KPE, 16, 1.0), (None, 32, 1.0)]
           + [(O_DSK, 64, 1.0)] * 2
           + [(O_IXW, 8, 1.0), (None, 120, 1.0)]
           + [(O_DSV, 64, 1.0), (None, 64, 1.0)])
    return bf, f32


B_SAQ, B_SAK, B_SAV, B_IXQ, B_IXK, B_Z, B_G, N_BF = 0, 1024, 1536, 2048, 2432, 2560, 4096, 7168
F_DSQ, F_CQ, F_CKV, F_KPE, F_KPER, F_DSK, F_IXW, F_DSV, N_F32 = 0, 1024, 1280, 1408, 1536, 1664, 1792, 1920, 2048


def _in_proj_kernel(x_ref, g_ref, w_ref, o_ref, h_ref):
    @pl.when(pl.program_id(1) == 0)
    def _():
        x = x_ref[...]
        ms = jnp.mean(x * x, axis=-1, keepdims=True)
        h_ref[...] = (x * lax.rsqrt(ms + EPS) * g_ref[...]).astype(BF16)

    o_ref[...] = jnp.dot(h_ref[...], w_ref[...], preferred_element_type=F32).astype(o_ref.dtype)


def _in_proj(x2, g, w, col0, n, out_dtype, tm, tn):
    m, d = x2.shape
    return pl.pallas_call(
        _in_proj_kernel,
        grid=(m // tm, n // tn),
        in_specs=[pl.BlockSpec((tm, d), lambda i, j: (i, 0)),
                  pl.BlockSpec((1, d), lambda i, j: (0, 0)),
                  pl.BlockSpec((d, tn), lambda i, j: (0, col0 // tn + j))],
        out_specs=pl.BlockSpec((tm, tn), lambda i, j: (i, j)),
        out_shape=jax.ShapeDtypeStruct((m, n), out_dtype),
        scratch_shapes=[pltpu.VMEM((tm, d), BF16)],
        compiler_params=pltpu.CompilerParams(dimension_semantics=("parallel", "arbitrary"),
                                             vmem_limit_bytes=VMEM_LIMIT),
        name="in_proj",
    )(x2, g, w)


def _rms(x, width):
    return lax.rsqrt(jnp.sum(x * x, axis=-1, keepdims=True) * (1.0 / width) + EPS)


def _prologue_kernel(x_ref, ng_ref, wf_ref, cos_ref, sin_ref,
                     wuq_ref, wuqr_ref, wuk_ref, wuv_ref,
                     qng_ref, kvng_ref, gq_ref, gqr_ref, gk_ref, gkr_ref, gdq_ref, gdk_ref,
                     qb_ref, kb_ref, vbt_ref, dq_ref, dkk_ref, ixw_ref, vvt_ref):
    x = x_ref[...]
    hidden = (x * lax.rsqrt(jnp.mean(x * x, axis=-1, keepdims=True) + EPS) * ng_ref[...]).astype(BF16)
    pf = jnp.dot(hidden, wf_ref[...], preferred_element_type=F32)
    ixw_ref[...] = pf[:, F_IXW:F_IXW + LANES]
    cos = cos_ref[...]
    sin = sin_ref[...]

    cq = pf[:, F_CQ:F_CQ + MLA_Q_LORA]
    cqn = (cq * _rms(cq, MLA_Q_LORA) * qng_ref[...]).astype(BF16)
    q = jnp.dot(cqn, wuq_ref[...], preferred_element_type=F32)
    qr = jnp.dot(cqn, wuqr_ref[...], preferred_element_type=F32)
    cg = cos * gq_ref[...]
    sg = sin * gqr_ref[...]
    for h in range(N_HEADS):
        sl = slice(h * LANES, (h + 1) * LANES)
        qh = q[:, sl]
        qb_ref[:, sl] = ((qh * cg + qr[:, sl] * sg) * _rms(qh, MLA_QK)).astype(BF16)

    ckv = pf[:, F_CKV:F_CKV + MLA_KV_LORA]
    ckvn = (ckv * _rms(ckv, MLA_KV_LORA) * kvng_ref[...]).astype(BF16)
    kn = jnp.dot(ckvn, wuk_ref[...], preferred_element_type=F32)
    tm = x.shape[0]
    ones_rows = (lax.broadcasted_iota(I32, (8, tm), 0) == 0).astype(BF16)
    v = jnp.dot(ckvn, wuv_ref[...], preferred_element_type=F32)
    for p in range(N_HEADS // 2):
        vbt_ref[0, p, 0:LANES, :] = jnp.transpose(v[:, p * LANES:(p + 1) * LANES]).astype(BF16)
        vbt_ref[0, p, LANES:LANES + 8, :] = ones_rows
    sub = vvt_ref.shape[3]
    for j in range(tm // sub):
        vals = pf[j * sub:(j + 1) * sub, F_DSV:F_DSV + LANES]
        vvt_ref[0, j, 0:HEAD_DIM, :] = jnp.transpose(vals)[0:HEAD_DIM].astype(BF16)
        vvt_ref[0, j, HEAD_DIM:V_ROWS, :] = (lax.broadcasted_iota(I32, (8, sub), 0) == 0).astype(BF16)
    kpe = pf[:, F_KPE:F_KPE + LANES]
    cgk = cos * gk_ref[...]
    rot = pf[:, F_KPER:F_KPER + LANES] * (sin * gkr_ref[...])
    for h in range(N_HEADS):
        sl = slice(h * LANES, (h + 1) * LANES)
        kf = kn[:, sl] + kpe
        kb_ref[:, sl] = ((kf * cgk + rot) * _rms(kf, MLA_QK)).astype(BF16)

    for h in range(N_HEADS):
        sl = slice(h * LANES, (h + 1) * LANES)
        xq = pf[:, F_DSQ + h * LANES:F_DSQ + (h + 1) * LANES]
        dq_ref[:, sl] = (xq * _rms(xq, HEAD_DIM) * gdq_ref[:, sl]).astype(BF16)
    xk = pf[:, F_DSK:F_DSK + LANES]
    dkk_ref[...] = (xk * _rms(xk, 2 * HEAD_DIM) * gdk_ref[...]).astype(BF16)


def _prologue(x2, cos_t, sin_t, lw, tm, sparse_tc):
    m, d = x2.shape
    assert tm % sparse_tc == 0

    def whole(a):
        return pl.BlockSpec(a.shape, lambda i: (0,) * a.ndim)

    weights = [lw[k] for k in ("wuq", "wuqr", "wuk", "wuv", "qng", "kvng", "gq", "gqr", "gk", "gkr", "gdq", "gdk")]
    row = lambda width: pl.BlockSpec((tm, width), lambda i: (i, 0))
    return pl.pallas_call(
        _prologue_kernel,
        grid=(m // tm,),
        in_specs=[row(d), whole(lw["norm_g"]), pl.BlockSpec((d, N_F32), lambda i: (0, 0)), row(LANES), row(LANES)]
        + [whole(w) for w in weights],
        out_specs=[row(1024), row(1024),
                   pl.BlockSpec((1, N_HEADS // 2, LANES + 8, tm), lambda i: (i, 0, 0, 0)),
                   row(1024), row(LANES), row(LANES),
                   pl.BlockSpec((1, tm // sparse_tc, V_ROWS, sparse_tc), lambda i: (i, 0, 0, 0))],
        out_shape=[jax.ShapeDtypeStruct((m, 1024), BF16), jax.ShapeDtypeStruct((m, 1024), BF16),
                   jax.ShapeDtypeStruct((m // tm, N_HEADS // 2, LANES + 8, tm), BF16),
                   jax.ShapeDtypeStruct((m, 1024), BF16),
                   jax.ShapeDtypeStruct((m, LANES), BF16), jax.ShapeDtypeStruct((m, LANES), F32),
                   jax.ShapeDtypeStruct((m // tm, tm // sparse_tc, V_ROWS, sparse_tc), BF16)],
        compiler_params=pltpu.CompilerParams(dimension_semantics=("parallel",), vmem_limit_bytes=VMEM_LIMIT),
        name="prologue",
    )(x2, lw["norm_g"], lw["w_all"], cos_t, sin_t, *weights)


def _sublane_all(x, op):
    for shift in (4, 2, 1):
        x = op(x, pltpu.roll(x, shift, 0))
    return x


def _sb_kernel(q_ref, k_ref, v_ref, o_ref, acc_ref, carry_ref, hl_ref, lb_ref, *, tq):
    t = LANES
    qi = pl.program_id(1)
    jj = lax.broadcasted_iota(I32, (t, 2 * t), 0)
    ss = lax.broadcasted_iota(I32, (t, 2 * t), 1)
    tri_ones = jnp.logical_or(ss >= t, jj > ss).astype(BF16)
    q_idx = qi * tq + lax.broadcasted_iota(I32, (tq, t), 0)
    k_lane = lax.broadcasted_iota(I32, (tq, t), 1)

    acc_ref[...] = jnp.zeros(acc_ref.shape, F32)
    carry_ref[...] = jnp.zeros(carry_ref.shape, F32)

    def cond(c):
        kb, cmax = c
        return jnp.logical_and(kb >= 0, cmax > -SB_UNDERFLOW)

    def body(c):
        kb, _ = c
        start = pl.multiple_of(kb * t, t)
        mask = (start + k_lane) < q_idx
        for h in range(N_HEADS):
            q = q_ref[0, :, h * LANES:(h + 1) * LANES]
            k = k_ref[0, pl.ds(start, t), (h // 2) * LANES:(h // 2 + 1) * LANES]
            z = lax.dot_general(q, k, NT_DIMS, preferred_element_type=F32)
            sp = jnp.maximum(z, 0.0) + jnp.log(1.0 + jnp.exp(-jnp.abs(z)))
            l1m = jnp.where(mask, -sp, 0.0)
            hi = l1m.astype(BF16)
            hl_ref[2 * h * tq:(2 * h + 1) * tq, :] = hi
            hl_ref[(2 * h + 1) * tq:(2 * h + 2) * tq, :] = (l1m - hi.astype(F32)).astype(BF16)
            lb_ref[h * tq:(h + 1) * tq, :] = z - sp
        sums = jnp.dot(hl_ref[...], tri_ones, preferred_element_type=F32)
        for h in range(N_HEADS):
            rows = slice(h * tq, (h + 1) * tq)
            both = sums[2 * h * tq:(2 * h + 1) * tq, :] + sums[(2 * h + 1) * tq:(2 * h + 2) * tq, :]
            carry = carry_ref[rows, :]
            a = jnp.where(mask, jnp.exp(lb_ref[rows, :] + carry + both[:, :t]), 0.0)
            v = v_ref[0, pl.ds(start, t), (h // 2) * LANES:(h // 2 + 1) * LANES]
            acc_ref[rows, :] += jnp.dot(a.astype(BF16), v, preferred_element_type=F32)
            carry_ref[rows, :] = carry + both[:, t:]
        return kb - 1, jnp.max(carry_ref[...])

    lax.while_loop(cond, body, (((qi + 1) * tq) // t - 1, jnp.float32(0.0)))

    lane = lax.broadcasted_iota(I32, (tq, LANES), 1)
    for p in range(N_HEADS // 2):
        even = acc_ref[2 * p * tq:(2 * p + 1) * tq, :]
        odd = acc_ref[(2 * p + 1) * tq:(2 * p + 2) * tq, :]
        o_ref[0, :, p * LANES:(p + 1) * LANES] = jnp.where(lane < HEAD_DIM, even, odd)


def _sb_attention(pb, b, s, tq):
    pb3 = pb.reshape(b, s, N_BF)
    kernel = functools.partial(_sb_kernel, tq=tq)
    return pl.pallas_call(
        kernel,
        grid=(b, s // tq),
        in_specs=[pl.BlockSpec((1, tq, N_HEADS * LANES), lambda bi, i: (bi, i, B_SAQ // (N_HEADS * LANES))),
                  pl.BlockSpec((1, s, BRANCH_WIDTH), lambda bi, i: (bi, 0, B_SAK // BRANCH_WIDTH)),
                  pl.BlockSpec((1, s, BRANCH_WIDTH), lambda bi, i: (bi, 0, B_SAV // BRANCH_WIDTH))],
        out_specs=pl.BlockSpec((1, tq, BRANCH_WIDTH), lambda bi, i: (bi, i, 0)),
        out_shape=jax.ShapeDtypeStruct((b, s, BRANCH_WIDTH), F32),
        scratch_shapes=[pltpu.VMEM((N_HEADS * tq, LANES), F32),
                        pltpu.VMEM((N_HEADS * tq, LANES), F32),
                        pltpu.VMEM((2 * N_HEADS * tq, LANES), BF16),
                        pltpu.VMEM((N_HEADS * tq, LANES), F32)],
        compiler_params=pltpu.CompilerParams(dimension_semantics=("parallel", "arbitrary"),
                                             vmem_limit_bytes=VMEM_LIMIT),
        name="sb_attention",
    )(pb3, pb3, pb3)


def _mla_kernel(q_ref, k_ref, vt_ref, o_ref, qbd_ref, p_ref, m_ref, alpha_ref, acc_ref, *, tq, tc):
    t = LANES
    assert tq == t
    groups = tc // 8
    pair_w = 2 * tq
    qi = pl.program_id(1)
    n_chunks = (qi * tq) // tc + 1
    k_in_chunk = (lax.broadcasted_iota(I32, (groups, 8, tq), 0) * 8
                  + lax.broadcasted_iota(I32, (groups, 8, tq), 1))
    q_lane = qi * tq + lax.broadcasted_iota(I32, (groups, 8, tq), 2)

    zeros = jnp.zeros((tq, t), BF16)
    for p in range(N_HEADS // 2):
        q_even = q_ref[0, :, 2 * p * LANES:(2 * p + 1) * LANES]
        q_odd = q_ref[0, :, (2 * p + 1) * LANES:(2 * p + 2) * LANES]
        qbd_ref[p, 0:tq, :] = jnp.concatenate([q_even, zeros], axis=1)
        qbd_ref[p, tq:2 * tq, :] = jnp.concatenate([zeros, q_odd], axis=1)

    v_rows = acc_ref.shape[0]
    m_ref[...] = jnp.full(m_ref.shape, NEG_BIG, F32)
    acc_ref[...] = jnp.zeros(acc_ref.shape, F32)

    def chunk(c, diagonal, n_keys=tc):
        start = pl.multiple_of(c * tc, tc)
        n_groups = n_keys // 8
        pair_cols = [slice(p * pair_w, (p + 1) * pair_w) for p in range(N_HEADS // 2)]
        s_pairs = [lax.dot_general(k_ref[0, pl.ds(start, n_keys), pair_cols[p]], qbd_ref[p], NT_DIMS,
                                   preferred_element_type=F32) for p in range(N_HEADS // 2)]
        for p in range(N_HEADS // 2):
            pcols = pair_cols[p]
            for i in range(2):
                cols = slice(p * pair_w + i * tq, p * pair_w + (i + 1) * tq)
                sb = s_pairs[p][:, i * tq:(i + 1) * tq].reshape(n_groups, 8, tq)
                if diagonal:
                    sb = jnp.where(start + k_in_chunk[:n_groups] <= q_lane[:n_groups], sb, NEG_BIG)
                m_old = m_ref[:, cols]
                m_new = jnp.maximum(m_old, _sublane_all(jnp.max(sb, axis=0), jnp.maximum))
                prob = jnp.exp(sb - m_new[None])
                alpha = jnp.exp(m_old - m_new)
                p_ref[0:n_keys, cols] = prob.reshape(n_keys, tq).astype(BF16)
                alpha_ref[:, cols] = alpha
                m_ref[:, cols] = m_new
            pv = jnp.dot(vt_ref[0, c, p, :, 0:n_keys], p_ref[0:n_keys, pcols],
                         preferred_element_type=F32)
            scaled = acc_ref[:, pcols].reshape(v_rows // 8, 8, pair_w) * alpha_ref[:, pcols][None]
            acc_ref[:, pcols] = scaled.reshape(v_rows, pair_w) + pv

    def body(c, carry):
        chunk(c, False)
        return carry

    lax.fori_loop(0, n_chunks - 1, body, 0)

    last = n_chunks - 1
    causal = (qi + 1) * tq - last * tc
    sizes = [n for n in (tc // 4, tc // 2) if n >= 2 * t] + [tc]
    for lo, n_keys in zip([0] + sizes[:-1], sizes):
        @pl.when(jnp.logical_and(causal > lo, causal <= n_keys))
        def _(n_keys=n_keys):
            chunk(last, True, n_keys)

    for p in range(N_HEADS // 2):
        outs = []
        for i in range(2):
            cols = slice(p * pair_w + i * tq, p * pair_w + (i + 1) * tq)
            outs.append(acc_ref[i * HEAD_DIM:(i + 1) * HEAD_DIM, cols] / acc_ref[t:t + 1, cols])
        o_ref[0, :, p * LANES:(p + 1) * LANES] = jnp.transpose(jnp.concatenate(outs, axis=0))


def _mla_attention(qb, kb, vbt, b, s, tq, tc):
    width = N_HEADS * tq
    kernel = functools.partial(_mla_kernel, tq=tq, tc=tc)
    return pl.pallas_call(
        kernel,
        grid=(b, s // tq),
        in_specs=[pl.BlockSpec((1, tq, N_HEADS * LANES), lambda bi, i: (bi, i, 0)),
                  pl.BlockSpec((1, s, N_HEADS * LANES), lambda bi, i: (bi, 0, 0)),
                  pl.BlockSpec((1, s // tc, N_HEADS // 2, LANES + 8, tc), lambda bi, i: (bi, 0, 0, 0, 0))],
        out_specs=pl.BlockSpec((1, tq, BRANCH_WIDTH), lambda bi, i: (bi, i, 0)),
        out_shape=jax.ShapeDtypeStruct((b, s, BRANCH_WIDTH), F32),
        scratch_shapes=[pltpu.VMEM((N_HEADS // 2, 2 * tq, 2 * LANES), BF16),
                        pltpu.VMEM((tc, width), BF16),
                        pltpu.VMEM((8, width), F32),
                        pltpu.VMEM((8, width), F32),
                        pltpu.VMEM((LANES + 8, width), F32)],
        compiler_params=pltpu.CompilerParams(dimension_semantics=("parallel", "arbitrary"),
                                             vmem_limit_bytes=VMEM_LIMIT),
        name="mla_attention",
    )(qb.reshape(b, s, -1), kb.reshape(b, s, -1), vbt)


def _sortable(x):
    bits = pltpu.bitcast(x + 0.0, I32)
    return bits ^ ((bits >> 31) & 0x7FFFFFFF)


def _dsa_kernel(rel_ref, tab_ref, ixq_ref, ixw_ref, ixk_ref, q_ref, kk_ref, vvt_ref, pos_ref, o_ref,
                keys_ref, qm_ref, qs_ref, wt_ref, p_ref, selb_ref, add_ref,
                m_ref, alpha_ref, acc_ref, thr_ref, *, tq, tc, topk, seq):
    t = LANES
    assert tq == t
    groups = tc // 8
    width = N_HEADS * tq
    qi = pl.program_id(1)
    n_chunks = (qi * tq) // tc + 1
    head_cols = [slice(h * tq, (h + 1) * tq) for h in range(N_HEADS)]
    lane = lax.broadcasted_iota(I32, (tq, t), 1)
    q_lane = qi * tq + lax.broadcasted_iota(I32, (groups, 8, tq), 2)
    k_in_chunk = (lax.broadcasted_iota(I32, (groups, 8, tq), 0) * 8
                  + lax.broadcasted_iota(I32, (groups, 8, tq), 1))

    for h in range(N_HEADS):
        grp = ixq_ref[0, :, (h // 4) * LANES:(h // 4 + 1) * LANES]
        lo = (h % 4) * IDX_DIM
        qm_ref[h * tq:(h + 1) * tq, :] = jnp.where((lane >= lo) & (lane < lo + IDX_DIM), grp, jnp.zeros_like(grp))
        qs_ref[h * tq:(h + 1) * tq, :] = q_ref[0, :, h * LANES:(h + 1) * LANES]
    wt_ref[...] = jnp.transpose(ixw_ref[0] * ((IDX_DIM ** -0.5) * (N_HEADS ** -0.5)))

    def chunk3(ref, c):
        start = pl.multiple_of(c * tc, tc)
        return ref[pl.ds(start, tc), :].reshape(groups, 8, tq)

    def score_chunk(c, carry):
        start = pl.multiple_of(c * tc, tc)
        st = lax.dot_general(ixk_ref[0, pl.ds(start, tc), :], qm_ref[...], NT_DIMS, preferred_element_type=F32)
        sc = None
        for h in range(N_HEADS):
            term = jnp.maximum(st[:, head_cols[h]], 0.0) * wt_ref[h:h + 1, :]
            sc = term if sc is None else sc + term
        key = jnp.where(start + k_in_chunk <= q_lane, _sortable(sc).reshape(groups, 8, tq), INT_MIN)
        keys_ref[pl.ds(start, tc), :] = key.reshape(tc, tq)
        return carry

    lax.fori_loop(0, n_chunks, score_chunk, 0)

    def count(pred):
        def one(c, cnt):
            return cnt + jnp.sum(pred(chunk3(keys_ref, c), c * tc).astype(I32), axis=0)

        def two(i, cnt):
            return one(2 * i + 1, one(2 * i, cnt))

        cnt = lax.fori_loop(0, n_chunks // 2, two, jnp.zeros((8, tq), I32))
        cnt = lax.fori_loop(2 * (n_chunks // 2), n_chunks, one, cnt)
        return _sublane_all(cnt, jnp.add)

    @pl.when((qi + 1) * tq <= topk)
    def _():
        thr_ref[0] = jnp.full((8, tq), INT_MIN, I32)
        thr_ref[1] = jnp.full((8, tq), -1, I32)

    @pl.when((qi + 1) * tq > topk)
    def _():
        ans = jnp.zeros((8, tq), I32)
        n_ge = jnp.full((8, tq), seq, I32)
        for bit in range(31, -1, -1):
            cand = ans | np.int32(INT_MIN if bit == 31 else 1 << bit)
            cand_s = cand ^ np.int32(INT_MIN)
            c = count(lambda keys, start, cand_s=cand_s: keys >= cand_s[None])
            take = c >= topk
            ans = jnp.where(take, cand, ans)
            n_ge = jnp.where(take, c, n_ge)
        tau = ans ^ np.int32(INT_MIN)
        thr_ref[0] = tau
        thr_ref[1] = jnp.full((8, tq), seq, I32)

        @pl.when(jnp.max(n_ge) > topk)
        def _():
            need = topk - count(lambda keys, start: keys > tau[None])
            cut = jnp.zeros((8, tq), I32)
            for bit in range(int(math.log2(seq)), -1, -1):
                cand = cut | np.int32(1 << bit)
                c = count(lambda keys, start, cand=cand: (keys == tau[None]) & (start + k_in_chunk < cand[None]))
                cut = jnp.where(c < need, cand, cut)
            thr_ref[1] = cut

    tau = thr_ref[0]
    cut = thr_ref[1]

    v_rows = acc_ref.shape[0]
    m_ref[...] = jnp.full(m_ref.shape, NEG_BIG, F32)
    acc_ref[...] = jnp.zeros(acc_ref.shape, F32)
    pq = pos_ref[0, qi]
    pq_min = jnp.min(pq)
    last_bias = [rel_ref[REL_BUCKETS - 1, h] for h in range(N_HEADS)]

    def softmax_head(h, s_t, addend, bias, n_keys):
        cols = head_cols[h]
        sb = s_t.reshape(n_keys // 8, 8, tq) + addend
        m_old = m_ref[:, cols]
        m_new = jnp.maximum(m_old, _sublane_all(jnp.max(sb, axis=0), jnp.maximum) + bias)
        p = jnp.exp(sb - (m_new - bias)[None])
        alpha = jnp.exp(m_old - m_new)
        p_ref[0:n_keys, cols] = p.reshape(n_keys, tq).astype(BF16)
        alpha_ref[:, cols] = alpha
        m_ref[:, cols] = m_new

    def accumulate(c, half, n_keys):
        cols = slice(half * (width // 2), (half + 1) * (width // 2))
        pv = jnp.dot(vvt_ref[0, c, :, 0:n_keys], p_ref[0:n_keys, cols],
                     preferred_element_type=F32)
        scaled = acc_ref[:, cols].reshape(v_rows // 8, 8, width // 2) * alpha_ref[:, cols][None]
        acc_ref[:, cols] = scaled.reshape(v_rows, width // 2) + pv

    def all_heads(c, addend_of, bias_of, n_keys):
        start = pl.multiple_of(c * tc, tc)
        s_all = lax.dot_general(kk_ref[0, pl.ds(start, n_keys), :], qs_ref[...], NT_DIMS,
                                preferred_element_type=F32)
        for half in range(2):
            for h in range(half * N_HEADS // 2, (half + 1) * N_HEADS // 2):
                softmax_head(h, s_all[:, head_cols[h]], addend_of(h), bias_of(h), n_keys)
            accumulate(c, half, n_keys)

    def attend_chunk(c, n_keys=tc):
        n_groups = n_keys // 8
        start = pl.multiple_of(c * tc, tc)
        keys = keys_ref[pl.ds(start, n_keys), :].reshape(n_groups, 8, tq)
        sel = (keys > tau[None]) | ((keys == tau[None]) & (k_in_chunk[:n_groups] <= (cut - c * tc)[None]))
        selb_ref[0:n_keys, :] = jnp.where(sel, 0.0, NEG_BIG).reshape(n_keys, tq)
        first_tile = c * (tc // t)
        pk_max = pos_ref[0, first_tile]
        for j in range(1, n_keys // t):
            pk_max = jnp.maximum(pk_max, pos_ref[0, first_tile + j])
        chunk_far = pq_min - jnp.max(pk_max) >= FAR_DIST

        @pl.when(chunk_far)
        def _():
            all_heads(c, lambda h: selb_ref[0:n_keys, :].reshape(n_groups, 8, tq), lambda h: last_bias[h], n_keys)

        @pl.when(jnp.logical_not(chunk_far))
        def _():
            for j in range(n_keys // t):
                rows = slice(j * t, (j + 1) * t)
                pk = pos_ref[0, first_tile + j]
                tile_far = jnp.logical_or(pq_min - jnp.max(pk) >= FAR_DIST, c * tc + j * t >= (qi + 1) * tq)

                @pl.when(tile_far)
                def _():
                    for h in range(N_HEADS):
                        add_ref[rows, head_cols[h]] = selb_ref[rows, :] + last_bias[h]

                @pl.when(jnp.logical_not(tile_far))
                def _():
                    pk_col = jnp.transpose(jnp.broadcast_to(pk, (t, t)))
                    dist = jnp.clip(pq - pk_col, 0, t - 1)
                    for h in range(N_HEADS):
                        table = jnp.broadcast_to(tab_ref[h:h + 1, :], (t, t))
                        bias = jnp.take_along_axis(table, dist, axis=1, mode="promise_in_bounds")
                        add_ref[rows, head_cols[h]] = selb_ref[rows, :] + bias

            all_heads(c, lambda h: add_ref[0:n_keys, head_cols[h]].reshape(n_groups, 8, tq), lambda h: 0.0, n_keys)

    def full_chunk(c, carry):
        attend_chunk(c)
        return carry

    lax.fori_loop(0, n_chunks - 1, full_chunk, 0)

    last = n_chunks - 1
    causal = (qi + 1) * tq - last * tc
    sizes = [tc // 4, tc // 2, tc]
    for lo, n_keys in zip([0] + sizes[:-1], sizes):
        @pl.when(jnp.logical_and(causal > lo, causal <= n_keys))
        def _(n_keys=n_keys):
            attend_chunk(last, n_keys)

    for p in range(N_HEADS // 2):
        outs = []
        for h in (2 * p, 2 * p + 1):
            cols = head_cols[h]
            outs.append(acc_ref[0:HEAD_DIM, cols] / acc_ref[HEAD_DIM:HEAD_DIM + 1, cols])
        o_ref[0, :, p * LANES:(p + 1) * LANES] = jnp.transpose(jnp.concatenate(outs, axis=0))


def _dsa_attention(rel_bias, dist_table, pb, ixw, dq, dkk, vvt, pos_rows, b, s, tq, tc):
    topk = min(TOPK_MAX, s // 4)
    pb3 = pb.reshape(b, s, N_BF)
    width = N_HEADS * tq
    kernel = functools.partial(_dsa_kernel, tq=tq, tc=tc, topk=topk, seq=s)
    return pl.pallas_call(
        kernel,
        grid=(b, s // tq),
        in_specs=[pl.BlockSpec(memory_space=pltpu.SMEM),
                  pl.BlockSpec(dist_table.shape, lambda bi, i: (0, 0)),
                  pl.BlockSpec((1, tq, 2 * LANES), lambda bi, i: (bi, i, B_IXQ // 256)),
                  pl.BlockSpec((1, tq, LANES), lambda bi, i: (bi, i, 0)),
                  pl.BlockSpec((1, s, LANES), lambda bi, i: (bi, 0, B_IXK // LANES)),
                  pl.BlockSpec((1, tq, N_HEADS * LANES), lambda bi, i: (bi, i, 0)),
                  pl.BlockSpec((1, s, LANES), lambda bi, i: (bi, 0, 0)),
                  pl.BlockSpec((1, s // tc, V_ROWS, tc), lambda bi, i: (bi, 0, 0, 0)),
                  pl.BlockSpec((1, s // LANES, 1, LANES), lambda bi, i: (bi, 0, 0, 0))],
        out_specs=pl.BlockSpec((1, tq, BRANCH_WIDTH), lambda bi, i: (bi, i, 0)),
        out_shape=jax.ShapeDtypeStruct((b, s, BRANCH_WIDTH), F32),
        scratch_shapes=[pltpu.VMEM((s, tq), I32),
                        pltpu.VMEM((width, LANES), BF16),
                        pltpu.VMEM((width, LANES), BF16),
                        pltpu.VMEM((LANES, tq), F32),
                        pltpu.VMEM((tc, width), BF16),
                        pltpu.VMEM((tc, tq), F32),
                        pltpu.VMEM((tc, width), F32),
                        pltpu.VMEM((8, width), F32),
                        pltpu.VMEM((8, width), F32),
                        pltpu.VMEM((V_ROWS, width), F32),
                        pltpu.VMEM((2, 8, tq), I32)],
        compiler_params=pltpu.CompilerParams(dimension_semantics=("parallel", "arbitrary"),
                                             vmem_limit_bytes=VMEM_LIMIT),
        name="dsa_attention",
    )(rel_bias, dist_table, pb3, ixw.reshape(b, s, LANES), pb3, dq.reshape(b, s, -1), dkk.reshape(b, s, -1), vvt, pos_rows)


def _merge_kernel(x_ref, ya_ref, yb_ref, yc_ref, za_ref, zb_ref, zc_ref, ga_ref, gb_ref, gc_ref,
                  bias_ref, wbr_ref, wout_ref, o_ref):
    merged = None
    branches = ((ya_ref, za_ref, ga_ref), (yb_ref, zb_ref, gb_ref), (yc_ref, zc_ref, gc_ref))
    for n, (y_ref, z_ref, g_ref) in enumerate(branches):
        z = z_ref[...].astype(F32)
        branch = (y_ref[...] * (z * jax.nn.sigmoid(z))).astype(BF16)
        up = jnp.dot(branch, wbr_ref[n], preferred_element_type=F32)
        gate = jax.nn.sigmoid(g_ref[...].astype(F32) + bias_ref[n:n + 1, :])
        merged = gate * up if merged is None else merged + gate * up
    o_ref[...] = x_ref[...] + jnp.dot(merged.astype(BF16), wout_ref[...], preferred_element_type=F32)


def _merge(x2, ya, yb, yc, pb, gate_b, wbr, wout, tm):
    m, d = x2.shape
    row = lambda width: pl.BlockSpec((tm, width), lambda i: (i, 0))
    col = lambda width, off: pl.BlockSpec((tm, width), lambda i, _b=off // width: (i, _b))
    z_specs = [col(BRANCH_WIDTH, B_Z + n * BRANCH_WIDTH) for n in range(N_BRANCH)]
    g_specs = [col(d, B_G + n * d) for n in range(N_BRANCH)]
    return pl.pallas_call(
        _merge_kernel,
        grid=(m // tm,),
        in_specs=[row(d), row(BRANCH_WIDTH), row(BRANCH_WIDTH), row(BRANCH_WIDTH)] + z_specs + g_specs + [
            pl.BlockSpec(gate_b.shape, lambda i: (0, 0)),
            pl.BlockSpec(wbr.shape, lambda i: (0, 0, 0)),
            pl.BlockSpec(wout.shape, lambda i: (0, 0))],
        out_specs=row(d),
        out_shape=jax.ShapeDtypeStruct((m, d), F32),
        compiler_params=pltpu.CompilerParams(dimension_semantics=("parallel",), vmem_limit_bytes=VMEM_LIMIT),
        name="merge",
    )(x2, ya, yb, yc, pb, pb, pb, pb, pb, pb, gate_b, wbr, wout)


def _lane_table(vals_by_lane):
    t = jnp.zeros((LANES,), F32)
    for start, v in vals_by_lane.items():
        t = t.at[start:start + v.shape[0]].set(v)
    return t[None, :]


def _layer_weights(l, w_all, norm_g, mla_q_norm_g, mla_kv_norm_g, mla_w_uq, mla_w_ukv, mla_q_g, mla_k_g,
                   dsa_q_g, dsa_k_g, gate_b, w_branch, w_out):
    uq, ukv = mla_w_uq[l], mla_w_ukv[l]
    half = MLA_ROPE // 2
    q_pieces, qr_pieces, k_pieces, v_pieces = [], [], [], []
    for h in range(N_HEADS):
        o = h * MLA_QK
        q_pieces += [(o, MLA_QK, 1.0), (None, LANES - MLA_QK, 1.0)]
        qr_pieces += [(None, MLA_NOPE, 1.0), (o + MLA_NOPE + half, half, 1.0), (o + MLA_NOPE, half, 1.0),
                      (None, LANES - MLA_QK, 1.0)]
        k_pieces += [(h * 128, MLA_NOPE, 1.0), (None, LANES - MLA_NOPE, 1.0)]
        v_pieces += [(h * 128 + MLA_NOPE, 64, 1.0)]
    scale = MLA_QK ** -0.5
    qg, kg = mla_q_g[l], mla_k_g[l]
    rot_gain = lambda g: jnp.concatenate([g[MLA_NOPE + half:], g[MLA_NOPE:MLA_NOPE + half]])
    dq_g = jnp.concatenate([jnp.concatenate([dsa_q_g[l], jnp.zeros((HEAD_DIM,), F32)]) if h % 2 == 0 else
                            jnp.concatenate([jnp.zeros((HEAD_DIM,), F32), dsa_q_g[l]]) for h in range(N_HEADS)])
    return dict(
        norm_g=norm_g[l][None, :], w_all=w_all[l],
        wuq=_assemble(uq, q_pieces).astype(BF16), wuqr=_assemble(uq, qr_pieces).astype(BF16),
        wuk=_assemble(ukv, k_pieces).astype(BF16), wuv=_assemble(ukv, v_pieces).astype(BF16),
        qng=mla_q_norm_g[l][None, :], kvng=mla_kv_norm_g[l][None, :],
        gq=_lane_table({0: qg * scale}), gqr=_lane_table({MLA_NOPE: rot_gain(qg) * scale}),
        gk=_lane_table({0: kg}), gkr=_lane_table({MLA_NOPE: rot_gain(kg)}),
        gdq=(dq_g * HEAD_DIM ** -0.5)[None, :], gdk=jnp.concatenate([dsa_k_g[l], dsa_k_g[l]])[None, :],
        gate_b=gate_b[l], wbr=w_branch[l].astype(BF16), wout=w_out[l].astype(BF16))


def kernel(x, positions, norm_g, w_in, mla_q_norm_g, mla_kv_norm_g, mla_w_uq, mla_w_ukv, mla_q_g, mla_k_g,
           dsa_q_g, dsa_k_g, rel_bias, gate_b, w_branch, w_out):
    b, s, d = x.shape
    depth = w_in.shape[0]
    m = b * s
    assert d == D_MODEL and w_in.shape[-1] == D_IN and s % 512 == 0 and s & (s - 1) == 0

    half = MLA_ROPE // 2
    inv_freq = ROPE_THETA ** (-(jnp.arange(half, dtype=F32) * 2.0) / MLA_ROPE)
    ang = positions.astype(F32)[..., None] * inv_freq
    cos, sin = jnp.cos(ang).reshape(m, half), jnp.sin(ang).reshape(m, half)
    cos_t = jnp.concatenate([jnp.ones((m, MLA_NOPE), F32), cos, cos, jnp.ones((m, LANES - MLA_QK), F32)], axis=1)
    sin_t = jnp.concatenate([jnp.zeros((m, MLA_NOPE), F32), -sin, sin, jnp.zeros((m, LANES - MLA_QK), F32)], axis=1)

    bf_pieces, f32_pieces = _in_proj_layouts()
    w_in16 = w_in.astype(BF16)
    w_all = _assemble(w_in16, f32_pieces + bf_pieces)

    tm = min(1024, m)
    mla_tc = min(1024, s)
    dsa_tc = 512
    tp = min(512, m)
    pos_rows = positions.reshape(b, s // LANES, 1, LANES)
    dist_table = rel_bias[BUCKET_OF_DIST, :].T
    x2 = x.reshape(m, d)
    for l in range(depth):
        lw = _layer_weights(l, w_all, norm_g, mla_q_norm_g, mla_kv_norm_g, mla_w_uq, mla_w_ukv,
                            mla_q_g, mla_k_g, dsa_q_g, dsa_k_g, gate_b, w_branch, w_out)
        pb = _in_proj(x2, lw["norm_g"], lw["w_all"], N_F32, N_BF, BF16, min(2 * tm, m), 1024)
        qb, kb, vbt, dq, dkk, ixw, vvt = _prologue(x2, cos_t, sin_t, lw, mla_tc, dsa_tc)
        ya = _sb_attention(pb, b, s, 128)
        vbt = vbt.reshape(b, s // mla_tc, N_HEADS // 2, LANES + 8, mla_tc)
        yb = _mla_attention(qb, kb, vbt, b, s, LANES, mla_tc)
        vvt = vvt.reshape(b, s // dsa_tc, V_ROWS, dsa_tc)
        yc = _dsa_attention(rel_bias, dist_table, pb, ixw, dq, dkk, vvt, pos_rows, b, s, LANES, dsa_tc)
        x2 = _merge(x2, ya.reshape(m, -1), yb.reshape(m, -1), yc.reshape(m, -1), pb, lw["gate_b"],
                    lw["wbr"], lw["wout"], tp)
    return x2.reshape(b, s, d)
```

```python
import functools
import math

import numpy as np
import jax
import jax.numpy as jnp
from jax import lax
from jax.experimental import pallas as pl
from jax.experimental.pallas import tpu as pltpu

F32 = jnp.float32
BF16 = jnp.bfloat16
I32 = jnp.int32

LANES = 128
EPS = 1e-6
D_MODEL = 1024
N_HEADS = 8
HEAD_DIM = 64
MLA_Q_LORA = 256
MLA_KV_LORA = 128
MLA_NOPE = 64
MLA_ROPE = 32
MLA_QK = MLA_NOPE + MLA_ROPE
ROPE_THETA = 10000.0
IDX_DIM = 32
TOPK_MAX = 256
REL_BUCKETS = 32
REL_MAX_DIST = 128
N_BRANCH = 3
BRANCH_WIDTH = 512

O_SAQ, O_SAK, O_SAV = 0, 512, 1024
O_CQ, O_CKV, O_KPE = 1536, 1792, 1920
O_DSQ, O_DSK, O_DSV = 1952, 2464, 2528
O_IXQ, O_IXK, O_IXW = 2592, 2848, 2880
O_Z, O_G = 2888, 4424
D_IN = 7496

NEG_BIG = -1e30
INT_MIN = -(2 ** 31)
SB_UNDERFLOW = 104.0
VMEM_LIMIT = 48 * 1024 * 1024

NT_DIMS = (((1,), (1,)), ((), ()))
V_ROWS = HEAD_DIM + 8


def _t5_buckets():
    exact = REL_BUCKETS // 2
    n = np.arange(0, 4 * REL_MAX_DIST)
    nf = np.maximum(n, 1).astype(np.float64)
    large = exact + (np.log(nf / exact) / math.log(REL_MAX_DIST / exact) * (REL_BUCKETS - exact)).astype(np.int64)
    bucket = np.where(n < exact, n, np.minimum(large, REL_BUCKETS - 1))
    assert np.all(np.diff(bucket) >= 0) and np.all(bucket[LANES - 1:] == REL_BUCKETS - 1)
    return bucket[:LANES]


BUCKET_OF_DIST = _t5_buckets()
FAR_DIST = int(np.argmax(BUCKET_OF_DIST == REL_BUCKETS - 1))


def _assemble(w, pieces):
    cols = []
    for start, width, scale in pieces:
        if start is None:
            cols.append(jnp.zeros(w.shape[:-1] + (width,), w.dtype))
        else:
            c = w[..., start:start + width]
            cols.append(c if scale == 1.0 else c * scale)
    return jnp.concatenate(cols, axis=-1)


def _head_padded(start, scale=1.0):
    pieces = []
    for h in range(N_HEADS):
        col = (start + h * HEAD_DIM, HEAD_DIM, scale)
        pad = (None, HEAD_DIM, 1.0)
        pieces += [col, pad] if h % 2 == 0 else [pad, col]
    return pieces


def _in_proj_layouts():
    bf = (_head_padded(O_SAQ, HEAD_DIM ** -0.5)
          + [(O_SAK, 512, 1.0), (O_SAV, 512, 1.0), (O_IXQ, 256, 1.0)]
          + [(None, 128, 1.0)]
          + [(O_IXK, 32, 1.0)] * 4
          + [(O_Z, 1536, 1.0), (O_G, 3072, 1.0)])
    f32 = (_head_padded(O_DSQ)
           + [(O_CQ, 256, 1.0), (O_CKV, 128, 1.0)]
           + [(None, 64, 1.0), (O_KPE, 32, 1.0), (None, 32, 1.0)]
           + [(None, 64, 1.0), (O_KPE + 16, 16, 1.0), (O_KPE, 16, 1.0), (None, 32, 1.0)]
           + [(O_DSK, 64, 1.0)] * 2
           + [(O_IXW, 8, 1.0), (None, 120, 1.0)]
           + [(O_DSV, 64, 1.0), (None, 64, 1.0)])
    return bf, f32


B_SAQ, B_SAK, B_SAV, B_IXQ, B_IXK, B_Z, B_G, N_BF = 0, 1024, 1536, 2048, 2432, 2560, 4096, 7168
F_DSQ, F_CQ, F_CKV, F_KPE, F_KPER, F_DSK, F_IXW, F_DSV, N_F32 = 0, 1024, 1280, 1408, 1536, 1664, 1792, 1920, 2048


def _in_proj_kernel(x_ref, g_ref, w_ref, o_ref, h_ref):
    @pl.when(pl.program_id(1) == 0)
    def _():
        x = x_ref[...]
        ms = jnp.mean(x * x, axis=-1, keepdims=True)
        h_ref[...] = (x * lax.rsqrt(ms + EPS) * g_ref[...]).astype(BF16)

    o_ref[...] = jnp.dot(h_ref[...], w_ref[...], preferred_element_type=F32).astype(o_ref.dtype)


def _in_proj(x2, g, w, col0, n, out_dtype, tm, tn):
    m, d = x2.shape
    return pl.pallas_call(
        _in_proj_kernel,
        grid=(m // tm, n // tn),
        in_specs=[pl.BlockSpec((tm, d), lambda i, j: (i, 0)),
                  pl.BlockSpec((1, d), lambda i, j: (0, 0)),
                  pl.BlockSpec((d, tn), lambda i, j: (0, col0 // tn + j))],
        out_specs=pl.BlockSpec((tm, tn), lambda i, j: (i, j)),
        out_shape=jax.ShapeDtypeStruct((m, n), out_dtype),
        scratch_shapes=[pltpu.VMEM((tm, d), BF16)],
        compiler_params=pltpu.CompilerParams(dimension_semantics=("parallel", "arbitrary"),
                                             vmem_limit_bytes=VMEM_LIMIT),
        name="in_proj",
    )(x2, g, w)


def _rms(x, width):
    return lax.rsqrt(jnp.sum(x * x, axis=-1, keepdims=True) * (1.0 / width) + EPS)


def _prologue_kernel(x_ref, ng_ref, wf_ref, cos_ref, sin_ref,
                     wuq_ref, wuqr_ref, wuk_ref, wuv_ref,
                     qng_ref, kvng_ref, gq_ref, gqr_ref, gk_ref, gkr_ref, gdq_ref, gdk_ref,
                     qb_ref, kb_ref, vbt_ref, dq_ref, dkk_ref, ixw_ref, vvt_ref):
    x = x_ref[...]
    hidden = (x * lax.rsqrt(jnp.mean(x * x, axis=-1, keepdims=True) + EPS) * ng_ref[...]).astype(BF16)
    pf = jnp.dot(hidden, wf_ref[...], preferred_element_type=F32)
    ixw_ref[...] = pf[:, F_IXW:F_IXW + LANES]
    cos = cos_ref[...]
    sin = sin_ref[...]

    cq = pf[:, F_CQ:F_CQ + MLA_Q_LORA]
    cqn = (cq * _rms(cq, MLA_Q_LORA) * qng_ref[...]).astype(BF16)
    q = jnp.dot(cqn, wuq_ref[...], preferred_element_type=F32)
    qr = jnp.dot(cqn, wuqr_ref[...], preferred_element_type=F32)
    cg = cos * gq_ref[...]
    sg = sin * gqr_ref[...]
    for h in range(N_HEADS):
        sl = slice(h * LANES, (h + 1) * LANES)
        qh = q[:, sl]
        qb_ref[:, sl] = ((qh * cg + qr[:, sl] * sg) * _rms(qh, MLA_QK)).astype(BF16)

    ckv = pf[:, F_CKV:F_CKV + MLA_KV_LORA]
    ckvn = (ckv * _rms(ckv, MLA_KV_LORA) * kvng_ref[...]).astype(BF16)
    kn = jnp.dot(ckvn, wuk_ref[...], preferred_element_type=F32)
    tm = x.shape[0]
    ones_rows = (lax.broadcasted_iota(I32, (8, tm), 0) == 0).astype(BF16)
    v = jnp.dot(ckvn, wuv_ref[...], preferred_element_type=F32)
    for p in range(N_HEADS // 2):
        vbt_ref[0, p, 0:LANES, :] = jnp.transpose(v[:, p * LANES:(p + 1) * LANES]).astype(BF16)
        vbt_ref[0, p, LANES:LANES + 8, :] = ones_rows
    sub = vvt_ref.shape[3]
    for j in range(tm // sub):
        vals = pf[j * sub:(j + 1) * sub, F_DSV:F_DSV + LANES]
        vvt_ref[0, j, 0:HEAD_DIM, :] = jnp.transpose(vals)[0:HEAD_DIM].astype(BF16)
        vvt_ref[0, j, HEAD_DIM:V_ROWS, :] = (lax.broadcasted_iota(I32, (8, sub), 0) == 0).astype(BF16)
    kpe = pf[:, F_KPE:F_KPE + LANES]
    cgk = cos * gk_ref[...]
    rot = pf[:, F_KPER:F_KPER + LANES] * (sin * gkr_ref[...])
    for h in range(N_HEADS):
        sl = slice(h * LANES, (h + 1) * LANES)
        kf = kn[:, sl] + kpe
        kb_ref[:, sl] = ((kf * cgk + rot) * _rms(kf, MLA_QK)).astype(BF16)

    for h in range(N_HEADS):
        sl = slice(h * LANES, (h + 1) * LANES)
        xq = pf[:, F_DSQ + h * LANES:F_DSQ + (h + 1) * LANES]
        dq_ref[:, sl] = (xq * _rms(xq, HEAD_DIM) * gdq_ref[:, sl]).astype(BF16)
    xk = pf[:, F_DSK:F_DSK + LANES]
    dkk_ref[...] = (xk * _rms(xk, 2 * HEAD_DIM) * gdk_ref[...]).astype(BF16)


def _prologue(x2, cos_t, sin_t, lw, tm, sparse_tc):
    m, d = x2.shape
    assert tm % sparse_tc == 0

    def whole(a):
        return pl.BlockSpec(a.shape, lambda i: (0,) * a.ndim)

    weights = [lw[k] for k in ("wuq", "wuqr", "wuk", "wuv", "qng", "kvng", "gq", "gqr", "gk", "gkr", "gdq", "gdk")]
    row = lambda width: pl.BlockSpec((tm, width), lambda i: (i, 0))
    return pl.pallas_call(
        _prologue_kernel,
        grid=(m // tm,),
        in_specs=[row(d), whole(lw["norm_g"]), pl.BlockSpec((d, N_F32), lambda i: (0, 0)), row(LANES), row(LANES)]
        + [whole(w) for w in weights],
        out_specs=[row(1024), row(1024),
                   pl.BlockSpec((1, N_HEADS // 2, LANES + 8, tm), lambda i: (i, 0, 0, 0)),
                   row(1024), row(LANES), row(LANES),
                   pl.BlockSpec((1, tm // sparse_tc, V_ROWS, sparse_tc), lambda i: (i, 0, 0, 0))],
        out_shape=[jax.ShapeDtypeStruct((m, 1024), BF16), jax.ShapeDtypeStruct((m, 1024), BF16),
                   jax.ShapeDtypeStruct((m // tm, N_HEADS // 2, LANES + 8, tm), BF16),
                   jax.ShapeDtypeStruct((m, 1024), BF16),
                   jax.ShapeDtypeStruct((m, LANES), BF16), jax.ShapeDtypeStruct((m, LANES), F32),
                   jax.ShapeDtypeStruct((m // tm, tm // sparse_tc, V_ROWS, sparse_tc), BF16)],
        compiler_params=pltpu.CompilerParams(dimension_semantics=("parallel",), vmem_limit_bytes=VMEM_LIMIT),
        name="prologue",
    )(x2, lw["norm_g"], lw["w_all"], cos_t, sin_t, *weights)


def _sublane_all(x, op):
    for shift in (4, 2, 1):
        x = op(x, pltpu.roll(x, shift, 0))
    return x


def _sb_kernel(q_ref, k_ref, v_ref, o_ref, acc_ref, carry_ref, hl_ref, lb_ref, *, tq):
    t = LANES
    qi = pl.program_id(1)
    jj = lax.broadcasted_iota(I32, (t, 2 * t), 0)
    ss = lax.broadcasted_iota(I32, (t, 2 * t), 1)
    tri_ones = jnp.logical_or(ss >= t, jj > ss).astype(BF16)
    q_idx = qi * tq + lax.broadcasted_iota(I32, (tq, t), 0)
    k_lane = lax.broadcasted_iota(I32, (tq, t), 1)

    acc_ref[...] = jnp.zeros(acc_ref.shape, F32)
    carry_ref[...] = jnp.zeros(carry_ref.shape, F32)

    def cond(c):
        kb, cmax = c
        return jnp.logical_and(kb >= 0, cmax > -SB_UNDERFLOW)

    def body(c):
        kb, _ = c
        start = pl.multiple_of(kb * t, t)
        mask = (start + k_lane) < q_idx
        for h in range(N_HEADS):
            q = q_ref[0, :, h * LANES:(h + 1) * LANES]
            k = k_ref[0, pl.ds(start, t), (h // 2) * LANES:(h // 2 + 1) * LANES]
            z = lax.dot_general(q, k, NT_DIMS, preferred_element_type=F32)
            sp = jnp.maximum(z, 0.0) + jnp.log(1.0 + jnp.exp(-jnp.abs(z)))
            l1m = jnp.where(mask, -sp, 0.0)
            hi = l1m.astype(BF16)
            hl_ref[2 * h * tq:(2 * h + 1) * tq, :] = hi
            hl_ref[(2 * h + 1) * tq:(2 * h + 2) * tq, :] = (l1m - hi.astype(F32)).astype(BF16)
            lb_ref[h * tq:(h + 1) * tq, :] = z - sp
        sums = jnp.dot(hl_ref[...], tri_ones, preferred_element_type=F32)
        for h in range(N_HEADS):
            rows = slice(h * tq, (h + 1) * tq)
            both = sums[2 * h * tq:(2 * h + 1) * tq, :] + sums[(2 * h + 1) * tq:(2 * h + 2) * tq, :]
            carry = carry_ref[rows, :]
            a = jnp.where(mask, jnp.exp(lb_ref[rows, :] + carry + both[:, :t]), 0.0)
            v = v_ref[0, pl.ds(start, t), (h // 2) * LANES:(h // 2 + 1) * LANES]
            acc_ref[rows, :] += jnp.dot(a.astype(BF16), v, preferred_element_type=F32)
            carry_ref[rows, :] = carry + both[:, t:]
        return kb - 1, jnp.max(carry_ref[...])

    lax.while_loop(cond, body, (((qi + 1) * tq) // t - 1, jnp.float32(0.0)))

    lane = lax.broadcasted_iota(I32, (tq, LANES), 1)
    for p in range(N_HEADS // 2):
        even = acc_ref[2 * p * tq:(2 * p + 1) * tq, :]
        odd = acc_ref[(2 * p + 1) * tq:(2 * p + 2) * tq, :]
        o_ref[0, :, p * LANES:(p + 1) * LANES] = jnp.where(lane < HEAD_DIM, even, odd)


def _sb_attention(pb, b, s, tq):
    pb3 = pb.reshape(b, s, N_BF)
    kernel = functools.partial(_sb_kernel, tq=tq)
    return pl.pallas_call(
        kernel,
        grid=(b, s // tq),
        in_specs=[pl.BlockSpec((1, tq, N_HEADS * LANES), lambda bi, i: (bi, i, B_SAQ // (N_HEADS * LANES))),
                  pl.BlockSpec((1, s, BRANCH_WIDTH), lambda bi, i: (bi, 0, B_SAK // BRANCH_WIDTH)),
                  pl.BlockSpec((1, s, BRANCH_WIDTH), lambda bi, i: (bi, 0, B_SAV // BRANCH_WIDTH))],
        out_specs=pl.BlockSpec((1, tq, BRANCH_WIDTH), lambda bi, i: (bi, i, 0)),
        out_shape=jax.ShapeDtypeStruct((b, s, BRANCH_WIDTH), F32),
        scratch_shapes=[pltpu.VMEM((N_HEADS * tq, LANES), F32),
                        pltpu.VMEM((N_HEADS * tq, LANES), F32),
                        pltpu.VMEM((2 * N_HEADS * tq, LANES), BF16),
                        pltpu.VMEM((N_HEADS * tq, LANES), F32)],
        compiler_params=pltpu.CompilerParams(dimension_semantics=("parallel", "arbitrary"),
                                             vmem_limit_bytes=VMEM_LIMIT),
        name="sb_attention",
    )(pb3, pb3, pb3)


def _mla_kernel(q_ref, k_ref, vt_ref, o_ref, qbd_ref, p_ref, m_ref, alpha_ref, acc_ref, *, tq, tc):
    t = LANES
    assert tq == t
    groups = tc // 8
    pair_w = 2 * tq
    qi = pl.program_id(1)
    n_chunks = (qi * tq) // tc + 1
    k_in_chunk = (lax.broadcasted_iota(I32, (groups, 8, tq), 0) * 8
                  + lax.broadcasted_iota(I32, (groups, 8, tq), 1))
    q_lane = qi * tq + lax.broadcasted_iota(I32, (groups, 8, tq), 2)

    zeros = jnp.zeros((tq, t), BF16)
    for p in range(N_HEADS // 2):
        q_even = q_ref[0, :, 2 * p * LANES:(2 * p + 1) * LANES]
        q_odd = q_ref[0, :, (2 * p + 1) * LANES:(2 * p + 2) * LANES]
        qbd_ref[p, 0:tq, :] = jnp.concatenate([q_even, zeros], axis=1)
        qbd_ref[p, tq:2 * tq, :] = jnp.concatenate([zeros, q_odd], axis=1)

    v_rows = acc_ref.shape[0]
    m_ref[...] = jnp.full(m_ref.shape, NEG_BIG, F32)
    acc_ref[...] = jnp.zeros(acc_ref.shape, F32)

    def chunk(c, diagonal, n_keys=tc):
        start = pl.multiple_of(c * tc, tc)
        n_groups = n_keys // 8
        pair_cols = [slice(p * pair_w, (p + 1) * pair_w) for p in range(N_HEADS // 2)]
        s_pairs = [lax.dot_general(k_ref[0, pl.ds(start, n_keys), pair_cols[p]], qbd_ref[p], NT_DIMS,
                                   preferred_element_type=F32) for p in range(N_HEADS // 2)]
        for p in range(N_HEADS // 2):
            pcols = pair_cols[p]
            for i in range(2):
                cols = slice(p * pair_w + i * tq, p * pair_w + (i + 1) * tq)
                sb = s_pairs[p][:, i * tq:(i + 1) * tq].reshape(n_groups, 8, tq)
                if diagonal:
                    sb = jnp.where(start + k_in_chunk[:n_groups] <= q_lane[:n_groups], sb, NEG_BIG)
                m_old = m_ref[:, cols]
                m_new = jnp.maximum(m_old, _sublane_all(jnp.max(sb, axis=0), jnp.maximum))
                prob = jnp.exp(sb - m_new[None])
                alpha = jnp.exp(m_old - m_new)
                p_ref[0:n_keys, cols] = prob.reshape(n_keys, tq).astype(BF16)
                alpha_ref[:, cols] = alpha
                m_ref[:, cols] = m_new
            pv = jnp.dot(vt_ref[0, c, p, :, 0:n_keys], p_ref[0:n_keys, pcols],
                         preferred_element_type=F32)
            scaled = acc_ref[:, pcols].reshape(v_rows // 8, 8, pair_w) * alpha_ref[:, pcols][None]
            acc_ref[:, pcols] = scaled.reshape(v_rows, pair_w) + pv

    def body(c, carry):
        chunk(c, False)
        return carry

    lax.fori_loop(0, n_chunks - 1, body, 0)

    last = n_chunks - 1
    causal = (qi + 1) * tq - last * tc
    sizes = [n for n in (tc // 4, tc // 2, 3 * tc // 4) if n >= 2 * t] + [tc]
    for lo, n_keys in zip([0] + sizes[:-1], sizes):
        @pl.when(jnp.logical_and(causal > lo, causal <= n_keys))
        def _(n_keys=n_keys):
            chunk(last, True, n_keys)

    for p in range(N_HEADS // 2):
        outs = []
        for i in range(2):
            cols = slice(p * pair_w + i * tq, p * pair_w + (i + 1) * tq)
            outs.append(acc_ref[i * HEAD_DIM:(i + 1) * HEAD_DIM, cols] / acc_ref[t:t + 1, cols])
        o_ref[0, :, p * LANES:(p + 1) * LANES] = jnp.transpose(jnp.concatenate(outs, axis=0))


def _mla_attention(qb, kb, vbt, b, s, tq, tc):
    width = N_HEADS * tq
    kernel = functools.partial(_mla_kernel, tq=tq, tc=tc)
    return pl.pallas_call(
        kernel,
        grid=(b, s // tq),
        in_specs=[pl.BlockSpec((1, tq, N_HEADS * LANES), lambda bi, i: (bi, i, 0)),
                  pl.BlockSpec((1, s, N_HEADS * LANES), lambda bi, i: (bi, 0, 0)),
                  pl.BlockSpec((1, s // tc, N_HEADS // 2, LANES + 8, tc), lambda bi, i: (bi, 0, 0, 0, 0))],
        out_specs=pl.BlockSpec((1, tq, BRANCH_WIDTH), lambda bi, i: (bi, i, 0)),
        out_shape=jax.ShapeDtypeStruct((b, s, BRANCH_WIDTH), F32),
        scratch_shapes=[pltpu.VMEM((N_HEADS // 2, 2 * tq, 2 * LANES), BF16),
                        pltpu.VMEM((tc, width), BF16),
                        pltpu.VMEM((8, width), F32),
                        pltpu.VMEM((8, width), F32),
                        pltpu.VMEM((LANES + 8, width), F32)],
        compiler_params=pltpu.CompilerParams(dimension_semantics=("parallel", "arbitrary"),
                                             vmem_limit_bytes=VMEM_LIMIT),
        name="mla_attention",
    )(qb.reshape(b, s, -1), kb.reshape(b, s, -1), vbt)


def _sortable(x):
    bits = pltpu.bitcast(x + 0.0, I32)
    return bits ^ ((bits >> 31) & 0x7FFFFFFF)


def _dsa_kernel(rel_ref, tab_ref, ixq_ref, ixw_ref, ixk_ref, q_ref, kk_ref, vvt_ref, pos_ref, o_ref,
                keys_ref, qm_ref, qs_ref, wt_ref, p_ref, selb_ref, add_ref,
                m_ref, alpha_ref, acc_ref, thr_ref, *, tq, tc, topk, seq):
    t = LANES
    assert tq == t
    groups = tc // 8
    width = N_HEADS * tq
    qi = pl.program_id(1)
    n_chunks = (qi * tq) // tc + 1
    head_cols = [slice(h * tq, (h + 1) * tq) for h in range(N_HEADS)]
    lane = lax.broadcasted_iota(I32, (tq, t), 1)
    q_lane = qi * tq + lax.broadcasted_iota(I32, (groups, 8, tq), 2)
    k_in_chunk = (lax.broadcasted_iota(I32, (groups, 8, tq), 0) * 8
                  + lax.broadcasted_iota(I32, (groups, 8, tq), 1))

    for h in range(N_HEADS):
        grp = ixq_ref[0, :, (h // 4) * LANES:(h // 4 + 1) * LANES]
        lo = (h % 4) * IDX_DIM
        qm_ref[h * tq:(h + 1) * tq, :] = jnp.where((lane >= lo) & (lane < lo + IDX_DIM), grp, jnp.zeros_like(grp))
        qs_ref[h * tq:(h + 1) * tq, :] = q_ref[0, :, h * LANES:(h + 1) * LANES]
    wt_ref[...] = jnp.transpose(ixw_ref[0] * ((IDX_DIM ** -0.5) * (N_HEADS ** -0.5)))

    def chunk3(ref, c):
        start = pl.multiple_of(c * tc, tc)
        return ref[pl.ds(start, tc), :].reshape(groups, 8, tq)

    last = n_chunks - 1
    causal = (qi + 1) * tq - last * tc
    sizes = [tc // 4, tc // 2, 3 * tc // 4, tc]
    ladder = list(zip([0] + sizes[:-1], sizes))

    def score_chunk(c, n_keys=tc):
        n_groups = n_keys // 8
        start = pl.multiple_of(c * tc, tc)
        st = lax.dot_general(ixk_ref[0, pl.ds(start, n_keys), :], qm_ref[...], NT_DIMS, preferred_element_type=F32)
        sc = None
        for h in range(N_HEADS):
            term = jnp.maximum(st[:, head_cols[h]], 0.0) * wt_ref[h:h + 1, :]
            sc = term if sc is None else sc + term
        key = jnp.where(start + k_in_chunk[:n_groups] <= q_lane[:n_groups],
                        _sortable(sc).reshape(n_groups, 8, tq), INT_MIN)
        keys_ref[pl.ds(start, n_keys), :] = key.reshape(n_keys, tq)
        if n_keys < tc:
            keys_ref[pl.ds(start + n_keys, tc - n_keys), :] = jnp.full((tc - n_keys, tq), INT_MIN, I32)

    def score_full(c, carry):
        score_chunk(c)
        return carry

    lax.fori_loop(0, last, score_full, 0)
    for lo, n_keys in ladder:
        @pl.when(jnp.logical_and(causal > lo, causal <= n_keys))
        def _(n_keys=n_keys):
            score_chunk(last, n_keys)

    def count(pred):
        def one(c, cnt):
            return cnt + jnp.sum(pred(chunk3(keys_ref, c), c * tc).astype(I32), axis=0)

        def two(i, cnt):
            return one(2 * i + 1, one(2 * i, cnt))

        cnt = lax.fori_loop(0, n_chunks // 2, two, jnp.zeros((8, tq), I32))
        cnt = lax.fori_loop(2 * (n_chunks // 2), n_chunks, one, cnt)
        return _sublane_all(cnt, jnp.add)

    @pl.when((qi + 1) * tq <= topk)
    def _():
        thr_ref[0] = jnp.full((8, tq), INT_MIN, I32)
        thr_ref[1] = jnp.full((8, tq), -1, I32)

    @pl.when((qi + 1) * tq > topk)
    def _():
        ans = jnp.zeros((8, tq), I32)
        n_ge = jnp.full((8, tq), seq, I32)
        for bit in range(31, -1, -1):
            cand = ans | np.int32(INT_MIN if bit == 31 else 1 << bit)
            cand_s = cand ^ np.int32(INT_MIN)
            c = count(lambda keys, start, cand_s=cand_s: keys >= cand_s[None])
            take = c >= topk
            ans = jnp.where(take, cand, ans)
            n_ge = jnp.where(take, c, n_ge)
        tau = ans ^ np.int32(INT_MIN)
        thr_ref[0] = tau
        thr_ref[1] = jnp.full((8, tq), seq, I32)

        @pl.when(jnp.max(n_ge) > topk)
        def _():
            need = topk - count(lambda keys, start: keys > tau[None])
            cut = jnp.zeros((8, tq), I32)
            for bit in range(int(math.log2(seq)), -1, -1):
                cand = cut | np.int32(1 << bit)
                c = count(lambda keys, start, cand=cand: (keys == tau[None]) & (start + k_in_chunk < cand[None]))
                cut = jnp.where(c < need, cand, cut)
            thr_ref[1] = cut

    tau = thr_ref[0]
    cut = thr_ref[1]

    v_rows = acc_ref.shape[0]
    m_ref[...] = jnp.full(m_ref.shape, NEG_BIG, F32)
    acc_ref[...] = jnp.zeros(acc_ref.shape, F32)
    pq = pos_ref[0, qi]
    pq_min = jnp.min(pq)
    last_bias = [rel_ref[REL_BUCKETS - 1, h] for h in range(N_HEADS)]

    def softmax_head(h, s_t, addend, bias, n_keys):
        cols = head_cols[h]
        sb = s_t.reshape(n_keys // 8, 8, tq) + addend
        m_old = m_ref[:, cols]
        m_new = jnp.maximum(m_old, _sublane_all(jnp.max(sb, axis=0), jnp.maximum) + bias)
        p = jnp.exp(sb - (m_new - bias)[None])
        alpha = jnp.exp(m_old - m_new)
        p_ref[0:n_keys, cols] = p.reshape(n_keys, tq).astype(BF16)
        alpha_ref[:, cols] = alpha
        m_ref[:, cols] = m_new

    def accumulate(c, half, n_keys):
        cols = slice(half * (width // 2), (half + 1) * (width // 2))
        pv = jnp.dot(vvt_ref[0, c, :, 0:n_keys], p_ref[0:n_keys, cols],
                     preferred_element_type=F32)
        scaled = acc_ref[:, cols].reshape(v_rows // 8, 8, width // 2) * alpha_ref[:, cols][None]
        acc_ref[:, cols] = scaled.reshape(v_rows, width // 2) + pv

    def all_heads(c, addend_of, bias_of, n_keys):
        start = pl.multiple_of(c * tc, tc)
        s_all = lax.dot_general(kk_ref[0, pl.ds(start, n_keys), :], qs_ref[...], NT_DIMS,
                                preferred_element_type=F32)
        for half in range(2):
            for h in range(half * N_HEADS // 2, (half + 1) * N_HEADS // 2):
                softmax_head(h, s_all[:, head_cols[h]], addend_of(h), bias_of(h), n_keys)
            accumulate(c, half, n_keys)

    def attend_chunk(c, n_keys=tc):
        n_groups = n_keys // 8
        start = pl.multiple_of(c * tc, tc)
        keys = keys_ref[pl.ds(start, n_keys), :].reshape(n_groups, 8, tq)
        sel = (keys > tau[None]) | ((keys == tau[None]) & (k_in_chunk[:n_groups] <= (cut - c * tc)[None]))
        selb_ref[0:n_keys, :] = jnp.where(sel, 0.0, NEG_BIG).reshape(n_keys, tq)
        first_tile = c * (tc // t)
        pk_max = pos_ref[0, first_tile]
        for j in range(1, n_keys // t):
            pk_max = jnp.maximum(pk_max, pos_ref[0, first_tile + j])
        chunk_far = pq_min - jnp.max(pk_max) >= FAR_DIST

        @pl.when(chunk_far)
        def _():
            all_heads(c, lambda h: selb_ref[0:n_keys, :].reshape(n_groups, 8, tq), lambda h: last_bias[h], n_keys)

        @pl.when(jnp.logical_not(chunk_far))
        def _():
            for j in range(n_keys // t):
                rows = slice(j * t, (j + 1) * t)
                pk = pos_ref[0, first_tile + j]
                tile_far = jnp.logical_or(pq_min - jnp.max(pk) >= FAR_DIST, c * tc + j * t >= (qi + 1) * tq)

                @pl.when(tile_far)
                def _():
                    for h in range(N_HEADS):
                        add_ref[rows, head_cols[h]] = selb_ref[rows, :] + last_bias[h]

                @pl.when(jnp.logical_not(tile_far))
                def _():
                    pk_col = jnp.transpose(jnp.broadcast_to(pk, (t, t)))
                    dist = jnp.clip(pq - pk_col, 0, t - 1)
                    for h in range(N_HEADS):
                        table = jnp.broadcast_to(tab_ref[h:h + 1, :], (t, t))
                        bias = jnp.take_along_axis(table, dist, axis=1, mode="promise_in_bounds")
                        add_ref[rows, head_cols[h]] = selb_ref[rows, :] + bias

            all_heads(c, lambda h: add_ref[0:n_keys, head_cols[h]].reshape(n_groups, 8, tq), lambda h: 0.0, n_keys)

    def full_chunk(c, carry):
        attend_chunk(c)
        return carry

    lax.fori_loop(0, last, full_chunk, 0)
    for lo, n_keys in ladder:
        @pl.when(jnp.logical_and(causal > lo, causal <= n_keys))
        def _(n_keys=n_keys):
            attend_chunk(last, n_keys)

    for p in range(N_HEADS // 2):
        outs = []
        for h in (2 * p, 2 * p + 1):
            cols = head_cols[h]
            outs.append(acc_ref[0:HEAD_DIM, cols] / acc_ref[HEAD_DIM:HEAD_DIM + 1, cols])
        o_ref[0, :, p * LANES:(p + 1) * LANES] = jnp.transpose(jnp.concatenate(outs, axis=0))


def _dsa_attention(rel_bias, dist_table, pb, ixw, dq, dkk, vvt, pos_rows, b, s, tq, tc):
    topk = min(TOPK_MAX, s // 4)
    pb3 = pb.reshape(b, s, N_BF)
    width = N_HEADS * tq
    kernel = functools.partial(_dsa_kernel, tq=tq, tc=tc, topk=topk, seq=s)
    return pl.pallas_call(
        kernel,
        grid=(b, s // tq),
        in_specs=[pl.BlockSpec(memory_space=pltpu.SMEM),
                  pl.BlockSpec(dist_table.shape, lambda bi, i: (0, 0)),
                  pl.BlockSpec((1, tq, 2 * LANES), lambda bi, i: (bi, i, B_IXQ // 256)),
                  pl.BlockSpec((1, tq, LANES), lambda bi, i: (bi, i, 0)),
                  pl.BlockSpec((1, s, LANES), lambda bi, i: (bi, 0, B_IXK // LANES)),
                  pl.BlockSpec((1, tq, N_HEADS * LANES), lambda bi, i: (bi, i, 0)),
                  pl.BlockSpec((1, s, LANES), lambda bi, i: (bi, 0, 0)),
                  pl.BlockSpec((1, s // tc, V_ROWS, tc), lambda bi, i: (bi, 0, 0, 0)),
                  pl.BlockSpec((1, s // LANES, 1, LANES), lambda bi, i: (bi, 0, 0, 0))],
        out_specs=pl.BlockSpec((1, tq, BRANCH_WIDTH), lambda bi, i: (bi, i, 0)),
        out_shape=jax.ShapeDtypeStruct((b, s, BRANCH_WIDTH), F32),
        scratch_shapes=[pltpu.VMEM((s, tq), I32),
                        pltpu.VMEM((width, LANES), BF16),
                        pltpu.VMEM((width, LANES), BF16),
                        pltpu.VMEM((LANES, tq), F32),
                        pltpu.VMEM((tc, width), BF16),
                        pltpu.VMEM((tc, tq), F32),
                        pltpu.VMEM((tc, width), F32),
                        pltpu.VMEM((8, width), F32),
                        pltpu.VMEM((8, width), F32),
                        pltpu.VMEM((V_ROWS, width), F32),
                        pltpu.VMEM((2, 8, tq), I32)],
        compiler_params=pltpu.CompilerParams(dimension_semantics=("parallel", "arbitrary"),
                                             vmem_limit_bytes=VMEM_LIMIT),
        name="dsa_attention",
    )(rel_bias, dist_table, pb3, ixw.reshape(b, s, LANES), pb3, dq.reshape(b, s, -1), dkk.reshape(b, s, -1), vvt, pos_rows)


def _merge_kernel(x_ref, ya_ref, yb_ref, yc_ref, za_ref, zb_ref, zc_ref, ga_ref, gb_ref, gc_ref,
                  bias_ref, wbr_ref, wout_ref, o_ref):
    merged = None
    branches = ((ya_ref, za_ref, ga_ref), (yb_ref, zb_ref, gb_ref), (yc_ref, zc_ref, gc_ref))
    for n, (y_ref, z_ref, g_ref) in enumerate(branches):
        z = z_ref[...].astype(F32)
        branch = (y_ref[...] * (z * jax.nn.sigmoid(z))).astype(BF16)
        up = jnp.dot(branch, wbr_ref[n], preferred_element_type=F32)
        gate = jax.nn.sigmoid(g_ref[...].astype(F32) + bias_ref[n:n + 1, :])
        merged = gate * up if merged is None else merged + gate * up
    o_ref[...] = x_ref[...] + jnp.dot(merged.astype(BF16), wout_ref[...], preferred_element_type=F32)


def _merge(x2, ya, yb, yc, pb, gate_b, wbr, wout, tm):
    m, d = x2.shape
    row = lambda width: pl.BlockSpec((tm, width), lambda i: (i, 0))
    col = lambda width, off: pl.BlockSpec((tm, width), lambda i, _b=off // width: (i, _b))
    z_specs = [col(BRANCH_WIDTH, B_Z + n * BRANCH_WIDTH) for n in range(N_BRANCH)]
    g_specs = [col(d, B_G + n * d) for n in range(N_BRANCH)]
    return pl.pallas_call(
        _merge_kernel,
        grid=(m // tm,),
        in_specs=[row(d), row(BRANCH_WIDTH), row(BRANCH_WIDTH), row(BRANCH_WIDTH)] + z_specs + g_specs + [
            pl.BlockSpec(gate_b.shape, lambda i: (0, 0)),
            pl.BlockSpec(wbr.shape, lambda i: (0, 0, 0)),
            pl.BlockSpec(wout.shape, lambda i: (0, 0))],
        out_specs=row(d),
        out_shape=jax.ShapeDtypeStruct((m, d), F32),
        compiler_params=pltpu.CompilerParams(dimension_semantics=("parallel",), vmem_limit_bytes=VMEM_LIMIT),
        name="merge",
    )(x2, ya, yb, yc, pb, pb, pb, pb, pb, pb, gate_b, wbr, wout)


def _lane_table(vals_by_lane):
    t = jnp.zeros((LANES,), F32)
    for start, v in vals_by_lane.items():
        t = t.at[start:start + v.shape[0]].set(v)
    return t[None, :]


def _layer_weights(l, w_all, norm_g, mla_q_norm_g, mla_kv_norm_g, mla_w_uq, mla_w_ukv, mla_q_g, mla_k_g,
                   dsa_q_g, dsa_k_g, gate_b, w_branch, w_out):
    uq, ukv = mla_w_uq[l], mla_w_ukv[l]
    half = MLA_ROPE // 2
    q_pieces, qr_pieces, k_pieces, v_pieces = [], [], [], []
    for h in range(N_HEADS):
        o = h * MLA_QK
        q_pieces += [(o, MLA_QK, 1.0), (None, LANES - MLA_QK, 1.0)]
        qr_pieces += [(None, MLA_NOPE, 1.0), (o + MLA_NOPE + half, half, 1.0), (o + MLA_NOPE, half, 1.0),
                      (None, LANES - MLA_QK, 1.0)]
        k_pieces += [(h * 128, MLA_NOPE, 1.0), (None, LANES - MLA_NOPE, 1.0)]
        v_pieces += [(h * 128 + MLA_NOPE, 64, 1.0)]
    scale = MLA_QK ** -0.5
    qg, kg = mla_q_g[l], mla_k_g[l]
    rot_gain = lambda g: jnp.concatenate([g[MLA_NOPE + half:], g[MLA_NOPE:MLA_NOPE + half]])
    dq_g = jnp.concatenate([jnp.concatenate([dsa_q_g[l], jnp.zeros((HEAD_DIM,), F32)]) if h % 2 == 0 else
                            jnp.concatenate([jnp.zeros((HEAD_DIM,), F32), dsa_q_g[l]]) for h in range(N_HEADS)])
    return dict(
        norm_g=norm_g[l][None, :], w_all=w_all[l],
        wuq=_assemble(uq, q_pieces).astype(BF16), wuqr=_assemble(uq, qr_pieces).astype(BF16),
        wuk=_assemble(ukv, k_pieces).astype(BF16), wuv=_assemble(ukv, v_pieces).astype(BF16),
        qng=mla_q_norm_g[l][None, :], kvng=mla_kv_norm_g[l][None, :],
        gq=_lane_table({0: qg * scale}), gqr=_lane_table({MLA_NOPE: rot_gain(qg) * scale}),
        gk=_lane_table({0: kg}), gkr=_lane_table({MLA_NOPE: rot_gain(kg)}),
        gdq=(dq_g * HEAD_DIM ** -0.5)[None, :], gdk=jnp.concatenate([dsa_k_g[l], dsa_k_g[l]])[None, :],
        gate_b=gate_b[l], wbr=w_branch[l].astype(BF16), wout=w_out[l].astype(BF16))


def kernel(x, positions, norm_g, w_in, mla_q_norm_g, mla_kv_norm_g, mla_w_uq, mla_w_ukv, mla_q_g, mla_k_g,
           dsa_q_g, dsa_k_g, rel_bias, gate_b, w_branch, w_out):
    b, s, d = x.shape
    depth = w_in.shape[0]
    m = b * s
    assert d == D_MODEL and w_in.shape[-1] == D_IN and s % 512 == 0 and s & (s - 1) == 0

    half = MLA_ROPE // 2
    inv_freq = ROPE_THETA ** (-(jnp.arange(half, dtype=F32) * 2.0) / MLA_ROPE)
    ang = positions.astype(F32)[..., None] * inv_freq
    cos, sin = jnp.cos(ang).reshape(m, half), jnp.sin(ang).reshape(m, half)
    cos_t = jnp.concatenate([jnp.ones((m, MLA_NOPE), F32), cos, cos, jnp.ones((m, LANES - MLA_QK), F32)], axis=1)
    sin_t = jnp.concatenate([jnp.zeros((m, MLA_NOPE), F32), -sin, sin, jnp.zeros((m, LANES - MLA_QK), F32)], axis=1)

    bf_pieces, f32_pieces = _in_proj_layouts()
    w_in16 = w_in.astype(BF16)
    w_all = _assemble(w_in16, f32_pieces + bf_pieces)

    tm = min(1024, m)
    mla_tc = min(1024, s)
    dsa_tc = 512
    tp = min(512, m)
    pos_rows = positions.reshape(b, s // LANES, 1, LANES)
    dist_table = rel_bias[BUCKET_OF_DIST, :].T
    x2 = x.reshape(m, d)
    for l in range(depth):
        lw = _layer_weights(l, w_all, norm_g, mla_q_norm_g, mla_kv_norm_g, mla_w_uq, mla_w_ukv,
                            mla_q_g, mla_k_g, dsa_q_g, dsa_k_g, gate_b, w_branch, w_out)
        pb = _in_proj(x2, lw["norm_g"], lw["w_all"], N_F32, N_BF, BF16, min(2 * tm, m), 1024)
        qb, kb, vbt, dq, dkk, ixw, vvt = _prologue(x2, cos_t, sin_t, lw, mla_tc, dsa_tc)
        ya = _sb_attention(pb, b, s, 128)
        vbt = vbt.reshape(b, s // mla_tc, N_HEADS // 2, LANES + 8, mla_tc)
        yb = _mla_attention(qb, kb, vbt, b, s, LANES, mla_tc)
        vvt = vvt.reshape(b, s // dsa_tc, V_ROWS, dsa_tc)
        yc = _dsa_attention(rel_bias, dist_table, pb, ixw, dq, dkk, vvt, pos_rows, b, s, LANES, dsa_tc)
        x2 = _merge(x2, ya.reshape(m, -1), yb.reshape(m, -1), yc.reshape(m, -1), pb, lw["gate_b"],
                    lw["wbr"], lw["wout"], tp)
    return x2.reshape(b, s, d)
```

```python
import functools
import math

import numpy as np
import jax
import jax.numpy as jnp
from jax import lax
from jax.experimental import pallas as pl
from jax.experimental.pallas import tpu as pltpu

F32 = jnp.float32
BF16 = jnp.bfloat16
I32 = jnp.int32

LANES = 128
EPS = 1e-6
D_MODEL = 1024
N_HEADS = 8
HEAD_DIM = 64
MLA_Q_LORA = 256
MLA_KV_LORA = 128
MLA_NOPE = 64
MLA_ROPE = 32
MLA_QK = MLA_NOPE + MLA_ROPE
ROPE_THETA = 10000.0
IDX_DIM = 32
TOPK_MAX = 256
REL_BUCKETS = 32
REL_MAX_DIST = 128
N_BRANCH = 3
BRANCH_WIDTH = 512

O_SAQ, O_SAK, O_SAV = 0, 512, 1024
O_CQ, O_CKV, O_KPE = 1536, 1792, 1920
O_DSQ, O_DSK, O_DSV = 1952, 2464, 2528
O_IXQ, O_IXK, O_IXW = 2592, 2848, 2880
O_Z, O_G = 2888, 4424
D_IN = 7496

NEG_BIG = -1e30
INT_MIN = -(2 ** 31)
SB_UNDERFLOW = 104.0
VMEM_LIMIT = 48 * 1024 * 1024

NT_DIMS = (((1,), (1,)), ((), ()))
V_ROWS = HEAD_DIM + 8


def _t5_buckets():
    exact = REL_BUCKETS // 2
    n = np.arange(0, 4 * REL_MAX_DIST)
    nf = np.maximum(n, 1).astype(np.float64)
    large = exact + (np.log(nf / exact) / math.log(REL_MAX_DIST / exact) * (REL_BUCKETS - exact)).astype(np.int64)
    bucket = np.where(n < exact, n, np.minimum(large, REL_BUCKETS - 1))
    assert np.all(np.diff(bucket) >= 0) and np.all(bucket[LANES - 1:] == REL_BUCKETS - 1)
    return bucket[:LANES]


BUCKET_OF_DIST = _t5_buckets()
FAR_DIST = int(np.argmax(BUCKET_OF_DIST == REL_BUCKETS - 1))


def _assemble(w, pieces):
    cols = []
    for start, width, scale in pieces:
        if start is None:
            cols.append(jnp.zeros(w.shape[:-1] + (width,), w.dtype))
        else:
            c = w[..., start:start + width]
            cols.append(c if scale == 1.0 else c * scale)
    return jnp.concatenate(cols, axis=-1)


def _head_padded(start, scale=1.0):
    pieces = []
    for h in range(N_HEADS):
        col = (start + h * HEAD_DIM, HEAD_DIM, scale)
        pad = (None, HEAD_DIM, 1.0)
        pieces += [col, pad] if h % 2 == 0 else [pad, col]
    return pieces


def _in_proj_layouts():
    bf = (_head_padded(O_SAQ, HEAD_DIM ** -0.5)
          + [(O_SAK, 512, 1.0), (O_SAV, 512, 1.0), (O_IXQ, 256, 1.0)]
          + [(None, 128, 1.0)]
          + [(O_IXK, 32, 1.0)] * 4
          + [(O_Z, 1536, 1.0), (O_G, 3072, 1.0)])
    f32 = (_head_padded(O_DSQ)
           + [(O_CQ, 256, 1.0), (O_CKV, 128, 1.0)]
           + [(None, 64, 1.0), (O_KPE, 32, 1.0), (None, 32, 1.0)]
           + [(None, 64, 1.0), (O_KPE + 16, 16, 1.0), (O_KPE, 16, 1.0), (None, 32, 1.0)]
           + [(O_DSK, 64, 1.0)] * 2
           + [(O_IXW, 8, 1.0), (None, 120, 1.0)]
           + [(O_DSV, 64, 1.0), (None, 64, 1.0)])
    return bf, f32


B_SAQ, B_SAK, B_SAV, B_IXQ, B_IXK, B_Z, B_G, N_BF = 0, 1024, 1536, 2048, 2432, 2560, 4096, 7168
F_DSQ, F_CQ, F_CKV, F_KPE, F_KPER, F_DSK, F_IXW, F_DSV, N_F32 = 0, 1024, 1280, 1408, 1536, 1664, 1792, 1920, 2048


def _in_proj_kernel(x_ref, g_ref, w_ref, o_ref, h_ref):
    @pl.when(pl.program_id(1) == 0)
    def _():
        x = x_ref[...]
        ms = jnp.mean(x * x, axis=-1, keepdims=True)
        h_ref[...] = (x * lax.rsqrt(ms + EPS) * g_ref[...]).astype(BF16)

    o_ref[...] = jnp.dot(h_ref[...], w_ref[...], preferred_element_type=F32).astype(o_ref.dtype)


def _in_proj(x2, g, w, col0, n, out_dtype, tm, tn):
    m, d = x2.shape
    return pl.pallas_call(
        _in_proj_kernel,
        grid=(m // tm, n // tn),
        in_specs=[pl.BlockSpec((tm, d), lambda i, j: (i, 0)),
                  pl.BlockSpec((1, d), lambda i, j: (0, 0)),
                  pl.BlockSpec((d, tn), lambda i, j: (0, col0 // tn + j))],
        out_specs=pl.BlockSpec((tm, tn), lambda i, j: (i, j)),
        out_shape=jax.ShapeDtypeStruct((m, n), out_dtype),
        scratch_shapes=[pltpu.VMEM((tm, d), BF16)],
        compiler_params=pltpu.CompilerParams(dimension_semantics=("parallel", "arbitrary"),
                                             vmem_limit_bytes=VMEM_LIMIT),
        name="in_proj",
    )(x2, g, w)


def _rms(x, width):
    return lax.rsqrt(jnp.sum(x * x, axis=-1, keepdims=True) * (1.0 / width) + EPS)


def _prologue_kernel(x_ref, ng_ref, wf_ref, cos_ref, sin_ref,
                     wuq_ref, wuqr_ref, wuk_ref, wuv_ref,
                     qng_ref, kvng_ref, gq_ref, gqr_ref, gk_ref, gkr_ref, gdq_ref, gdk_ref,
                     qb_ref, kb_ref, vbt_ref, dq_ref, dkk_ref, ixw_ref, vvt_ref):
    x = x_ref[...]
    hidden = (x * lax.rsqrt(jnp.mean(x * x, axis=-1, keepdims=True) + EPS) * ng_ref[...]).astype(BF16)
    pf = jnp.dot(hidden, wf_ref[...], preferred_element_type=F32)
    ixw_ref[...] = pf[:, F_IXW:F_IXW + LANES]
    cos = cos_ref[...]
    sin = sin_ref[...]

    cq = pf[:, F_CQ:F_CQ + MLA_Q_LORA]
    cqn = (cq * _rms(cq, MLA_Q_LORA) * qng_ref[...]).astype(BF16)
    q = jnp.dot(cqn, wuq_ref[...], preferred_element_type=F32)
    qr = jnp.dot(cqn, wuqr_ref[...], preferred_element_type=F32)
    cg = cos * gq_ref[...]
    sg = sin * gqr_ref[...]
    for h in range(N_HEADS):
        sl = slice(h * LANES, (h + 1) * LANES)
        qh = q[:, sl]
        qb_ref[:, sl] = ((qh * cg + qr[:, sl] * sg) * _rms(qh, MLA_QK)).astype(BF16)

    ckv = pf[:, F_CKV:F_CKV + MLA_KV_LORA]
    ckvn = (ckv * _rms(ckv, MLA_KV_LORA) * kvng_ref[...]).astype(BF16)
    kn = jnp.dot(ckvn, wuk_ref[...], preferred_element_type=F32)
    tm = x.shape[0]
    ones_rows = (lax.broadcasted_iota(I32, (8, tm), 0) == 0).astype(BF16)
    v = jnp.dot(ckvn, wuv_ref[...], preferred_element_type=F32)
    for p in range(N_HEADS // 2):
        vbt_ref[0, p, 0:LANES, :] = jnp.transpose(v[:, p * LANES:(p + 1) * LANES]).astype(BF16)
        vbt_ref[0, p, LANES:LANES + 8, :] = ones_rows
    sub = vvt_ref.shape[3]
    for j in range(tm // sub):
        vals = pf[j * sub:(j + 1) * sub, F_DSV:F_DSV + LANES]
        vvt_ref[0, j, 0:HEAD_DIM, :] = jnp.transpose(vals)[0:HEAD_DIM].astype(BF16)
        vvt_ref[0, j, HEAD_DIM:V_ROWS, :] = (lax.broadcasted_iota(I32, (8, sub), 0) == 0).astype(BF16)
    kpe = pf[:, F_KPE:F_KPE + LANES]
    cgk = cos * gk_ref[...]
    rot = pf[:, F_KPER:F_KPER + LANES] * (sin * gkr_ref[...])
    for h in range(N_HEADS):
        sl = slice(h * LANES, (h + 1) * LANES)
        kf = kn[:, sl] + kpe
        kb_ref[:, sl] = ((kf * cgk + rot) * _rms(kf, MLA_QK)).astype(BF16)

    for h in range(N_HEADS):
        sl = slice(h * LANES, (h + 1) * LANES)
        xq = pf[:, F_DSQ + h * LANES:F_DSQ + (h + 1) * LANES]
        dq_ref[:, sl] = (xq * _rms(xq, HEAD_DIM) * gdq_ref[:, sl]).astype(BF16)
    xk = pf[:, F_DSK:F_DSK + LANES]
    dkk_ref[...] = (xk * _rms(xk, 2 * HEAD_DIM) * gdk_ref[...]).astype(BF16)


def _prologue(x2, cos_t, sin_t, lw, tm, sparse_tc):
    m, d = x2.shape
    assert tm % sparse_tc == 0

    def whole(a):
        return pl.BlockSpec(a.shape, lambda i: (0,) * a.ndim)

    weights = [lw[k] for k in ("wuq", "wuqr", "wuk", "wuv", "qng", "kvng", "gq", "gqr", "gk", "gkr", "gdq", "gdk")]
    row = lambda width: pl.BlockSpec((tm, width), lambda i: (i, 0))
    return pl.pallas_call(
        _prologue_kernel,
        grid=(m // tm,),
        in_specs=[row(d), whole(lw["norm_g"]), pl.BlockSpec((d, N_F32), lambda i: (0, 0)), row(LANES), row(LANES)]
        + [whole(w) for w in weights],
        out_specs=[row(1024), row(1024),
                   pl.BlockSpec((1, N_HEADS // 2, LANES + 8, tm), lambda i: (i, 0, 0, 0)),
                   row(1024), row(LANES), row(LANES),
                   pl.BlockSpec((1, tm // sparse_tc, V_ROWS, sparse_tc), lambda i: (i, 0, 0, 0))],
        out_shape=[jax.ShapeDtypeStruct((m, 1024), BF16), jax.ShapeDtypeStruct((m, 1024), BF16),
                   jax.ShapeDtypeStruct((m // tm, N_HEADS // 2, LANES + 8, tm), BF16),
                   jax.ShapeDtypeStruct((m, 1024), BF16),
                   jax.ShapeDtypeStruct((m, LANES), BF16), jax.ShapeDtypeStruct((m, LANES), F32),
                   jax.ShapeDtypeStruct((m // tm, tm // sparse_tc, V_ROWS, sparse_tc), BF16)],
        compiler_params=pltpu.CompilerParams(dimension_semantics=("parallel",), vmem_limit_bytes=VMEM_LIMIT),
        name="prologue",
    )(x2, lw["norm_g"], lw["w_all"], cos_t, sin_t, *weights)


def _sublane_all(x, op):
    for shift in (4, 2, 1):
        x = op(x, pltpu.roll(x, shift, 0))
    return x


def _sb_kernel(q_ref, k_ref, v_ref, o_ref, acc_ref, carry_ref, hl_ref, lb_ref, *, tq):
    t = LANES
    qi = pl.program_id(1)
    jj = lax.broadcasted_iota(I32, (t, 2 * t), 0)
    ss = lax.broadcasted_iota(I32, (t, 2 * t), 1)
    tri_ones = jnp.logical_or(ss >= t, jj > ss).astype(BF16)
    q_idx = qi * tq + lax.broadcasted_iota(I32, (tq, t), 0)
    k_lane = lax.broadcasted_iota(I32, (tq, t), 1)

    acc_ref[...] = jnp.zeros(acc_ref.shape, F32)
    carry_ref[...] = jnp.zeros(carry_ref.shape, F32)

    def cond(c):
        kb, cmax = c
        return jnp.logical_and(kb >= 0, cmax > -SB_UNDERFLOW)

    def body(c):
        kb, _ = c
        start = pl.multiple_of(kb * t, t)
        mask = (start + k_lane) < q_idx
        for h in range(N_HEADS):
            q = q_ref[0, :, h * LANES:(h + 1) * LANES]
            k = k_ref[0, pl.ds(start, t), (h // 2) * LANES:(h // 2 + 1) * LANES]
            z = lax.dot_general(q, k, NT_DIMS, preferred_element_type=F32)
            sp = jnp.maximum(z, 0.0) + jnp.log(1.0 + jnp.exp(-jnp.abs(z)))
            l1m = jnp.where(mask, -sp, 0.0)
            hi = l1m.astype(BF16)
            hl_ref[2 * h * tq:(2 * h + 1) * tq, :] = hi
            hl_ref[(2 * h + 1) * tq:(2 * h + 2) * tq, :] = (l1m - hi.astype(F32)).astype(BF16)
            lb_ref[h * tq:(h + 1) * tq, :] = z - sp
        sums = jnp.dot(hl_ref[...], tri_ones, preferred_element_type=F32)
        for h in range(N_HEADS):
            rows = slice(h * tq, (h + 1) * tq)
            both = sums[2 * h * tq:(2 * h + 1) * tq, :] + sums[(2 * h + 1) * tq:(2 * h + 2) * tq, :]
            carry = carry_ref[rows, :]
            a = jnp.where(mask, jnp.exp(lb_ref[rows, :] + carry + both[:, :t]), 0.0)
            v = v_ref[0, pl.ds(start, t), (h // 2) * LANES:(h // 2 + 1) * LANES]
            acc_ref[rows, :] += jnp.dot(a.astype(BF16), v, preferred_element_type=F32)
            carry_ref[rows, :] = carry + both[:, t:]
        return kb - 1, jnp.max(carry_ref[...])

    lax.while_loop(cond, body, (((qi + 1) * tq) // t - 1, jnp.float32(0.0)))

    lane = lax.broadcasted_iota(I32, (tq, LANES), 1)
    for p in range(N_HEADS // 2):
        even = acc_ref[2 * p * tq:(2 * p + 1) * tq, :]
        odd = acc_ref[(2 * p + 1) * tq:(2 * p + 2) * tq, :]
        o_ref[0, :, p * LANES:(p + 1) * LANES] = jnp.where(lane < HEAD_DIM, even, odd)


def _sb_attention(pb, b, s, tq):
    pb3 = pb.reshape(b, s, N_BF)
    kernel = functools.partial(_sb_kernel, tq=tq)
    return pl.pallas_call(
        kernel,
        grid=(b, s // tq),
        in_specs=[pl.BlockSpec((1, tq, N_HEADS * LANES), lambda bi, i: (bi, i, B_SAQ // (N_HEADS * LANES))),
                  pl.BlockSpec((1, s, BRANCH_WIDTH), lambda bi, i: (bi, 0, B_SAK // BRANCH_WIDTH)),
                  pl.BlockSpec((1, s, BRANCH_WIDTH), lambda bi, i: (bi, 0, B_SAV // BRANCH_WIDTH))],
        out_specs=pl.BlockSpec((1, tq, BRANCH_WIDTH), lambda bi, i: (bi, i, 0)),
        out_shape=jax.ShapeDtypeStruct((b, s, BRANCH_WIDTH), F32),
        scratch_shapes=[pltpu.VMEM((N_HEADS * tq, LANES), F32),
                        pltpu.VMEM((N_HEADS * tq, LANES), F32),
                        pltpu.VMEM((2 * N_HEADS * tq, LANES), BF16),
                        pltpu.VMEM((N_HEADS * tq, LANES), F32)],
        compiler_params=pltpu.CompilerParams(dimension_semantics=("parallel", "arbitrary"),
                                             vmem_limit_bytes=VMEM_LIMIT),
        name="sb_attention",
    )(pb3, pb3, pb3)


def _mla_kernel(q_ref, k_ref, vt_ref, o_ref, qbd_ref, p_ref, m_ref, alpha_ref, acc_ref, *, tq, tc):
    t = LANES
    assert tq == t
    groups = tc // 8
    pair_w = 2 * tq
    qi = pl.program_id(1)
    n_chunks = (qi * tq) // tc + 1
    k_in_chunk = (lax.broadcasted_iota(I32, (groups, 8, tq), 0) * 8
                  + lax.broadcasted_iota(I32, (groups, 8, tq), 1))
    q_lane = qi * tq + lax.broadcasted_iota(I32, (groups, 8, tq), 2)

    zeros = jnp.zeros((tq, t), BF16)
    for p in range(N_HEADS // 2):
        q_even = q_ref[0, :, 2 * p * LANES:(2 * p + 1) * LANES]
        q_odd = q_ref[0, :, (2 * p + 1) * LANES:(2 * p + 2) * LANES]
        qbd_ref[p, 0:tq, :] = jnp.concatenate([q_even, zeros], axis=1)
        qbd_ref[p, tq:2 * tq, :] = jnp.concatenate([zeros, q_odd], axis=1)

    v_rows = acc_ref.shape[0]
    m_ref[...] = jnp.full(m_ref.shape, NEG_BIG, F32)
    acc_ref[...] = jnp.zeros(acc_ref.shape, F32)

    def chunk(c, diagonal, n_keys=tc):
        start = pl.multiple_of(c * tc, tc)
        n_groups = n_keys // 8
        pair_cols = [slice(p * pair_w, (p + 1) * pair_w) for p in range(N_HEADS // 2)]
        s_pairs = [lax.dot_general(k_ref[0, pl.ds(start, n_keys), pair_cols[p]], qbd_ref[p], NT_DIMS,
                                   preferred_element_type=F32) for p in range(N_HEADS // 2)]
        for p in range(N_HEADS // 2):
            pcols = pair_cols[p]
            for i in range(2):
                cols = slice(p * pair_w + i * tq, p * pair_w + (i + 1) * tq)
                sb = s_pairs[p][:, i * tq:(i + 1) * tq].reshape(n_groups, 8, tq)
                if diagonal:
                    sb = jnp.where(start + k_in_chunk[:n_groups] <= q_lane[:n_groups], sb, NEG_BIG)
                m_old = m_ref[:, cols]
                m_new = jnp.maximum(m_old, _sublane_all(jnp.max(sb, axis=0), jnp.maximum))
                prob = jnp.exp(sb - m_new[None])
                alpha = jnp.exp(m_old - m_new)
                p_ref[0:n_keys, cols] = prob.reshape(n_keys, tq).astype(BF16)
                alpha_ref[:, cols] = alpha
                m_ref[:, cols] = m_new
            pv = jnp.dot(vt_ref[0, c, p, :, 0:n_keys], p_ref[0:n_keys, pcols],
                         preferred_element_type=F32)
            scaled = acc_ref[:, pcols].reshape(v_rows // 8, 8, pair_w) * alpha_ref[:, pcols][None]
            acc_ref[:, pcols] = scaled.reshape(v_rows, pair_w) + pv

    def body(c, carry):
        chunk(c, False)
        return carry

    lax.fori_loop(0, n_chunks - 1, body, 0)

    last = n_chunks - 1
    causal = (qi + 1) * tq - last * tc
    sizes = [n for n in (tc // 4, tc // 2, 3 * tc // 4) if n >= 2 * t] + [tc]
    for lo, n_keys in zip([0] + sizes[:-1], sizes):
        @pl.when(jnp.logical_and(causal > lo, causal <= n_keys))
        def _(n_keys=n_keys):
            chunk(last, True, n_keys)

    for p in range(N_HEADS // 2):
        outs = []
        for i in range(2):
            cols = slice(p * pair_w + i * tq, p * pair_w + (i + 1) * tq)
            outs.append(acc_ref[i * HEAD_DIM:(i + 1) * HEAD_DIM, cols] / acc_ref[t:t + 1, cols])
        o_ref[0, :, p * LANES:(p + 1) * LANES] = jnp.transpose(jnp.concatenate(outs, axis=0))


def _mla_attention(qb, kb, vbt, b, s, tq, tc):
    width = N_HEADS * tq
    kernel = functools.partial(_mla_kernel, tq=tq, tc=tc)
    return pl.pallas_call(
        kernel,
        grid=(b, s // tq),
        in_specs=[pl.BlockSpec((1, tq, N_HEADS * LANES), lambda bi, i: (bi, i, 0)),
                  pl.BlockSpec((1, s, N_HEADS * LANES), lambda bi, i: (bi, 0, 0)),
                  pl.BlockSpec((1, s // tc, N_HEADS // 2, LANES + 8, tc), lambda bi, i: (bi, 0, 0, 0, 0))],
        out_specs=pl.BlockSpec((1, tq, BRANCH_WIDTH), lambda bi, i: (bi, i, 0)),
        out_shape=jax.ShapeDtypeStruct((b, s, BRANCH_WIDTH), F32),
        scratch_shapes=[pltpu.VMEM((N_HEADS // 2, 2 * tq, 2 * LANES), BF16),
                        pltpu.VMEM((tc, width), BF16),
                        pltpu.VMEM((8, width), F32),
                        pltpu.VMEM((8, width), F32),
                        pltpu.VMEM((LANES + 8, width), F32)],
        compiler_params=pltpu.CompilerParams(dimension_semantics=("parallel", "arbitrary"),
                                             vmem_limit_bytes=VMEM_LIMIT),
        name="mla_attention",
    )(qb.reshape(b, s, -1), kb.reshape(b, s, -1), vbt)


def _sortable(x):
    bits = pltpu.bitcast(x + 0.0, I32)
    return bits ^ ((bits >> 31) & 0x7FFFFFFF)


def _dsa_kernel(rel_ref, tab_ref, ixq_ref, ixw_ref, ixk_ref, q_ref, kk_ref, vvt_ref, pos_ref, o_ref,
                keys_ref, qm_ref, qs_ref, wt_ref, p_ref, selb_ref, selb2_ref, add_ref,
                m_ref, alpha_ref, acc_ref, thr_ref, *, tq, tc, topk, seq):
    t = LANES
    assert tq == t
    groups = tc // 8
    width = N_HEADS * tq
    qi = pl.program_id(1)
    n_chunks = (qi * tq) // tc + 1
    head_cols = [slice(h * tq, (h + 1) * tq) for h in range(N_HEADS)]
    lane = lax.broadcasted_iota(I32, (tq, t), 1)
    q_lane = qi * tq + lax.broadcasted_iota(I32, (groups, 8, tq), 2)
    k_in_chunk = (lax.broadcasted_iota(I32, (groups, 8, tq), 0) * 8
                  + lax.broadcasted_iota(I32, (groups, 8, tq), 1))

    for h in range(N_HEADS):
        grp = ixq_ref[0, :, (h // 4) * LANES:(h // 4 + 1) * LANES]
        lo = (h % 4) * IDX_DIM
        qm_ref[h * tq:(h + 1) * tq, :] = jnp.where((lane >= lo) & (lane < lo + IDX_DIM), grp, jnp.zeros_like(grp))
        qs_ref[h * tq:(h + 1) * tq, :] = q_ref[0, :, h * LANES:(h + 1) * LANES]
    wt_ref[...] = jnp.transpose(ixw_ref[0] * ((IDX_DIM ** -0.5) * (N_HEADS ** -0.5)))

    def chunk3(ref, c):
        start = pl.multiple_of(c * tc, tc)
        return ref[pl.ds(start, tc), :].reshape(groups, 8, tq)

    last = n_chunks - 1
    causal = (qi + 1) * tq - last * tc
    sizes = [tc // 4, tc // 2, 3 * tc // 4, tc]
    ladder = list(zip([0] + sizes[:-1], sizes))

    def score_chunk(c, n_keys=tc):
        n_groups = n_keys // 8
        start = pl.multiple_of(c * tc, tc)
        st = lax.dot_general(ixk_ref[0, pl.ds(start, n_keys), :], qm_ref[...], NT_DIMS, preferred_element_type=F32)
        sc = None
        for h in range(N_HEADS):
            term = jnp.maximum(st[:, head_cols[h]], 0.0) * wt_ref[h:h + 1, :]
            sc = term if sc is None else sc + term
        key = jnp.where(start + k_in_chunk[:n_groups] <= q_lane[:n_groups],
                        _sortable(sc).reshape(n_groups, 8, tq), INT_MIN)
        keys_ref[pl.ds(start, n_keys), :] = key.reshape(n_keys, tq)
        if n_keys < tc:
            keys_ref[pl.ds(start + n_keys, tc - n_keys), :] = jnp.full((tc - n_keys, tq), INT_MIN, I32)

    def score_full(c, carry):
        score_chunk(c)
        return carry

    lax.fori_loop(0, last, score_full, 0)
    for lo, n_keys in ladder:
        @pl.when(jnp.logical_and(causal > lo, causal <= n_keys))
        def _(n_keys=n_keys):
            score_chunk(last, n_keys)

    def count(pred):
        def one(c, cnt):
            return cnt + jnp.sum(pred(chunk3(keys_ref, c), c * tc).astype(I32), axis=0)

        def two(i, cnt):
            return one(2 * i + 1, one(2 * i, cnt))

        cnt = lax.fori_loop(0, n_chunks // 2, two, jnp.zeros((8, tq), I32))
        cnt = lax.fori_loop(2 * (n_chunks // 2), n_chunks, one, cnt)
        return _sublane_all(cnt, jnp.add)

    @pl.when((qi + 1) * tq <= topk)
    def _():
        thr_ref[0] = jnp.full((8, tq), INT_MIN, I32)
        thr_ref[1] = jnp.full((8, tq), -1, I32)

    @pl.when((qi + 1) * tq > topk)
    def _():
        ans = jnp.zeros((8, tq), I32)
        n_ge = jnp.full((8, tq), seq, I32)
        for bit in range(31, -1, -1):
            cand = ans | np.int32(INT_MIN if bit == 31 else 1 << bit)
            cand_s = cand ^ np.int32(INT_MIN)
            c = count(lambda keys, start, cand_s=cand_s: keys >= cand_s[None])
            take = c >= topk
            ans = jnp.where(take, cand, ans)
            n_ge = jnp.where(take, c, n_ge)
        tau = ans ^ np.int32(INT_MIN)
        thr_ref[0] = tau
        thr_ref[1] = jnp.full((8, tq), seq, I32)

        @pl.when(jnp.max(n_ge) > topk)
        def _():
            need = topk - count(lambda keys, start: keys > tau[None])
            cut = jnp.zeros((8, tq), I32)
            for bit in range(int(math.log2(seq)), -1, -1):
                cand = cut | np.int32(1 << bit)
                c = count(lambda keys, start, cand=cand: (keys == tau[None]) & (start + k_in_chunk < cand[None]))
                cut = jnp.where(c < need, cand, cut)
            thr_ref[1] = cut

    tau = thr_ref[0]
    cut = thr_ref[1]

    v_rows = acc_ref.shape[0]
    m_ref[...] = jnp.full(m_ref.shape, NEG_BIG, F32)
    acc_ref[...] = jnp.zeros(acc_ref.shape, F32)
    pq = pos_ref[0, qi]
    pq_min = jnp.min(pq)
    last_bias = [rel_ref[REL_BUCKETS - 1, h] for h in range(N_HEADS)]

    def softmax_head(h, s_t, addend, bias, n_keys):
        cols = head_cols[h]
        sb = s_t.reshape(n_keys // 8, 8, tq) + addend
        m_old = m_ref[:, cols]
        m_new = jnp.maximum(m_old, _sublane_all(jnp.max(sb, axis=0), jnp.maximum) + bias)
        p = jnp.exp(sb - (m_new - bias)[None])
        alpha = jnp.exp(m_old - m_new)
        p_ref[0:n_keys, cols] = p.reshape(n_keys, tq).astype(BF16)
        alpha_ref[:, cols] = alpha
        m_ref[:, cols] = m_new

    def accumulate(c, half, n_keys):
        cols = slice(half * (width // 2), (half + 1) * (width // 2))
        pv = jnp.dot(vvt_ref[0, c, :, 0:n_keys], p_ref[0:n_keys, cols],
                     preferred_element_type=F32)
        scaled = acc_ref[:, cols].reshape(v_rows // 8, 8, width // 2) * alpha_ref[:, cols][None]
        acc_ref[:, cols] = scaled.reshape(v_rows, width // 2) + pv

    def all_heads(c, addend_of, bias_of, n_keys):
        start = pl.multiple_of(c * tc, tc)
        s_all = lax.dot_general(kk_ref[0, pl.ds(start, n_keys), :], qs_ref[...], NT_DIMS,
                                preferred_element_type=F32)
        for half in range(2):
            for h in range(half * N_HEADS // 2, (half + 1) * N_HEADS // 2):
                softmax_head(h, s_all[:, head_cols[h]], addend_of(h), bias_of(h), n_keys)
            accumulate(c, half, n_keys)

    def attend_chunk(c, n_keys=tc):
        n_groups = n_keys // 8
        start = pl.multiple_of(c * tc, tc)
        keys = keys_ref[pl.ds(start, n_keys), :].reshape(n_groups, 8, tq)
        sel = (keys > tau[None]) | ((keys == tau[None]) & (k_in_chunk[:n_groups] <= (cut - c * tc)[None]))
        selb_ref[0:n_keys, :] = jnp.where(sel, 0.0, NEG_BIG).reshape(n_keys, tq)
        first_tile = c * (tc // t)
        pk_max = pos_ref[0, first_tile]
        for j in range(1, n_keys // t):
            pk_max = jnp.maximum(pk_max, pos_ref[0, first_tile + j])
        chunk_far = pq_min - jnp.max(pk_max) >= FAR_DIST

        @pl.when(chunk_far)
        def _():
            all_heads(c, lambda h: selb_ref[0:n_keys, :].reshape(n_groups, 8, tq), lambda h: last_bias[h], n_keys)

        @pl.when(jnp.logical_not(chunk_far))
        def _():
            for j in range(n_keys // t):
                rows = slice(j * t, (j + 1) * t)
                pk = pos_ref[0, first_tile + j]
                tile_far = jnp.logical_or(pq_min - jnp.max(pk) >= FAR_DIST, c * tc + j * t >= (qi + 1) * tq)

                @pl.when(tile_far)
                def _():
                    for h in range(N_HEADS):
                        add_ref[rows, head_cols[h]] = selb_ref[rows, :] + last_bias[h]

                @pl.when(jnp.logical_not(tile_far))
                def _():
                    pk_col = jnp.transpose(jnp.broadcast_to(pk, (t, t)))
                    dist = jnp.clip(pq - pk_col, 0, t - 1)
                    for h in range(N_HEADS):
                        table = jnp.broadcast_to(tab_ref[h:h + 1, :], (t, t))
                        bias = jnp.take_along_axis(table, dist, axis=1, mode="promise_in_bounds")
                        add_ref[rows, head_cols[h]] = selb_ref[rows, :] + bias

            all_heads(c, lambda h: add_ref[0:n_keys, head_cols[h]].reshape(n_groups, 8, tq), lambda h: 0.0, n_keys)

    def chunk_is_far(c):
        pk_max = pos_ref[0, c * (tc // t)]
        for j in range(1, tc // t):
            pk_max = jnp.maximum(pk_max, pos_ref[0, c * (tc // t) + j])
        return pq_min - jnp.max(pk_max) >= FAR_DIST

    def far_pair(a):
        scores = []
        for c, ref in ((a, selb_ref), (a + 1, selb2_ref)):
            start = pl.multiple_of(c * tc, tc)
            scores.append(lax.dot_general(kk_ref[0, pl.ds(start, tc), :], qs_ref[...], NT_DIMS,
                                          preferred_element_type=F32))
            keys = chunk3(keys_ref, c)
            sel = (keys > tau[None]) | ((keys == tau[None]) & (k_in_chunk <= (cut - c * tc)[None]))
            ref[...] = jnp.where(sel, 0.0, NEG_BIG).reshape(tc, tq)
        for (c, ref), s_all in zip(((a, selb_ref), (a + 1, selb2_ref)), scores):
            for half in range(2):
                for h in range(half * N_HEADS // 2, (half + 1) * N_HEADS // 2):
                    softmax_head(h, s_all[:, head_cols[h]], ref[...].reshape(groups, 8, tq), last_bias[h], tc)
                accumulate(c, half, tc)

    def chunk_pair(i, carry):
        a = 2 * i
        both_far = jnp.logical_and(chunk_is_far(a), chunk_is_far(a + 1))

        @pl.when(both_far)
        def _():
            far_pair(a)

        @pl.when(jnp.logical_not(both_far))
        def _():
            attend_chunk(a)
            attend_chunk(a + 1)

        return carry

    lax.fori_loop(0, last // 2, chunk_pair, 0)

    @pl.when(last % 2 == 1)
    def _():
        attend_chunk(last - 1)
    for lo, n_keys in ladder:
        @pl.when(jnp.logical_and(causal > lo, causal <= n_keys))
        def _(n_keys=n_keys):
            attend_chunk(last, n_keys)

    for p in range(N_HEADS // 2):
        outs = []
        for h in (2 * p, 2 * p + 1):
            cols = head_cols[h]
            outs.append(acc_ref[0:HEAD_DIM, cols] / acc_ref[HEAD_DIM:HEAD_DIM + 1, cols])
        o_ref[0, :, p * LANES:(p + 1) * LANES] = jnp.transpose(jnp.concatenate(outs, axis=0))


def _dsa_attention(rel_bias, dist_table, pb, ixw, dq, dkk, vvt, pos_rows, b, s, tq, tc):
    topk = min(TOPK_MAX, s // 4)
    pb3 = pb.reshape(b, s, N_BF)
    width = N_HEADS * tq
    kernel = functools.partial(_dsa_kernel, tq=tq, tc=tc, topk=topk, seq=s)
    return pl.pallas_call(
        kernel,
        grid=(b, s // tq),
        in_specs=[pl.BlockSpec(memory_space=pltpu.SMEM),
                  pl.BlockSpec(dist_table.shape, lambda bi, i: (0, 0)),
                  pl.BlockSpec((1, tq, 2 * LANES), lambda bi, i: (bi, i, B_IXQ // 256)),
                  pl.BlockSpec((1, tq, LANES), lambda bi, i: (bi, i, 0)),
                  pl.BlockSpec((1, s, LANES), lambda bi, i: (bi, 0, B_IXK // LANES)),
                  pl.BlockSpec((1, tq, N_HEADS * LANES), lambda bi, i: (bi, i, 0)),
                  pl.BlockSpec((1, s, LANES), lambda bi, i: (bi, 0, 0)),
                  pl.BlockSpec((1, s // tc, V_ROWS, tc), lambda bi, i: (bi, 0, 0, 0)),
                  pl.BlockSpec((1, s // LANES, 1, LANES), lambda bi, i: (bi, 0, 0, 0))],
        out_specs=pl.BlockSpec((1, tq, BRANCH_WIDTH), lambda bi, i: (bi, i, 0)),
        out_shape=jax.ShapeDtypeStruct((b, s, BRANCH_WIDTH), F32),
        scratch_shapes=[pltpu.VMEM((s, tq), I32),
                        pltpu.VMEM((width, LANES), BF16),
                        pltpu.VMEM((width, LANES), BF16),
                        pltpu.VMEM((LANES, tq), F32),
                        pltpu.VMEM((tc, width), BF16),
                        pltpu.VMEM((tc, tq), F32),
                        pltpu.VMEM((tc, tq), F32),
                        pltpu.VMEM((tc, width), F32),
                        pltpu.VMEM((8, width), F32),
                        pltpu.VMEM((8, width), F32),
                        pltpu.VMEM((V_ROWS, width), F32),
                        pltpu.VMEM((2, 8, tq), I32)],
        compiler_params=pltpu.CompilerParams(dimension_semantics=("parallel", "arbitrary"),
                                             vmem_limit_bytes=VMEM_LIMIT),
        name="dsa_attention",
    )(rel_bias, dist_table, pb3, ixw.reshape(b, s, LANES), pb3, dq.reshape(b, s, -1), dkk.reshape(b, s, -1), vvt, pos_rows)


def _merge_kernel(x_ref, ya_ref, yb_ref, yc_ref, za_ref, zb_ref, zc_ref, ga_ref, gb_ref, gc_ref,
                  bias_ref, wbr_ref, wout_ref, o_ref):
    merged = None
    branches = ((ya_ref, za_ref, ga_ref), (yb_ref, zb_ref, gb_ref), (yc_ref, zc_ref, gc_ref))
    for n, (y_ref, z_ref, g_ref) in enumerate(branches):
        z = z_ref[...].astype(F32)
        branch = (y_ref[...] * (z * jax.nn.sigmoid(z))).astype(BF16)
        up = jnp.dot(branch, wbr_ref[n], preferred_element_type=F32)
        gate = jax.nn.sigmoid(g_ref[...].astype(F32) + bias_ref[n:n + 1, :])
        merged = gate * up if merged is None else merged + gate * up
    o_ref[...] = x_ref[...] + jnp.dot(merged.astype(BF16), wout_ref[...], preferred_element_type=F32)


def _merge(x2, ya, yb, yc, pb, gate_b, wbr, wout, tm):
    m, d = x2.shape
    row = lambda width: pl.BlockSpec((tm, width), lambda i: (i, 0))
    col = lambda width, off: pl.BlockSpec((tm, width), lambda i, _b=off // width: (i, _b))
    z_specs = [col(BRANCH_WIDTH, B_Z + n * BRANCH_WIDTH) for n in range(N_BRANCH)]
    g_specs = [col(d, B_G + n * d) for n in range(N_BRANCH)]
    return pl.pallas_call(
        _merge_kernel,
        grid=(m // tm,),
        in_specs=[row(d), row(BRANCH_WIDTH), row(BRANCH_WIDTH), row(BRANCH_WIDTH)] + z_specs + g_specs + [
            pl.BlockSpec(gate_b.shape, lambda i: (0, 0)),
            pl.BlockSpec(wbr.shape, lambda i: (0, 0, 0)),
            pl.BlockSpec(wout.shape, lambda i: (0, 0))],
        out_specs=row(d),
        out_shape=jax.ShapeDtypeStruct((m, d), F32),
        compiler_params=pltpu.CompilerParams(dimension_semantics=("parallel",), vmem_limit_bytes=VMEM_LIMIT),
        name="merge",
    )(x2, ya, yb, yc, pb, pb, pb, pb, pb, pb, gate_b, wbr, wout)


def _lane_table(vals_by_lane):
    t = jnp.zeros((LANES,), F32)
    for start, v in vals_by_lane.items():
        t = t.at[start:start + v.shape[0]].set(v)
    return t[None, :]


def _layer_weights(l, w_all, norm_g, mla_q_norm_g, mla_kv_norm_g, mla_w_uq, mla_w_ukv, mla_q_g, mla_k_g,
                   dsa_q_g, dsa_k_g, gate_b, w_branch, w_out):
    uq, ukv = mla_w_uq[l], mla_w_ukv[l]
    half = MLA_ROPE // 2
    q_pieces, qr_pieces, k_pieces, v_pieces = [], [], [], []
    for h in range(N_HEADS):
        o = h * MLA_QK
        q_pieces += [(o, MLA_QK, 1.0), (None, LANES - MLA_QK, 1.0)]
        qr_pieces += [(None, MLA_NOPE, 1.0), (o + MLA_NOPE + half, half, 1.0), (o + MLA_NOPE, half, 1.0),
                      (None, LANES - MLA_QK, 1.0)]
        k_pieces += [(h * 128, MLA_NOPE, 1.0), (None, LANES - MLA_NOPE, 1.0)]
        v_pieces += [(h * 128 + MLA_NOPE, 64, 1.0)]
    scale = MLA_QK ** -0.5
    qg, kg = mla_q_g[l], mla_k_g[l]
    rot_gain = lambda g: jnp.concatenate([g[MLA_NOPE + half:], g[MLA_NOPE:MLA_NOPE + half]])
    dq_g = jnp.concatenate([jnp.concatenate([dsa_q_g[l], jnp.zeros((HEAD_DIM,), F32)]) if h % 2 == 0 else
                            jnp.concatenate([jnp.zeros((HEAD_DIM,), F32), dsa_q_g[l]]) for h in range(N_HEADS)])
    return dict(
        norm_g=norm_g[l][None, :], w_all=w_all[l],
        wuq=_assemble(uq, q_pieces).astype(BF16), wuqr=_assemble(uq, qr_pieces).astype(BF16),
        wuk=_assemble(ukv, k_pieces).astype(BF16), wuv=_assemble(ukv, v_pieces).astype(BF16),
        qng=mla_q_norm_g[l][None, :], kvng=mla_kv_norm_g[l][None, :],
        gq=_lane_table({0: qg * scale}), gqr=_lane_table({MLA_NOPE: rot_gain(qg) * scale}),
        gk=_lane_table({0: kg}), gkr=_lane_table({MLA_NOPE: rot_gain(kg)}),
        gdq=(dq_g * HEAD_DIM ** -0.5)[None, :], gdk=jnp.concatenate([dsa_k_g[l], dsa_k_g[l]])[None, :],
        gate_b=gate_b[l], wbr=w_branch[l].astype(BF16), wout=w_out[l].astype(BF16))


def kernel(x, positions, norm_g, w_in, mla_q_norm_g, mla_kv_norm_g, mla_w_uq, mla_w_ukv, mla_q_g, mla_k_g,
           dsa_q_g, dsa_k_g, rel_bias, gate_b, w_branch, w_out):
    b, s, d = x.shape
    depth = w_in.shape[0]
    m = b * s
    assert d == D_MODEL and w_in.shape[-1] == D_IN and s % 512 == 0 and s & (s - 1) == 0

    half = MLA_ROPE // 2
    inv_freq = ROPE_THETA ** (-(jnp.arange(half, dtype=F32) * 2.0) / MLA_ROPE)
    ang = positions.astype(F32)[..., None] * inv_freq
    cos, sin = jnp.cos(ang).reshape(m, half), jnp.sin(ang).reshape(m, half)
    cos_t = jnp.concatenate([jnp.ones((m, MLA_NOPE), F32), cos, cos, jnp.ones((m, LANES - MLA_QK), F32)], axis=1)
    sin_t = jnp.concatenate([jnp.zeros((m, MLA_NOPE), F32), -sin, sin, jnp.zeros((m, LANES - MLA_QK), F32)], axis=1)

    bf_pieces, f32_pieces = _in_proj_layouts()
    w_in16 = w_in.astype(BF16)
    w_all = _assemble(w_in16, f32_pieces + bf_pieces)

    tm = min(1024, m)
    mla_tc = min(1024, s)
    dsa_tc = 512
    tp = min(512, m)
    pos_rows = positions.reshape(b, s // LANES, 1, LANES)
    dist_table = rel_bias[BUCKET_OF_DIST, :].T
    x2 = x.reshape(m, d)
    for l in range(depth):
        lw = _layer_weights(l, w_all, norm_g, mla_q_norm_g, mla_kv_norm_g, mla_w_uq, mla_w_ukv,
                            mla_q_g, mla_k_g, dsa_q_g, dsa_k_g, gate_b, w_branch, w_out)
        pb = _in_proj(x2, lw["norm_g"], lw["w_all"], N_F32, N_BF, BF16, min(2 * tm, m), 1024)
        qb, kb, vbt, dq, dkk, ixw, vvt = _prologue(x2, cos_t, sin_t, lw, mla_tc, dsa_tc)
        ya = _sb_attention(pb, b, s, 128)
        vbt = vbt.reshape(b, s // mla_tc, N_HEADS // 2, LANES + 8, mla_tc)
        yb = _mla_attention(qb, kb, vbt, b, s, LANES, mla_tc)
        vvt = vvt.reshape(b, s // dsa_tc, V_ROWS, dsa_tc)
        yc = _dsa_attention(rel_bias, dist_table, pb, ixw, dq, dkk, vvt, pos_rows, b, s, LANES, dsa_tc)
        x2 = _merge(x2, ya.reshape(m, -1), yb.reshape(m, -1), yc.reshape(m, -1), pb, lw["gate_b"],
                    lw["wbr"], lw["wout"], tp)
    return x2.reshape(b, s, d)
```

```python
import functools
import math

import numpy as np
import jax
import jax.numpy as jnp
from jax import lax
from jax.experimental import pallas as pl
from jax.experimental.pallas import tpu as pltpu

F32 = jnp.float32
BF16 = jnp.bfloat16
I32 = jnp.int32

LANES = 128
EPS = 1e-6
D_MODEL = 1024
N_HEADS = 8
HEAD_DIM = 64
MLA_Q_LORA = 256
MLA_KV_LORA = 128
MLA_NOPE = 64
MLA_ROPE = 32
MLA_QK = MLA_NOPE + MLA_ROPE
ROPE_THETA = 10000.0
IDX_DIM = 32
TOPK_MAX = 256
REL_BUCKETS = 32
REL_MAX_DIST = 128
N_BRANCH = 3
BRANCH_WIDTH = 512

O_SAQ, O_SAK, O_SAV = 0, 512, 1024
O_CQ, O_CKV, O_KPE = 1536, 1792, 1920
O_DSQ, O_DSK, O_DSV = 1952, 2464, 2528
O_IXQ, O_IXK, O_IXW = 2592, 2848, 2880
O_Z, O_G = 2888, 4424
D_IN = 7496

NEG_BIG = -1e30
INT_MIN = -(2 ** 31)
SB_UNDERFLOW = 104.0
VMEM_LIMIT = 48 * 1024 * 1024

NT_DIMS = (((1,), (1,)), ((), ()))
V_ROWS = HEAD_DIM + 8


def _t5_buckets():
    exact = REL_BUCKETS // 2
    n = np.arange(0, 4 * REL_MAX_DIST)
    nf = np.maximum(n, 1).astype(np.float64)
    large = exact + (np.log(nf / exact) / math.log(REL_MAX_DIST / exact) * (REL_BUCKETS - exact)).astype(np.int64)
    bucket = np.where(n < exact, n, np.minimum(large, REL_BUCKETS - 1))
    assert np.all(np.diff(bucket) >= 0) and np.all(bucket[LANES - 1:] == REL_BUCKETS - 1)
    return bucket[:LANES]


BUCKET_OF_DIST = _t5_buckets()
FAR_DIST = int(np.argmax(BUCKET_OF_DIST == REL_BUCKETS - 1))


def _assemble(w, pieces):
    cols = []
    for start, width, scale in pieces:
        if start is None:
            cols.append(jnp.zeros(w.shape[:-1] + (width,), w.dtype))
        else:
            c = w[..., start:start + width]
            cols.append(c if scale == 1.0 else c * scale)
    return jnp.concatenate(cols, axis=-1)


def _head_padded(start, scale=1.0):
    pieces = []
    for h in range(N_HEADS):
        col = (start + h * HEAD_DIM, HEAD_DIM, scale)
        pad = (None, HEAD_DIM, 1.0)
        pieces += [col, pad] if h % 2 == 0 else [pad, col]
    return pieces


def _in_proj_layouts():
    bf = (_head_padded(O_SAQ, HEAD_DIM ** -0.5)
          + [(O_SAK, 512, 1.0), (O_SAV, 512, 1.0), (O_IXQ, 256, 1.0)]
          + [(None, 128, 1.0)]
          + [(O_IXK, 32, 1.0)] * 4
          + [(O_Z, 1536, 1.0), (O_G, 3072, 1.0)])
    f32 = (_head_padded(O_DSQ)
           + [(O_CQ, 256, 1.0), (O_CKV, 128, 1.0)]
           + [(None, 64, 1.0), (O_KPE, 32, 1.0), (None, 32, 1.0)]
           + [(None, 64, 1.0), (O_KPE + 16, 16, 1.0), (O_KPE, 16, 1.0), (None, 32, 1.0)]
           + [(O_DSK, 64, 1.0)] * 2
           + [(O_IXW, 8, 1.0), (None, 120, 1.0)]
           + [(O_DSV, 64, 1.0), (None, 64, 1.0)])
    return bf, f32


B_SAQ, B_SAK, B_SAV, B_IXQ, B_IXK, B_Z, B_G, N_BF = 0, 1024, 1536, 2048, 2432, 2560, 4096, 7168
F_DSQ, F_CQ, F_CKV, F_KPE, F_KPER, F_DSK, F_IXW, F_DSV, N_F32 = 0, 1024, 1280, 1408, 1536, 1664, 1792, 1920, 2048


def _in_proj_kernel(x_ref, g_ref, w_ref, o_ref, h_ref):
    @pl.when(pl.program_id(1) == 0)
    def _():
        x = x_ref[...]
        ms = jnp.mean(x * x, axis=-1, keepdims=True)
        h_ref[...] = (x * lax.rsqrt(ms + EPS) * g_ref[...]).astype(BF16)

    o_ref[...] = jnp.dot(h_ref[...], w_ref[...], preferred_element_type=F32).astype(o_ref.dtype)


def _in_proj(x2, g, w, col0, n, out_dtype, tm, tn):
    m, d = x2.shape
    return pl.pallas_call(
        _in_proj_kernel,
        grid=(m // tm, n // tn),
        in_specs=[pl.BlockSpec((tm, d), lambda i, j: (i, 0)),
                  pl.BlockSpec((1, d), lambda i, j: (0, 0)),
                  pl.BlockSpec((d, tn), lambda i, j: (0, col0 // tn + j))],
        out_specs=pl.BlockSpec((tm, tn), lambda i, j: (i, j)),
        out_shape=jax.ShapeDtypeStruct((m, n), out_dtype),
        scratch_shapes=[pltpu.VMEM((tm, d), BF16)],
        compiler_params=pltpu.CompilerParams(dimension_semantics=("parallel", "arbitrary"),
                                             vmem_limit_bytes=VMEM_LIMIT),
        name="in_proj",
    )(x2, g, w)


def _rms(x, width):
    return lax.rsqrt(jnp.sum(x * x, axis=-1, keepdims=True) * (1.0 / width) + EPS)


def _prologue_kernel(x_ref, ng_ref, wf_ref, cos_ref, sin_ref,
                     wuq_ref, wuqr_ref, wuk_ref, wuv_ref,
                     qng_ref, kvng_ref, gq_ref, gqr_ref, gk_ref, gkr_ref, gdq_ref, gdk_ref,
                     qb_ref, kb_ref, vbt_ref, dq_ref, dkk_ref, ixw_ref, vvt_ref):
    x = x_ref[...]
    hidden = (x * lax.rsqrt(jnp.mean(x * x, axis=-1, keepdims=True) + EPS) * ng_ref[...]).astype(BF16)
    pf = jnp.dot(hidden, wf_ref[...], preferred_element_type=F32)
    ixw_ref[...] = pf[:, F_IXW:F_IXW + LANES]
    cos = cos_ref[...]
    sin = sin_ref[...]

    cq = pf[:, F_CQ:F_CQ + MLA_Q_LORA]
    cqn = (cq * _rms(cq, MLA_Q_LORA) * qng_ref[...]).astype(BF16)
    q = jnp.dot(cqn, wuq_ref[...], preferred_element_type=F32)
    qr = jnp.dot(cqn, wuqr_ref[...], preferred_element_type=F32)
    cg = cos * gq_ref[...]
    sg = sin * gqr_ref[...]
    for h in range(N_HEADS):
        sl = slice(h * LANES, (h + 1) * LANES)
        qh = q[:, sl]
        qb_ref[:, sl] = ((qh * cg + qr[:, sl] * sg) * _rms(qh, MLA_QK)).astype(BF16)

    ckv = pf[:, F_CKV:F_CKV + MLA_KV_LORA]
    ckvn = (ckv * _rms(ckv, MLA_KV_LORA) * kvng_ref[...]).astype(BF16)
    kn = jnp.dot(ckvn, wuk_ref[...], preferred_element_type=F32)
    tm = x.shape[0]
    ones_rows = (lax.broadcasted_iota(I32, (8, tm), 0) == 0).astype(BF16)
    v = jnp.dot(ckvn, wuv_ref[...], preferred_element_type=F32)
    for p in range(N_HEADS // 2):
        vbt_ref[0, p, 0:LANES, :] = jnp.transpose(v[:, p * LANES:(p + 1) * LANES]).astype(BF16)
        vbt_ref[0, p, LANES:LANES + 8, :] = ones_rows
    sub = vvt_ref.shape[3]
    for j in range(tm // sub):
        vals = pf[j * sub:(j + 1) * sub, F_DSV:F_DSV + LANES]
        vvt_ref[0, j, 0:HEAD_DIM, :] = jnp.transpose(vals)[0:HEAD_DIM].astype(BF16)
        vvt_ref[0, j, HEAD_DIM:V_ROWS, :] = (lax.broadcasted_iota(I32, (8, sub), 0) == 0).astype(BF16)
    kpe = pf[:, F_KPE:F_KPE + LANES]
    cgk = cos * gk_ref[...]
    rot = pf[:, F_KPER:F_KPER + LANES] * (sin * gkr_ref[...])
    for h in range(N_HEADS):
        sl = slice(h * LANES, (h + 1) * LANES)
        kf = kn[:, sl] + kpe
        kb_ref[:, sl] = ((kf * cgk + rot) * _rms(kf, MLA_QK)).astype(BF16)

    for h in range(N_HEADS):
        sl = slice(h * LANES, (h + 1) * LANES)
        xq = pf[:, F_DSQ + h * LANES:F_DSQ + (h + 1) * LANES]
        dq_ref[:, sl] = (xq * _rms(xq, HEAD_DIM) * gdq_ref[:, sl]).astype(BF16)
    xk = pf[:, F_DSK:F_DSK + LANES]
    dkk_ref[...] = (xk * _rms(xk, 2 * HEAD_DIM) * gdk_ref[...]).astype(BF16)


def _prologue(x2, cos_t, sin_t, lw, tm, sparse_tc):
    m, d = x2.shape
    assert tm % sparse_tc == 0

    def whole(a):
        return pl.BlockSpec(a.shape, lambda i: (0,) * a.ndim)

    weights = [lw[k] for k in ("wuq", "wuqr", "wuk", "wuv", "qng", "kvng", "gq", "gqr", "gk", "gkr", "gdq", "gdk")]
    row = lambda width: pl.BlockSpec((tm, width), lambda i: (i, 0))
    return pl.pallas_call(
        _prologue_kernel,
        grid=(m // tm,),
        in_specs=[row(d), whole(lw["norm_g"]), pl.BlockSpec((d, N_F32), lambda i: (0, 0)), row(LANES), row(LANES)]
        + [whole(w) for w in weights],
        out_specs=[row(1024), row(1024),
                   pl.BlockSpec((1, N_HEADS // 2, LANES + 8, tm), lambda i: (i, 0, 0, 0)),
                   row(1024), row(LANES), row(LANES),
                   pl.BlockSpec((1, tm // sparse_tc, V_ROWS, sparse_tc), lambda i: (i, 0, 0, 0))],
        out_shape=[jax.ShapeDtypeStruct((m, 1024), BF16), jax.ShapeDtypeStruct((m, 1024), BF16),
                   jax.ShapeDtypeStruct((m // tm, N_HEADS // 2, LANES + 8, tm), BF16),
                   jax.ShapeDtypeStruct((m, 1024), BF16),
                   jax.ShapeDtypeStruct((m, LANES), BF16), jax.ShapeDtypeStruct((m, LANES), F32),
                   jax.ShapeDtypeStruct((m // tm, tm // sparse_tc, V_ROWS, sparse_tc), BF16)],
        compiler_params=pltpu.CompilerParams(dimension_semantics=("parallel",), vmem_limit_bytes=VMEM_LIMIT),
        name="prologue",
    )(x2, lw["norm_g"], lw["w_all"], cos_t, sin_t, *weights)


def _sublane_all(x, op):
    for shift in (4, 2, 1):
        x = op(x, pltpu.roll(x, shift, 0))
    return x


def _sb_kernel(q_ref, k_ref, v_ref, o_ref, acc_ref, carry_ref, hl_ref, lb_ref, *, tq):
    t = LANES
    qi = pl.program_id(1)
    jj = lax.broadcasted_iota(I32, (t, 2 * t), 0)
    ss = lax.broadcasted_iota(I32, (t, 2 * t), 1)
    tri_ones = jnp.logical_or(ss >= t, jj > ss).astype(BF16)
    q_idx = qi * tq + lax.broadcasted_iota(I32, (tq, t), 0)
    k_lane = lax.broadcasted_iota(I32, (tq, t), 1)

    acc_ref[...] = jnp.zeros(acc_ref.shape, F32)
    carry_ref[...] = jnp.zeros(carry_ref.shape, F32)

    def cond(c):
        kb, cmax = c
        return jnp.logical_and(kb >= 0, cmax > -SB_UNDERFLOW)

    def body(c):
        kb, _ = c
        start = pl.multiple_of(kb * t, t)
        mask = (start + k_lane) < q_idx
        for h in range(N_HEADS):
            q = q_ref[0, :, h * LANES:(h + 1) * LANES]
            k = k_ref[0, pl.ds(start, t), (h // 2) * LANES:(h // 2 + 1) * LANES]
            z = lax.dot_general(q, k, NT_DIMS, preferred_element_type=F32)
            sp = jnp.maximum(z, 0.0) + jnp.log(1.0 + jnp.exp(-jnp.abs(z)))
            l1m = jnp.where(mask, -sp, 0.0)
            hi = l1m.astype(BF16)
            hl_ref[2 * h * tq:(2 * h + 1) * tq, :] = hi
            hl_ref[(2 * h + 1) * tq:(2 * h + 2) * tq, :] = (l1m - hi.astype(F32)).astype(BF16)
            lb_ref[h * tq:(h + 1) * tq, :] = z - sp
        sums = jnp.dot(hl_ref[...], tri_ones, preferred_element_type=F32)
        for h in range(N_HEADS):
            rows = slice(h * tq, (h + 1) * tq)
            both = sums[2 * h * tq:(2 * h + 1) * tq, :] + sums[(2 * h + 1) * tq:(2 * h + 2) * tq, :]
            carry = carry_ref[rows, :]
            a = jnp.where(mask, jnp.exp(lb_ref[rows, :] + carry + both[:, :t]), 0.0)
            v = v_ref[0, pl.ds(start, t), (h // 2) * LANES:(h // 2 + 1) * LANES]
            acc_ref[rows, :] += jnp.dot(a.astype(BF16), v, preferred_element_type=F32)
            carry_ref[rows, :] = carry + both[:, t:]
        return kb - 1, jnp.max(carry_ref[...])

    lax.while_loop(cond, body, (((qi + 1) * tq) // t - 1, jnp.float32(0.0)))

    lane = lax.broadcasted_iota(I32, (tq, LANES), 1)
    for p in range(N_HEADS // 2):
        even = acc_ref[2 * p * tq:(2 * p + 1) * tq, :]
        odd = acc_ref[(2 * p + 1) * tq:(2 * p + 2) * tq, :]
        o_ref[0, :, p * LANES:(p + 1) * LANES] = jnp.where(lane < HEAD_DIM, even, odd)


def _sb_attention(pb, b, s, tq):
    pb3 = pb.reshape(b, s, N_BF)
    kernel = functools.partial(_sb_kernel, tq=tq)
    return pl.pallas_call(
        kernel,
        grid=(b, s // tq),
        in_specs=[pl.BlockSpec((1, tq, N_HEADS * LANES), lambda bi, i: (bi, i, B_SAQ // (N_HEADS * LANES))),
                  pl.BlockSpec((1, s, BRANCH_WIDTH), lambda bi, i: (bi, 0, B_SAK // BRANCH_WIDTH)),
                  pl.BlockSpec((1, s, BRANCH_WIDTH), lambda bi, i: (bi, 0, B_SAV // BRANCH_WIDTH))],
        out_specs=pl.BlockSpec((1, tq, BRANCH_WIDTH), lambda bi, i: (bi, i, 0)),
        out_shape=jax.ShapeDtypeStruct((b, s, BRANCH_WIDTH), F32),
        scratch_shapes=[pltpu.VMEM((N_HEADS * tq, LANES), F32),
                        pltpu.VMEM((N_HEADS * tq, LANES), F32),
                        pltpu.VMEM((2 * N_HEADS * tq, LANES), BF16),
                        pltpu.VMEM((N_HEADS * tq, LANES), F32)],
        compiler_params=pltpu.CompilerParams(dimension_semantics=("parallel", "arbitrary"),
                                             vmem_limit_bytes=VMEM_LIMIT),
        name="sb_attention",
    )(pb3, pb3, pb3)


def _mla_kernel(q_ref, k_ref, vt_ref, o_ref, qbd_ref, p_ref, m_ref, alpha_ref, acc_ref, *, tq, tc):
    t = LANES
    assert tq == t
    groups = tc // 8
    pair_w = 2 * tq
    qi = pl.program_id(1)
    n_chunks = (qi * tq) // tc + 1
    k_in_chunk = (lax.broadcasted_iota(I32, (groups, 8, tq), 0) * 8
                  + lax.broadcasted_iota(I32, (groups, 8, tq), 1))
    q_lane = qi * tq + lax.broadcasted_iota(I32, (groups, 8, tq), 2)

    zeros = jnp.zeros((tq, t), BF16)
    for p in range(N_HEADS // 2):
        q_even = q_ref[0, :, 2 * p * LANES:(2 * p + 1) * LANES]
        q_odd = q_ref[0, :, (2 * p + 1) * LANES:(2 * p + 2) * LANES]
        qbd_ref[p, 0:tq, :] = jnp.concatenate([q_even, zeros], axis=1)
        qbd_ref[p, tq:2 * tq, :] = jnp.concatenate([zeros, q_odd], axis=1)

    v_rows = acc_ref.shape[0]
    m_ref[...] = jnp.full(m_ref.shape, NEG_BIG, F32)
    acc_ref[...] = jnp.zeros(acc_ref.shape, F32)

    def chunk(c, diagonal, n_keys=tc):
        start = pl.multiple_of(c * tc, tc)
        n_groups = n_keys // 8
        pair_cols = [slice(p * pair_w, (p + 1) * pair_w) for p in range(N_HEADS // 2)]
        s_pairs = [lax.dot_general(k_ref[0, pl.ds(start, n_keys), pair_cols[p]], qbd_ref[p], NT_DIMS,
                                   preferred_element_type=F32) for p in range(N_HEADS // 2)]
        for p in range(N_HEADS // 2):
            pcols = pair_cols[p]
            for i in range(2):
                cols = slice(p * pair_w + i * tq, p * pair_w + (i + 1) * tq)
                sb = s_pairs[p][:, i * tq:(i + 1) * tq].reshape(n_groups, 8, tq)
                if diagonal:
                    sb = jnp.where(start + k_in_chunk[:n_groups] <= q_lane[:n_groups], sb, NEG_BIG)
                m_old = m_ref[:, cols]
                m_new = jnp.maximum(m_old, _sublane_all(jnp.max(sb, axis=0), jnp.maximum))
                prob = jnp.exp(sb - m_new[None])
                alpha = jnp.exp(m_old - m_new)
                p_ref[0:n_keys, cols] = prob.reshape(n_keys, tq).astype(BF16)
                alpha_ref[:, cols] = alpha
                m_ref[:, cols] = m_new
            pv = jnp.dot(vt_ref[0, c, p, :, 0:n_keys], p_ref[0:n_keys, pcols],
                         preferred_element_type=F32)
            scaled = acc_ref[:, pcols].reshape(v_rows // 8, 8, pair_w) * alpha_ref[:, pcols][None]
            acc_ref[:, pcols] = scaled.reshape(v_rows, pair_w) + pv

    def body(c, carry):
        chunk(c, False)
        return carry

    lax.fori_loop(0, n_chunks - 1, body, 0)

    last = n_chunks - 1
    causal = (qi + 1) * tq - last * tc
    sizes = [n for n in (tc // 4, tc // 2, 3 * tc // 4) if n >= 2 * t] + [tc]
    for lo, n_keys in zip([0] + sizes[:-1], sizes):
        @pl.when(jnp.logical_and(causal > lo, causal <= n_keys))
        def _(n_keys=n_keys):
            chunk(last, True, n_keys)

    for p in range(N_HEADS // 2):
        outs = []
        for i in range(2):
            cols = slice(p * pair_w + i * tq, p * pair_w + (i + 1) * tq)
            outs.append(acc_ref[i * HEAD_DIM:(i + 1) * HEAD_DIM, cols] / acc_ref[t:t + 1, cols])
        o_ref[0, :, p * LANES:(p + 1) * LANES] = jnp.transpose(jnp.concatenate(outs, axis=0))


def _mla_attention(qb, kb, vbt, b, s, tq, tc):
    width = N_HEADS * tq
    kernel = functools.partial(_mla_kernel, tq=tq, tc=tc)
    return pl.pallas_call(
        kernel,
        grid=(b, s // tq),
        in_specs=[pl.BlockSpec((1, tq, N_HEADS * LANES), lambda bi, i: (bi, i, 0)),
                  pl.BlockSpec((1, s, N_HEADS * LANES), lambda bi, i: (bi, 0, 0)),
                  pl.BlockSpec((1, s // tc, N_HEADS // 2, LANES + 8, tc), lambda bi, i: (bi, 0, 0, 0, 0))],
        out_specs=pl.BlockSpec((1, tq, BRANCH_WIDTH), lambda bi, i: (bi, i, 0)),
        out_shape=jax.ShapeDtypeStruct((b, s, BRANCH_WIDTH), F32),
        scratch_shapes=[pltpu.VMEM((N_HEADS // 2, 2 * tq, 2 * LANES), BF16),
                        pltpu.VMEM((tc, width), BF16),
                        pltpu.VMEM((8, width), F32),
                        pltpu.VMEM((8, width), F32),
                        pltpu.VMEM((LANES + 8, width), F32)],
        compiler_params=pltpu.CompilerParams(dimension_semantics=("parallel", "arbitrary"),
                                             vmem_limit_bytes=VMEM_LIMIT),
        name="mla_attention",
    )(qb.reshape(b, s, -1), kb.reshape(b, s, -1), vbt)


def _sortable(x):
    bits = pltpu.bitcast(x + 0.0, I32)
    return bits ^ ((bits >> 31) & 0x7FFFFFFF)


def _dsa_kernel(rel_ref, tab_ref, ixq_ref, ixw_ref, ixk_ref, q_ref, kk_ref, vvt_ref, pos_ref, o_ref,
                keys_ref, qm_ref, qs_ref, wt_ref, p_ref, selb_ref, selb2_ref, add_ref,
                m_ref, alpha_ref, acc_ref, thr_ref, *, tq, tc, topk, seq):
    t = LANES
    assert tq == t
    groups = tc // 8
    width = N_HEADS * tq
    qi = pl.program_id(1)
    n_chunks = (qi * tq) // tc + 1
    head_cols = [slice(h * tq, (h + 1) * tq) for h in range(N_HEADS)]
    lane = lax.broadcasted_iota(I32, (tq, t), 1)
    q_lane = qi * tq + lax.broadcasted_iota(I32, (groups, 8, tq), 2)
    k_in_chunk = (lax.broadcasted_iota(I32, (groups, 8, tq), 0) * 8
                  + lax.broadcasted_iota(I32, (groups, 8, tq), 1))

    for h in range(N_HEADS):
        grp = ixq_ref[0, :, (h // 4) * LANES:(h // 4 + 1) * LANES]
        lo = (h % 4) * IDX_DIM
        qm_ref[h * tq:(h + 1) * tq, :] = jnp.where((lane >= lo) & (lane < lo + IDX_DIM), grp, jnp.zeros_like(grp))
        qs_ref[h * tq:(h + 1) * tq, :] = q_ref[0, :, h * LANES:(h + 1) * LANES]
    wt_ref[...] = jnp.transpose(ixw_ref[0] * ((IDX_DIM ** -0.5) * (N_HEADS ** -0.5)))

    def chunk3(ref, c):
        start = pl.multiple_of(c * tc, tc)
        return ref[pl.ds(start, tc), :].reshape(groups, 8, tq)

    last = n_chunks - 1
    causal = (qi + 1) * tq - last * tc
    sizes = [tc // 4, tc // 2, 3 * tc // 4, tc]
    ladder = list(zip([0] + sizes[:-1], sizes))

    def indexer_dots(c, n_keys):
        start = pl.multiple_of(c * tc, tc)
        return lax.dot_general(ixk_ref[0, pl.ds(start, n_keys), :], qm_ref[...], NT_DIMS, preferred_element_type=F32)

    def score_chunk(c, n_keys=tc, st=None):
        n_groups = n_keys // 8
        start = pl.multiple_of(c * tc, tc)
        if st is None:
            st = indexer_dots(c, n_keys)
        sc = None
        for h in range(N_HEADS):
            term = jnp.maximum(st[:, head_cols[h]], 0.0) * wt_ref[h:h + 1, :]
            sc = term if sc is None else sc + term
        key = jnp.where(start + k_in_chunk[:n_groups] <= q_lane[:n_groups],
                        _sortable(sc).reshape(n_groups, 8, tq), INT_MIN)
        keys_ref[pl.ds(start, n_keys), :] = key.reshape(n_keys, tq)
        if n_keys < tc:
            keys_ref[pl.ds(start + n_keys, tc - n_keys), :] = jnp.full((tc - n_keys, tq), INT_MIN, I32)

    def score_pair(i, carry):
        dots = [indexer_dots(2 * i + j, tc) for j in range(2)]
        for j in range(2):
            score_chunk(2 * i + j, tc, dots[j])
        return carry

    lax.fori_loop(0, last // 2, score_pair, 0)

    @pl.when(last % 2 == 1)
    def _():
        score_chunk(last - 1)

    for lo, n_keys in ladder:
        @pl.when(jnp.logical_and(causal > lo, causal <= n_keys))
        def _(n_keys=n_keys):
            score_chunk(last, n_keys)

    def count(pred):
        def one(c, cnt):
            return cnt + jnp.sum(pred(chunk3(keys_ref, c), c * tc).astype(I32), axis=0)

        def two(i, cnt):
            return one(2 * i + 1, one(2 * i, cnt))

        cnt = lax.fori_loop(0, n_chunks // 2, two, jnp.zeros((8, tq), I32))
        cnt = lax.fori_loop(2 * (n_chunks // 2), n_chunks, one, cnt)
        return _sublane_all(cnt, jnp.add)

    @pl.when((qi + 1) * tq <= topk)
    def _():
        thr_ref[0] = jnp.full((8, tq), INT_MIN, I32)
        thr_ref[1] = jnp.full((8, tq), -1, I32)

    @pl.when((qi + 1) * tq > topk)
    def _():
        ans = jnp.zeros((8, tq), I32)
        n_ge = jnp.full((8, tq), seq, I32)
        for bit in range(31, -1, -1):
            cand = ans | np.int32(INT_MIN if bit == 31 else 1 << bit)
            cand_s = cand ^ np.int32(INT_MIN)
            c = count(lambda keys, start, cand_s=cand_s: keys >= cand_s[None])
            take = c >= topk
            ans = jnp.where(take, cand, ans)
            n_ge = jnp.where(take, c, n_ge)
        tau = ans ^ np.int32(INT_MIN)
        thr_ref[0] = tau
        thr_ref[1] = jnp.full((8, tq), seq, I32)

        @pl.when(jnp.max(n_ge) > topk)
        def _():
            need = topk - count(lambda keys, start: keys > tau[None])
            cut = jnp.zeros((8, tq), I32)
            for bit in range(int(math.log2(seq)), -1, -1):
                cand = cut | np.int32(1 << bit)
                c = count(lambda keys, start, cand=cand: (keys == tau[None]) & (start + k_in_chunk < cand[None]))
                cut = jnp.where(c < need, cand, cut)
            thr_ref[1] = cut

    tau = thr_ref[0]
    cut = thr_ref[1]

    v_rows = acc_ref.shape[0]
    m_ref[...] = jnp.full(m_ref.shape, NEG_BIG, F32)
    acc_ref[...] = jnp.zeros(acc_ref.shape, F32)
    pq = pos_ref[0, qi]
    pq_min = jnp.min(pq)
    last_bias = [rel_ref[REL_BUCKETS - 1, h] for h in range(N_HEADS)]

    def softmax_head(h, s_t, addend, bias, n_keys):
        cols = head_cols[h]
        sb = s_t.reshape(n_keys // 8, 8, tq) + addend
        m_old = m_ref[:, cols]
        m_new = jnp.maximum(m_old, _sublane_all(jnp.max(sb, axis=0), jnp.maximum) + bias)
        p = jnp.exp(sb - (m_new - bias)[None])
        alpha = jnp.exp(m_old - m_new)
        p_ref[0:n_keys, cols] = p.reshape(n_keys, tq).astype(BF16)
        alpha_ref[:, cols] = alpha
        m_ref[:, cols] = m_new

    def accumulate(c, half, n_keys):
        cols = slice(half * (width // 2), (half + 1) * (width // 2))
        pv = jnp.dot(vvt_ref[0, c, :, 0:n_keys], p_ref[0:n_keys, cols],
                     preferred_element_type=F32)
        scaled = acc_ref[:, cols].reshape(v_rows // 8, 8, width // 2) * alpha_ref[:, cols][None]
        acc_ref[:, cols] = scaled.reshape(v_rows, width // 2) + pv

    def all_heads(c, addend_of, bias_of, n_keys):
        start = pl.multiple_of(c * tc, tc)
        s_all = lax.dot_general(kk_ref[0, pl.ds(start, n_keys), :], qs_ref[...], NT_DIMS,
                                preferred_element_type=F32)
        for half in range(2):
            for h in range(half * N_HEADS // 2, (half + 1) * N_HEADS // 2):
                softmax_head(h, s_all[:, head_cols[h]], addend_of(h), bias_of(h), n_keys)
            accumulate(c, half, n_keys)

    def attend_chunk(c, n_keys=tc):
        n_groups = n_keys // 8
        start = pl.multiple_of(c * tc, tc)
        keys = keys_ref[pl.ds(start, n_keys), :].reshape(n_groups, 8, tq)
        sel = (keys > tau[None]) | ((keys == tau[None]) & (k_in_chunk[:n_groups] <= (cut - c * tc)[None]))
        selb_ref[0:n_keys, :] = jnp.where(sel, 0.0, NEG_BIG).reshape(n_keys, tq)
        first_tile = c * (tc // t)
        pk_max = pos_ref[0, first_tile]
        for j in range(1, n_keys // t):
            pk_max = jnp.maximum(pk_max, pos_ref[0, first_tile + j])
        chunk_far = pq_min - jnp.max(pk_max) >= FAR_DIST

        @pl.when(chunk_far)
        def _():
            all_heads(c, lambda h: selb_ref[0:n_keys, :].reshape(n_groups, 8, tq), lambda h: last_bias[h], n_keys)

        @pl.when(jnp.logical_not(chunk_far))
        def _():
            for j in range(n_keys // t):
                rows = slice(j * t, (j + 1) * t)
                pk = pos_ref[0, first_tile + j]
                tile_far = jnp.logical_or(pq_min - jnp.max(pk) >= FAR_DIST, c * tc + j * t >= (qi + 1) * tq)

                @pl.when(tile_far)
                def _():
                    for h in range(N_HEADS):
                        add_ref[rows, head_cols[h]] = selb_ref[rows, :] + last_bias[h]

                @pl.when(jnp.logical_not(tile_far))
                def _():
                    pk_col = jnp.transpose(jnp.broadcast_to(pk, (t, t)))
                    dist = jnp.clip(pq - pk_col, 0, t - 1)
                    for h in range(N_HEADS):
                        table = jnp.broadcast_to(tab_ref[h:h + 1, :], (t, t))
                        bias = jnp.take_along_axis(table, dist, axis=1, mode="promise_in_bounds")
                        add_ref[rows, head_cols[h]] = selb_ref[rows, :] + bias

            all_heads(c, lambda h: add_ref[0:n_keys, head_cols[h]].reshape(n_groups, 8, tq), lambda h: 0.0, n_keys)

    def chunk_is_far(c):
        pk_max = pos_ref[0, c * (tc // t)]
        for j in range(1, tc // t):
            pk_max = jnp.maximum(pk_max, pos_ref[0, c * (tc // t) + j])
        return pq_min - jnp.max(pk_max) >= FAR_DIST

    def far_pair(a):
        scores = []
        for c, ref in ((a, selb_ref), (a + 1, selb2_ref)):
            start = pl.multiple_of(c * tc, tc)
            scores.append(lax.dot_general(kk_ref[0, pl.ds(start, tc), :], qs_ref[...], NT_DIMS,
                                          preferred_element_type=F32))
            keys = chunk3(keys_ref, c)
            sel = (keys > tau[None]) | ((keys == tau[None]) & (k_in_chunk <= (cut - c * tc)[None]))
            ref[...] = jnp.where(sel, 0.0, NEG_BIG).reshape(tc, tq)
        for (c, ref), s_all in zip(((a, selb_ref), (a + 1, selb2_ref)), scores):
            for half in range(2):
                for h in range(half * N_HEADS // 2, (half + 1) * N_HEADS // 2):
                    softmax_head(h, s_all[:, head_cols[h]], ref[...].reshape(groups, 8, tq), last_bias[h], tc)
                accumulate(c, half, tc)

    def chunk_pair(i, carry):
        a = 2 * i
        both_far = jnp.logical_and(chunk_is_far(a), chunk_is_far(a + 1))

        @pl.when(both_far)
        def _():
            far_pair(a)

        @pl.when(jnp.logical_not(both_far))
        def _():
            attend_chunk(a)
            attend_chunk(a + 1)

        return carry

    lax.fori_loop(0, last // 2, chunk_pair, 0)

    @pl.when(last % 2 == 1)
    def _():
        attend_chunk(last - 1)
    for lo, n_keys in ladder:
        @pl.when(jnp.logical_and(causal > lo, causal <= n_keys))
        def _(n_keys=n_keys):
            attend_chunk(last, n_keys)

    for p in range(N_HEADS // 2):
        outs = []
        for h in (2 * p, 2 * p + 1):
            cols = head_cols[h]
            outs.append(acc_ref[0:HEAD_DIM, cols] / acc_ref[HEAD_DIM:HEAD_DIM + 1, cols])
        o_ref[0, :, p * LANES:(p + 1) * LANES] = jnp.transpose(jnp.concatenate(outs, axis=0))


def _dsa_attention(rel_bias, dist_table, pb, ixw, dq, dkk, vvt, pos_rows, b, s, tq, tc):
    topk = min(TOPK_MAX, s // 4)
    pb3 = pb.reshape(b, s, N_BF)
    width = N_HEADS * tq
    kernel = functools.partial(_dsa_kernel, tq=tq, tc=tc, topk=topk, seq=s)
    return pl.pallas_call(
        kernel,
        grid=(b, s // tq),
        in_specs=[pl.BlockSpec(memory_space=pltpu.SMEM),
                  pl.BlockSpec(dist_table.shape, lambda bi, i: (0, 0)),
                  pl.BlockSpec((1, tq, 2 * LANES), lambda bi, i: (bi, i, B_IXQ // 256)),
                  pl.BlockSpec((1, tq, LANES), lambda bi, i: (bi, i, 0)),
                  pl.BlockSpec((1, s, LANES), lambda bi, i: (bi, 0, B_IXK // LANES)),
                  pl.BlockSpec((1, tq, N_HEADS * LANES), lambda bi, i: (bi, i, 0)),
                  pl.BlockSpec((1, s, LANES), lambda bi, i: (bi, 0, 0)),
                  pl.BlockSpec((1, s // tc, V_ROWS, tc), lambda bi, i: (bi, 0, 0, 0)),
                  pl.BlockSpec((1, s // LANES, 1, LANES), lambda bi, i: (bi, 0, 0, 0))],
        out_specs=pl.BlockSpec((1, tq, BRANCH_WIDTH), lambda bi, i: (bi, i, 0)),
        out_shape=jax.ShapeDtypeStruct((b, s, BRANCH_WIDTH), F32),
        scratch_shapes=[pltpu.VMEM((s, tq), I32),
                        pltpu.VMEM((width, LANES), BF16),
                        pltpu.VMEM((width, LANES), BF16),
                        pltpu.VMEM((LANES, tq), F32),
                        pltpu.VMEM((tc, width), BF16),
                        pltpu.VMEM((tc, tq), F32),
                        pltpu.VMEM((tc, tq), F32),
                        pltpu.VMEM((tc, width), F32),
                        pltpu.VMEM((8, width), F32),
                        pltpu.VMEM((8, width), F32),
                        pltpu.VMEM((V_ROWS, width), F32),
                        pltpu.VMEM((2, 8, tq), I32)],
        compiler_params=pltpu.CompilerParams(dimension_semantics=("parallel", "arbitrary"),
                                             vmem_limit_bytes=VMEM_LIMIT),
        name="dsa_attention",
    )(rel_bias, dist_table, pb3, ixw.reshape(b, s, LANES), pb3, dq.reshape(b, s, -1), dkk.reshape(b, s, -1), vvt, pos_rows)


def _merge_kernel(x_ref, ya_ref, yb_ref, yc_ref, za_ref, zb_ref, zc_ref, ga_ref, gb_ref, gc_ref,
                  bias_ref, wbr_ref, wout_ref, o_ref):
    merged = None
    branches = ((ya_ref, za_ref, ga_ref), (yb_ref, zb_ref, gb_ref), (yc_ref, zc_ref, gc_ref))
    for n, (y_ref, z_ref, g_ref) in enumerate(branches):
        z = z_ref[...].astype(F32)
        branch = (y_ref[...] * (z * jax.nn.sigmoid(z))).astype(BF16)
        up = jnp.dot(branch, wbr_ref[n], preferred_element_type=F32)
        gate = jax.nn.sigmoid(g_ref[...].astype(F32) + bias_ref[n:n + 1, :])
        merged = gate * up if merged is None else merged + gate * up
    o_ref[...] = x_ref[...] + jnp.dot(merged.astype(BF16), wout_ref[...], preferred_element_type=F32)


def _merge(x2, ya, yb, yc, pb, gate_b, wbr, wout, tm):
    m, d = x2.shape
    row = lambda width: pl.BlockSpec((tm, width), lambda i: (i, 0))
    col = lambda width, off: pl.BlockSpec((tm, width), lambda i, _b=off // width: (i, _b))
    z_specs = [col(BRANCH_WIDTH, B_Z + n * BRANCH_WIDTH) for n in range(N_BRANCH)]
    g_specs = [col(d, B_G + n * d) for n in range(N_BRANCH)]
    return pl.pallas_call(
        _merge_kernel,
        grid=(m // tm,),
        in_specs=[row(d), row(BRANCH_WIDTH), row(BRANCH_WIDTH), row(BRANCH_WIDTH)] + z_specs + g_specs + [
            pl.BlockSpec(gate_b.shape, lambda i: (0, 0)),
            pl.BlockSpec(wbr.shape, lambda i: (0, 0, 0)),
            pl.BlockSpec(wout.shape, lambda i: (0, 0))],
        out_specs=row(d),
        out_shape=jax.ShapeDtypeStruct((m, d), F32),
        compiler_params=pltpu.CompilerParams(dimension_semantics=("parallel",), vmem_limit_bytes=VMEM_LIMIT),
        name="merge",
    )(x2, ya, yb, yc, pb, pb, pb, pb, pb, pb, gate_b, wbr, wout)


def _lane_table(vals_by_lane):
    t = jnp.zeros((LANES,), F32)
    for start, v in vals_by_lane.items():
        t = t.at[start:start + v.shape[0]].set(v)
    return t[None, :]


def _layer_weights(l, w_all, norm_g, mla_q_norm_g, mla_kv_norm_g, mla_w_uq, mla_w_ukv, mla_q_g, mla_k_g,
                   dsa_q_g, dsa_k_g, gate_b, w_branch, w_out):
    uq, ukv = mla_w_uq[l], mla_w_ukv[l]
    half = MLA_ROPE // 2
    q_pieces, qr_pieces, k_pieces, v_pieces = [], [], [], []
    for h in range(N_HEADS):
        o = h * MLA_QK
        q_pieces += [(o, MLA_QK, 1.0), (None, LANES - MLA_QK, 1.0)]
        qr_pieces += [(None, MLA_NOPE, 1.0), (o + MLA_NOPE + half, half, 1.0), (o + MLA_NOPE, half, 1.0),
                      (None, LANES - MLA_QK, 1.0)]
        k_pieces += [(h * 128, MLA_NOPE, 1.0), (None, LANES - MLA_NOPE, 1.0)]
        v_pieces += [(h * 128 + MLA_NOPE, 64, 1.0)]
    scale = MLA_QK ** -0.5
    qg, kg = mla_q_g[l], mla_k_g[l]
    rot_gain = lambda g: jnp.concatenate([g[MLA_NOPE + half:], g[MLA_NOPE:MLA_NOPE + half]])
    dq_g = jnp.concatenate([jnp.concatenate([dsa_q_g[l], jnp.zeros((HEAD_DIM,), F32)]) if h % 2 == 0 else
                            jnp.concatenate([jnp.zeros((HEAD_DIM,), F32), dsa_q_g[l]]) for h in range(N_HEADS)])
    return dict(
        norm_g=norm_g[l][None, :], w_all=w_all[l],
        wuq=_assemble(uq, q_pieces).astype(BF16), wuqr=_assemble(uq, qr_pieces).astype(BF16),
        wuk=_assemble(ukv, k_pieces).astype(BF16), wuv=_assemble(ukv, v_pieces).astype(BF16),
        qng=mla_q_norm_g[l][None, :], kvng=mla_kv_norm_g[l][None, :],
        gq=_lane_table({0: qg * scale}), gqr=_lane_table({MLA_NOPE: rot_gain(qg) * scale}),
        gk=_lane_table({0: kg}), gkr=_lane_table({MLA_NOPE: rot_gain(kg)}),
        gdq=(dq_g * HEAD_DIM ** -0.5)[None, :], gdk=jnp.concatenate([dsa_k_g[l], dsa_k_g[l]])[None, :],
        gate_b=gate_b[l], wbr=w_branch[l].astype(BF16), wout=w_out[l].astype(BF16))


def kernel(x, positions, norm_g, w_in, mla_q_norm_g, mla_kv_norm_g, mla_w_uq, mla_w_ukv, mla_q_g, mla_k_g,
           dsa_q_g, dsa_k_g, rel_bias, gate_b, w_branch, w_out):
    b, s, d = x.shape
    depth = w_in.shape[0]
    m = b * s
    assert d == D_MODEL and w_in.shape[-1] == D_IN and s % 512 == 0 and s & (s - 1) == 0

    half = MLA_ROPE // 2
    inv_freq = ROPE_THETA ** (-(jnp.arange(half, dtype=F32) * 2.0) / MLA_ROPE)
    ang = positions.astype(F32)[..., None] * inv_freq
    cos, sin = jnp.cos(ang).reshape(m, half), jnp.sin(ang).reshape(m, half)
    cos_t = jnp.concatenate([jnp.ones((m, MLA_NOPE), F32), cos, cos, jnp.ones((m, LANES - MLA_QK), F32)], axis=1)
    sin_t = jnp.concatenate([jnp.zeros((m, MLA_NOPE), F32), -sin, sin, jnp.zeros((m, LANES - MLA_QK), F32)], axis=1)

    bf_pieces, f32_pieces = _in_proj_layouts()
    w_in16 = w_in.astype(BF16)
    w_all = _assemble(w_in16, f32_pieces + bf_pieces)

    tm = min(1024, m)
    mla_tc = min(1024, s)
    dsa_tc = 512
    tp = min(512, m)
    pos_rows = positions.reshape(b, s // LANES, 1, LANES)
    dist_table = rel_bias[BUCKET_OF_DIST, :].T
    x2 = x.reshape(m, d)
    for l in range(depth):
        lw = _layer_weights(l, w_all, norm_g, mla_q_norm_g, mla_kv_norm_g, mla_w_uq, mla_w_ukv,
                            mla_q_g, mla_k_g, dsa_q_g, dsa_k_g, gate_b, w_branch, w_out)
        pb = _in_proj(x2, lw["norm_g"], lw["w_all"], N_F32, N_BF, BF16, min(2 * tm, m), 1024)
        qb, kb, vbt, dq, dkk, ixw, vvt = _prologue(x2, cos_t, sin_t, lw, mla_tc, dsa_tc)
        ya = _sb_attention(pb, b, s, 128)
        vbt = vbt.reshape(b, s // mla_tc, N_HEADS // 2, LANES + 8, mla_tc)
        yb = _mla_attention(qb, kb, vbt, b, s, LANES, mla_tc)
        vvt = vvt.reshape(b, s // dsa_tc, V_ROWS, dsa_tc)
        yc = _dsa_attention(rel_bias, dist_table, pb, ixw, dq, dkk, vvt, pos_rows, b, s, LANES, dsa_tc)
        x2 = _merge(x2, ya.reshape(m, -1), yb.reshape(m, -1), yc.reshape(m, -1), pb, lw["gate_b"],
                    lw["wbr"], lw["wout"], tp)
    return x2.reshape(b, s, d)
```

```python
import functools
import math

import numpy as np
import jax
import jax.numpy as jnp
from jax import lax
from jax.experimental import pallas as pl
from jax.experimental.pallas import tpu as pltpu

F32 = jnp.float32
BF16 = jnp.bfloat16
I32 = jnp.int32

LANES = 128
EPS = 1e-6
D_MODEL = 1024
N_HEADS = 8
HEAD_DIM = 64
MLA_Q_LORA = 256
MLA_KV_LORA = 128
MLA_NOPE = 64
MLA_ROPE = 32
MLA_QK = MLA_NOPE + MLA_ROPE
ROPE_THETA = 10000.0
IDX_DIM = 32
TOPK_MAX = 256
REL_BUCKETS = 32
REL_MAX_DIST = 128
N_BRANCH = 3
BRANCH_WIDTH = 512

O_SAQ, O_SAK, O_SAV = 0, 512, 1024
O_CQ, O_CKV, O_KPE = 1536, 1792, 1920
O_DSQ, O_DSK, O_DSV = 1952, 2464, 2528
O_IXQ, O_IXK, O_IXW = 2592, 2848, 2880
O_Z, O_G = 2888, 4424
D_IN = 7496

NEG_BIG = -1e30
INT_MIN = -(2 ** 31)
SB_UNDERFLOW = 104.0
VMEM_LIMIT = 48 * 1024 * 1024

NT_DIMS = (((1,), (1,)), ((), ()))
V_ROWS = HEAD_DIM + 8


def _t5_buckets():
    exact = REL_BUCKETS // 2
    n = np.arange(0, 4 * REL_MAX_DIST)
    nf = np.maximum(n, 1).astype(np.float64)
    large = exact + (np.log(nf / exact) / math.log(REL_MAX_DIST / exact) * (REL_BUCKETS - exact)).astype(np.int64)
    bucket = np.where(n < exact, n, np.minimum(large, REL_BUCKETS - 1))
    assert np.all(np.diff(bucket) >= 0) and np.all(bucket[LANES - 1:] == REL_BUCKETS - 1)
    return bucket[:LANES]


BUCKET_OF_DIST = _t5_buckets()
FAR_DIST = int(np.argmax(BUCKET_OF_DIST == REL_BUCKETS - 1))


def _assemble(w, pieces):
    cols = []
    for start, width, scale in pieces:
        if start is None:
            cols.append(jnp.zeros(w.shape[:-1] + (width,), w.dtype))
        else:
            c = w[..., start:start + width]
            cols.append(c if scale == 1.0 else c * scale)
    return jnp.concatenate(cols, axis=-1)


def _head_padded(start, scale=1.0):
    pieces = []
    for h in range(N_HEADS):
        col = (start + h * HEAD_DIM, HEAD_DIM, scale)
        pad = (None, HEAD_DIM, 1.0)
        pieces += [col, pad] if h % 2 == 0 else [pad, col]
    return pieces


def _in_proj_layouts():
    bf = (_head_padded(O_SAQ, HEAD_DIM ** -0.5)
          + [(O_SAK, 512, 1.0), (O_SAV, 512, 1.0), (O_IXQ, 256, 1.0)]
          + [(None, 128, 1.0)]
          + [(O_IXK, 32, 1.0)] * 4
          + [(O_Z, 1536, 1.0), (O_G, 3072, 1.0)])
    f32 = (_head_padded(O_DSQ)
           + [(O_CQ, 256, 1.0), (O_CKV, 128, 1.0)]
           + [(None, 64, 1.0), (O_KPE, 32, 1.0), (None, 32, 1.0)]
           + [(None, 64, 1.0), (O_KPE + 16, 16, 1.0), (O_KPE, 16, 1.0), (None, 32, 1.0)]
           + [(O_DSK, 64, 1.0)] * 2
           + [(O_IXW, 8, 1.0), (None, 120, 1.0)]
           + [(O_DSV, 64, 1.0), (None, 64, 1.0)])
    return bf, f32


B_SAQ, B_SAK, B_SAV, B_IXQ, B_IXK, B_Z, B_G, N_BF = 0, 1024, 1536, 2048, 2432, 2560, 4096, 7168
F_DSQ, F_CQ, F_CKV, F_KPE, F_KPER, F_DSK, F_IXW, F_DSV, N_F32 = 0, 1024, 1280, 1408, 1536, 1664, 1792, 1920, 2048


def _in_proj_kernel(x_ref, g_ref, w_ref, o_ref, h_ref):
    @pl.when(pl.program_id(1) == 0)
    def _():
        x = x_ref[...]
        ms = jnp.mean(x * x, axis=-1, keepdims=True)
        h_ref[...] = (x * lax.rsqrt(ms + EPS) * g_ref[...]).astype(BF16)

    o_ref[...] = jnp.dot(h_ref[...], w_ref[...], preferred_element_type=F32).astype(o_ref.dtype)


def _in_proj(x2, g, w, col0, n, out_dtype, tm, tn):
    m, d = x2.shape
    return pl.pallas_call(
        _in_proj_kernel,
        grid=(m // tm, n // tn),
        in_specs=[pl.BlockSpec((tm, d), lambda i, j: (i, 0)),
                  pl.BlockSpec((1, d), lambda i, j: (0, 0)),
                  pl.BlockSpec((d, tn), lambda i, j: (0, col0 // tn + j))],
        out_specs=pl.BlockSpec((tm, tn), lambda i, j: (i, j)),
        out_shape=jax.ShapeDtypeStruct((m, n), out_dtype),
        scratch_shapes=[pltpu.VMEM((tm, d), BF16)],
        compiler_params=pltpu.CompilerParams(dimension_semantics=("parallel", "arbitrary"),
                                             vmem_limit_bytes=VMEM_LIMIT),
        name="in_proj",
    )(x2, g, w)


def _rms(x, width):
    return lax.rsqrt(jnp.sum(x * x, axis=-1, keepdims=True) * (1.0 / width) + EPS)


def _prologue_kernel(x_ref, ng_ref, wf_ref, cos_ref, sin_ref,
                     wuq_ref, wuqr_ref, wuk_ref, wuv_ref,
                     qng_ref, kvng_ref, gq_ref, gqr_ref, gk_ref, gkr_ref, gdq_ref, gdk_ref,
                     qb_ref, kb_ref, vbt_ref, dq_ref, dkk_ref, ixw_ref, vvt_ref):
    x = x_ref[...]
    hidden = (x * lax.rsqrt(jnp.mean(x * x, axis=-1, keepdims=True) + EPS) * ng_ref[...]).astype(BF16)
    pf = jnp.dot(hidden, wf_ref[...], preferred_element_type=F32)
    ixw_ref[...] = pf[:, F_IXW:F_IXW + LANES]
    cos = cos_ref[...]
    sin = sin_ref[...]

    cq = pf[:, F_CQ:F_CQ + MLA_Q_LORA]
    cqn = (cq * _rms(cq, MLA_Q_LORA) * qng_ref[...]).astype(BF16)
    q = jnp.dot(cqn, wuq_ref[...], preferred_element_type=F32)
    qr = jnp.dot(cqn, wuqr_ref[...], preferred_element_type=F32)
    cg = cos * gq_ref[...]
    sg = sin * gqr_ref[...]
    for h in range(N_HEADS):
        sl = slice(h * LANES, (h + 1) * LANES)
        qh = q[:, sl]
        qb_ref[:, sl] = ((qh * cg + qr[:, sl] * sg) * _rms(qh, MLA_QK)).astype(BF16)

    ckv = pf[:, F_CKV:F_CKV + MLA_KV_LORA]
    ckvn = (ckv * _rms(ckv, MLA_KV_LORA) * kvng_ref[...]).astype(BF16)
    kn = jnp.dot(ckvn, wuk_ref[...], preferred_element_type=F32)
    tm = x.shape[0]
    ones_rows = (lax.broadcasted_iota(I32, (8, tm), 0) == 0).astype(BF16)
    v = jnp.dot(ckvn, wuv_ref[...], preferred_element_type=F32)
    for p in range(N_HEADS // 2):
        vbt_ref[0, p, 0:LANES, :] = jnp.transpose(v[:, p * LANES:(p + 1) * LANES]).astype(BF16)
        vbt_ref[0, p, LANES:LANES + 8, :] = ones_rows
    sub = vvt_ref.shape[3]
    for j in range(tm // sub):
        vals = pf[j * sub:(j + 1) * sub, F_DSV:F_DSV + LANES]
        vvt_ref[0, j, 0:HEAD_DIM, :] = jnp.transpose(vals)[0:HEAD_DIM].astype(BF16)
        vvt_ref[0, j, HEAD_DIM:V_ROWS, :] = (lax.broadcasted_iota(I32, (8, sub), 0) == 0).astype(BF16)
    kpe = pf[:, F_KPE:F_KPE + LANES]
    cgk = cos * gk_ref[...]
    rot = pf[:, F_KPER:F_KPER + LANES] * (sin * gkr_ref[...])
    for h in range(N_HEADS):
        sl = slice(h * LANES, (h + 1) * LANES)
        kf = kn[:, sl] + kpe
        kb_ref[:, sl] = ((kf * cgk + rot) * _rms(kf, MLA_QK)).astype(BF16)

    for h in range(N_HEADS):
        sl = slice(h * LANES, (h + 1) * LANES)
        xq = pf[:, F_DSQ + h * LANES:F_DSQ + (h + 1) * LANES]
        dq_ref[:, sl] = (xq * _rms(xq, HEAD_DIM) * gdq_ref[:, sl]).astype(BF16)
    xk = pf[:, F_DSK:F_DSK + LANES]
    dkk_ref[...] = (xk * _rms(xk, 2 * HEAD_DIM) * gdk_ref[...]).astype(BF16)


def _prologue(x2, cos_t, sin_t, lw, tm, sparse_tc):
    m, d = x2.shape
    assert tm % sparse_tc == 0

    def whole(a):
        return pl.BlockSpec(a.shape, lambda i: (0,) * a.ndim)

    weights = [lw[k] for k in ("wuq", "wuqr", "wuk", "wuv", "qng", "kvng", "gq", "gqr", "gk", "gkr", "gdq", "gdk")]
    row = lambda width: pl.BlockSpec((tm, width), lambda i: (i, 0))
    return pl.pallas_call(
        _prologue_kernel,
        grid=(m // tm,),
        in_specs=[row(d), whole(lw["norm_g"]), pl.BlockSpec((d, N_F32), lambda i: (0, 0)), row(LANES), row(LANES)]
        + [whole(w) for w in weights],
        out_specs=[row(1024), row(1024),
                   pl.BlockSpec((1, N_HEADS // 2, LANES + 8, tm), lambda i: (i, 0, 0, 0)),
                   row(1024), row(LANES), row(LANES),
                   pl.BlockSpec((1, tm // sparse_tc, V_ROWS, sparse_tc), lambda i: (i, 0, 0, 0))],
        out_shape=[jax.ShapeDtypeStruct((m, 1024), BF16), jax.ShapeDtypeStruct((m, 1024), BF16),
                   jax.ShapeDtypeStruct((m // tm, N_HEADS // 2, LANES + 8, tm), BF16),
                   jax.ShapeDtypeStruct((m, 1024), BF16),
                   jax.ShapeDtypeStruct((m, LANES), BF16), jax.ShapeDtypeStruct((m, LANES), F32),
                   jax.ShapeDtypeStruct((m // tm, tm // sparse_tc, V_ROWS, sparse_tc), BF16)],
        compiler_params=pltpu.CompilerParams(dimension_semantics=("parallel",), vmem_limit_bytes=VMEM_LIMIT),
        name="prologue",
    )(x2, lw["norm_g"], lw["w_all"], cos_t, sin_t, *weights)


def _sublane_all(x, op):
    for shift in (4, 2, 1):
        x = op(x, pltpu.roll(x, shift, 0))
    return x


def _sb_kernel(q_ref, k_ref, v_ref, o_ref, acc_ref, carry_ref, hl_ref, lb_ref, *, tq):
    t = LANES
    qi = pl.program_id(1)
    jj = lax.broadcasted_iota(I32, (t, 2 * t), 0)
    ss = lax.broadcasted_iota(I32, (t, 2 * t), 1)
    tri_ones = jnp.logical_or(ss >= t, jj > ss).astype(BF16)
    q_idx = qi * tq + lax.broadcasted_iota(I32, (tq, t), 0)
    k_lane = lax.broadcasted_iota(I32, (tq, t), 1)

    acc_ref[...] = jnp.zeros(acc_ref.shape, F32)
    carry_ref[...] = jnp.zeros(carry_ref.shape, F32)

    def cond(c):
        kb, cmax = c
        return jnp.logical_and(kb >= 0, cmax > -SB_UNDERFLOW)

    def body(c):
        kb, _ = c
        start = pl.multiple_of(kb * t, t)
        mask = (start + k_lane) < q_idx
        for h in range(N_HEADS):
            q = q_ref[0, :, h * LANES:(h + 1) * LANES]
            k = k_ref[0, pl.ds(start, t), (h // 2) * LANES:(h // 2 + 1) * LANES]
            z = lax.dot_general(q, k, NT_DIMS, preferred_element_type=F32)
            sp = jnp.maximum(z, 0.0) + jnp.log(1.0 + jnp.exp(-jnp.abs(z)))
            l1m = jnp.where(mask, -sp, 0.0)
            hi = l1m.astype(BF16)
            hl_ref[2 * h * tq:(2 * h + 1) * tq, :] = hi
            hl_ref[(2 * h + 1) * tq:(2 * h + 2) * tq, :] = (l1m - hi.astype(F32)).astype(BF16)
            lb_ref[h * tq:(h + 1) * tq, :] = z - sp
        sums = jnp.dot(hl_ref[...], tri_ones, preferred_element_type=F32)
        for h in range(N_HEADS):
            rows = slice(h * tq, (h + 1) * tq)
            both = sums[2 * h * tq:(2 * h + 1) * tq, :] + sums[(2 * h + 1) * tq:(2 * h + 2) * tq, :]
            carry = carry_ref[rows, :]
            a = jnp.where(mask, jnp.exp(lb_ref[rows, :] + carry + both[:, :t]), 0.0)
            v = v_ref[0, pl.ds(start, t), (h // 2) * LANES:(h // 2 + 1) * LANES]
            acc_ref[rows, :] += jnp.dot(a.astype(BF16), v, preferred_element_type=F32)
            carry_ref[rows, :] = carry + both[:, t:]
        return kb - 1, jnp.max(carry_ref[...])

    lax.while_loop(cond, body, (((qi + 1) * tq) // t - 1, jnp.float32(0.0)))

    lane = lax.broadcasted_iota(I32, (tq, LANES), 1)
    for p in range(N_HEADS // 2):
        even = acc_ref[2 * p * tq:(2 * p + 1) * tq, :]
        odd = acc_ref[(2 * p + 1) * tq:(2 * p + 2) * tq, :]
        o_ref[0, :, p * LANES:(p + 1) * LANES] = jnp.where(lane < HEAD_DIM, even, odd)


def _sb_attention(pb, b, s, tq):
    pb3 = pb.reshape(b, s, N_BF)
    kernel = functools.partial(_sb_kernel, tq=tq)
    return pl.pallas_call(
        kernel,
        grid=(b, s // tq),
        in_specs=[pl.BlockSpec((1, tq, N_HEADS * LANES), lambda bi, i: (bi, i, B_SAQ // (N_HEADS * LANES))),
                  pl.BlockSpec((1, s, BRANCH_WIDTH), lambda bi, i: (bi, 0, B_SAK // BRANCH_WIDTH)),
                  pl.BlockSpec((1, s, BRANCH_WIDTH), lambda bi, i: (bi, 0, B_SAV // BRANCH_WIDTH))],
        out_specs=pl.BlockSpec((1, tq, BRANCH_WIDTH), lambda bi, i: (bi, i, 0)),
        out_shape=jax.ShapeDtypeStruct((b, s, BRANCH_WIDTH), F32),
        scratch_shapes=[pltpu.VMEM((N_HEADS * tq, LANES), F32),
                        pltpu.VMEM((N_HEADS * tq, LANES), F32),
                        pltpu.VMEM((2 * N_HEADS * tq, LANES), BF16),
                        pltpu.VMEM((N_HEADS * tq, LANES), F32)],
        compiler_params=pltpu.CompilerParams(dimension_semantics=("parallel", "arbitrary"),
                                             vmem_limit_bytes=VMEM_LIMIT),
        name="sb_attention",
    )(pb3, pb3, pb3)


def _mla_kernel(q_ref, k_ref, vt_ref, o_ref, qbd_ref, p_ref, m_ref, alpha_ref, acc_ref, *, tq, tc):
    t = LANES
    assert tq == t
    groups = tc // 8
    pair_w = 2 * tq
    qi = pl.program_id(1)
    n_chunks = (qi * tq) // tc + 1
    k_in_chunk = (lax.broadcasted_iota(I32, (groups, 8, tq), 0) * 8
                  + lax.broadcasted_iota(I32, (groups, 8, tq), 1))
    q_lane = qi * tq + lax.broadcasted_iota(I32, (groups, 8, tq), 2)

    zeros = jnp.zeros((tq, t), BF16)
    for p in range(N_HEADS // 2):
        q_even = q_ref[0, :, 2 * p * LANES:(2 * p + 1) * LANES]
        q_odd = q_ref[0, :, (2 * p + 1) * LANES:(2 * p + 2) * LANES]
        qbd_ref[p, 0:tq, :] = jnp.concatenate([q_even, zeros], axis=1)
        qbd_ref[p, tq:2 * tq, :] = jnp.concatenate([zeros, q_odd], axis=1)

    v_rows = acc_ref.shape[0]
    m_ref[...] = jnp.full(m_ref.shape, NEG_BIG, F32)
    acc_ref[...] = jnp.zeros(acc_ref.shape, F32)

    pair_cols = [slice(p * pair_w, (p + 1) * pair_w) for p in range(N_HEADS // 2)]

    def scores_of(c, n_keys):
        start = pl.multiple_of(c * tc, tc)
        return [lax.dot_general(k_ref[0, pl.ds(start, n_keys), pair_cols[p]], qbd_ref[p], NT_DIMS,
                                preferred_element_type=F32) for p in range(N_HEADS // 2)]

    def chunk(c, diagonal, n_keys=tc, s_pairs=None):
        start = pl.multiple_of(c * tc, tc)
        n_groups = n_keys // 8
        if s_pairs is None:
            s_pairs = scores_of(c, n_keys)
        for p in range(N_HEADS // 2):
            pcols = pair_cols[p]
            for i in range(2):
                cols = slice(p * pair_w + i * tq, p * pair_w + (i + 1) * tq)
                sb = s_pairs[p][:, i * tq:(i + 1) * tq].reshape(n_groups, 8, tq)
                if diagonal:
                    sb = jnp.where(start + k_in_chunk[:n_groups] <= q_lane[:n_groups], sb, NEG_BIG)
                m_old = m_ref[:, cols]
                m_new = jnp.maximum(m_old, _sublane_all(jnp.max(sb, axis=0), jnp.maximum))
                prob = jnp.exp(sb - m_new[None])
                alpha = jnp.exp(m_old - m_new)
                p_ref[0:n_keys, cols] = prob.reshape(n_keys, tq).astype(BF16)
                alpha_ref[:, cols] = alpha
                m_ref[:, cols] = m_new
            pv = jnp.dot(vt_ref[0, c, p, :, 0:n_keys], p_ref[0:n_keys, pcols],
                         preferred_element_type=F32)
            scaled = acc_ref[:, pcols].reshape(v_rows // 8, 8, pair_w) * alpha_ref[:, pcols][None]
            acc_ref[:, pcols] = scaled.reshape(v_rows, pair_w) + pv

    def chunk_pair(i, carry):
        both = [scores_of(2 * i + j, tc) for j in range(2)]
        for j in range(2):
            chunk(2 * i + j, False, tc, both[j])
        return carry

    lax.fori_loop(0, (n_chunks - 1) // 2, chunk_pair, 0)

    @pl.when((n_chunks - 1) % 2 == 1)
    def _():
        chunk(n_chunks - 2, False)

    last = n_chunks - 1
    causal = (qi + 1) * tq - last * tc
    sizes = [n for n in (tc // 4, tc // 2, 3 * tc // 4) if n >= 2 * t] + [tc]
    for lo, n_keys in zip([0] + sizes[:-1], sizes):
        @pl.when(jnp.logical_and(causal > lo, causal <= n_keys))
        def _(n_keys=n_keys):
            chunk(last, True, n_keys)

    for p in range(N_HEADS // 2):
        outs = []
        for i in range(2):
            cols = slice(p * pair_w + i * tq, p * pair_w + (i + 1) * tq)
            outs.append(acc_ref[i * HEAD_DIM:(i + 1) * HEAD_DIM, cols] / acc_ref[t:t + 1, cols])
        o_ref[0, :, p * LANES:(p + 1) * LANES] = jnp.transpose(jnp.concatenate(outs, axis=0))


def _mla_attention(qb, kb, vbt, b, s, tq, tc):
    width = N_HEADS * tq
    kernel = functools.partial(_mla_kernel, tq=tq, tc=tc)
    return pl.pallas_call(
        kernel,
        grid=(b, s // tq),
        in_specs=[pl.BlockSpec((1, tq, N_HEADS * LANES), lambda bi, i: (bi, i, 0)),
                  pl.BlockSpec((1, s, N_HEADS * LANES), lambda bi, i: (bi, 0, 0)),
                  pl.BlockSpec((1, s // tc, N_HEADS // 2, LANES + 8, tc), lambda bi, i: (bi, 0, 0, 0, 0))],
        out_specs=pl.BlockSpec((1, tq, BRANCH_WIDTH), lambda bi, i: (bi, i, 0)),
        out_shape=jax.ShapeDtypeStruct((b, s, BRANCH_WIDTH), F32),
        scratch_shapes=[pltpu.VMEM((N_HEADS // 2, 2 * tq, 2 * LANES), BF16),
                        pltpu.VMEM((tc, width), BF16),
                        pltpu.VMEM((8, width), F32),
                        pltpu.VMEM((8, width), F32),
                        pltpu.VMEM((LANES + 8, width), F32)],
        compiler_params=pltpu.CompilerParams(dimension_semantics=("parallel", "arbitrary"),
                                             vmem_limit_bytes=VMEM_LIMIT),
        name="mla_attention",
    )(qb.reshape(b, s, -1), kb.reshape(b, s, -1), vbt)


def _sortable(x):
    bits = pltpu.bitcast(x + 0.0, I32)
    return bits ^ ((bits >> 31) & 0x7FFFFFFF)


def _dsa_kernel(rel_ref, tab_ref, ixq_ref, ixw_ref, ixk_ref, q_ref, kk_ref, vvt_ref, pos_ref, o_ref,
                keys_ref, qm_ref, qs_ref, wt_ref, p_ref, selb_ref, selb2_ref, add_ref,
                m_ref, alpha_ref, acc_ref, thr_ref, *, tq, tc, topk, seq):
    t = LANES
    assert tq == t
    groups = tc // 8
    width = N_HEADS * tq
    qi = pl.program_id(1)
    n_chunks = (qi * tq) // tc + 1
    head_cols = [slice(h * tq, (h + 1) * tq) for h in range(N_HEADS)]
    lane = lax.broadcasted_iota(I32, (tq, t), 1)
    q_lane = qi * tq + lax.broadcasted_iota(I32, (groups, 8, tq), 2)
    k_in_chunk = (lax.broadcasted_iota(I32, (groups, 8, tq), 0) * 8
                  + lax.broadcasted_iota(I32, (groups, 8, tq), 1))

    for h in range(N_HEADS):
        grp = ixq_ref[0, :, (h // 4) * LANES:(h // 4 + 1) * LANES]
        lo = (h % 4) * IDX_DIM
        qm_ref[h * tq:(h + 1) * tq, :] = jnp.where((lane >= lo) & (lane < lo + IDX_DIM), grp, jnp.zeros_like(grp))
        qs_ref[h * tq:(h + 1) * tq, :] = q_ref[0, :, h * LANES:(h + 1) * LANES]
    wt_ref[...] = jnp.transpose(ixw_ref[0] * ((IDX_DIM ** -0.5) * (N_HEADS ** -0.5)))

    def chunk3(ref, c):
        start = pl.multiple_of(c * tc, tc)
        return ref[pl.ds(start, tc), :].reshape(groups, 8, tq)

    last = n_chunks - 1
    causal = (qi + 1) * tq - last * tc
    sizes = [tc // 4, tc // 2, 3 * tc // 4, tc]
    ladder = list(zip([0] + sizes[:-1], sizes))

    def indexer_dots(c, n_keys):
        start = pl.multiple_of(c * tc, tc)
        return lax.dot_general(ixk_ref[0, pl.ds(start, n_keys), :], qm_ref[...], NT_DIMS, preferred_element_type=F32)

    def score_chunk(c, n_keys=tc, st=None):
        n_groups = n_keys // 8
        start = pl.multiple_of(c * tc, tc)
        if st is None:
            st = indexer_dots(c, n_keys)
        sc = None
        for h in range(N_HEADS):
            term = jnp.maximum(st[:, head_cols[h]], 0.0) * wt_ref[h:h + 1, :]
            sc = term if sc is None else sc + term
        key = jnp.where(start + k_in_chunk[:n_groups] <= q_lane[:n_groups],
                        _sortable(sc).reshape(n_groups, 8, tq), INT_MIN)
        keys_ref[pl.ds(start, n_keys), :] = key.reshape(n_keys, tq)
        if n_keys < tc:
            keys_ref[pl.ds(start + n_keys, tc - n_keys), :] = jnp.full((tc - n_keys, tq), INT_MIN, I32)

    def score_pair(i, carry):
        dots = [indexer_dots(2 * i + j, tc) for j in range(2)]
        for j in range(2):
            score_chunk(2 * i + j, tc, dots[j])
        return carry

    lax.fori_loop(0, last // 2, score_pair, 0)

    @pl.when(last % 2 == 1)
    def _():
        score_chunk(last - 1)

    for lo, n_keys in ladder:
        @pl.when(jnp.logical_and(causal > lo, causal <= n_keys))
        def _(n_keys=n_keys):
            score_chunk(last, n_keys)

    def count(pred):
        def one(c, cnt):
            return cnt + jnp.sum(pred(chunk3(keys_ref, c), c * tc).astype(I32), axis=0)

        def two(i, cnt):
            return one(2 * i + 1, one(2 * i, cnt))

        cnt = lax.fori_loop(0, n_chunks // 2, two, jnp.zeros((8, tq), I32))
        cnt = lax.fori_loop(2 * (n_chunks // 2), n_chunks, one, cnt)
        return _sublane_all(cnt, jnp.add)

    @pl.when((qi + 1) * tq <= topk)
    def _():
        thr_ref[0] = jnp.full((8, tq), INT_MIN, I32)
        thr_ref[1] = jnp.full((8, tq), -1, I32)

    @pl.when((qi + 1) * tq > topk)
    def _():
        ans = jnp.zeros((8, tq), I32)
        n_ge = jnp.full((8, tq), seq, I32)
        for bit in range(31, -1, -1):
            cand = ans | np.int32(INT_MIN if bit == 31 else 1 << bit)
            cand_s = cand ^ np.int32(INT_MIN)
            c = count(lambda keys, start, cand_s=cand_s: keys >= cand_s[None])
            take = c >= topk
            ans = jnp.where(take, cand, ans)
            n_ge = jnp.where(take, c, n_ge)
        tau = ans ^ np.int32(INT_MIN)
        thr_ref[0] = tau
        thr_ref[1] = jnp.full((8, tq), seq, I32)

        @pl.when(jnp.max(n_ge) > topk)
        def _():
            need = topk - count(lambda keys, start: keys > tau[None])
            cut = jnp.zeros((8, tq), I32)
            for bit in range(int(math.log2(seq)), -1, -1):
                cand = cut | np.int32(1 << bit)
                c = count(lambda keys, start, cand=cand: (keys == tau[None]) & (start + k_in_chunk < cand[None]))
                cut = jnp.where(c < need, cand, cut)
            thr_ref[1] = cut

    tau = thr_ref[0]
    cut = thr_ref[1]

    v_rows = acc_ref.shape[0]
    m_ref[...] = jnp.full(m_ref.shape, NEG_BIG, F32)
    acc_ref[...] = jnp.zeros(acc_ref.shape, F32)
    pq = pos_ref[0, qi]
    pq_min = jnp.min(pq)
    last_bias = [rel_ref[REL_BUCKETS - 1, h] for h in range(N_HEADS)]

    def softmax_head(h, s_t, addend, bias, n_keys):
        cols = head_cols[h]
        sb = s_t.reshape(n_keys // 8, 8, tq) + addend
        m_old = m_ref[:, cols]
        m_new = jnp.maximum(m_old, _sublane_all(jnp.max(sb, axis=0), jnp.maximum) + bias)
        p = jnp.exp(sb - (m_new - bias)[None])
        alpha = jnp.exp(m_old - m_new)
        p_ref[0:n_keys, cols] = p.reshape(n_keys, tq).astype(BF16)
        alpha_ref[:, cols] = alpha
        m_ref[:, cols] = m_new

    def accumulate(c, half, n_keys):
        cols = slice(half * (width // 2), (half + 1) * (width // 2))
        pv = jnp.dot(vvt_ref[0, c, :, 0:n_keys], p_ref[0:n_keys, cols],
                     preferred_element_type=F32)
        scaled = acc_ref[:, cols].reshape(v_rows // 8, 8, width // 2) * alpha_ref[:, cols][None]
        acc_ref[:, cols] = scaled.reshape(v_rows, width // 2) + pv

    def all_heads(c, addend_of, bias_of, n_keys):
        start = pl.multiple_of(c * tc, tc)
        s_all = lax.dot_general(kk_ref[0, pl.ds(start, n_keys), :], qs_ref[...], NT_DIMS,
                                preferred_element_type=F32)
        for half in range(2):
            for h in range(half * N_HEADS // 2, (half + 1) * N_HEADS // 2):
                softmax_head(h, s_all[:, head_cols[h]], addend_of(h), bias_of(h), n_keys)
            accumulate(c, half, n_keys)

    def attend_chunk(c, n_keys=tc):
        n_groups = n_keys // 8
        start = pl.multiple_of(c * tc, tc)
        keys = keys_ref[pl.ds(start, n_keys), :].reshape(n_groups, 8, tq)
        sel = (keys > tau[None]) | ((keys == tau[None]) & (k_in_chunk[:n_groups] <= (cut - c * tc)[None]))
        selb_ref[0:n_keys, :] = jnp.where(sel, 0.0, NEG_BIG).reshape(n_keys, tq)
        first_tile = c * (tc // t)
        pk_max = pos_ref[0, first_tile]
        for j in range(1, n_keys // t):
            pk_max = jnp.maximum(pk_max, pos_ref[0, first_tile + j])
        chunk_far = pq_min - jnp.max(pk_max) >= FAR_DIST

        @pl.when(chunk_far)
        def _():
            all_heads(c, lambda h: selb_ref[0:n_keys, :].reshape(n_groups, 8, tq), lambda h: last_bias[h], n_keys)

        @pl.when(jnp.logical_not(chunk_far))
        def _():
            for j in range(n_keys // t):
                rows = slice(j * t, (j + 1) * t)
                pk = pos_ref[0, first_tile + j]
                tile_far = jnp.logical_or(pq_min - jnp.max(pk) >= FAR_DIST, c * tc + j * t >= (qi + 1) * tq)

                @pl.when(tile_far)
                def _():
                    for h in range(N_HEADS):
                        add_ref[rows, head_cols[h]] = selb_ref[rows, :] + last_bias[h]

                @pl.when(jnp.logical_not(tile_far))
                def _():
                    pk_col = jnp.transpose(jnp.broadcast_to(pk, (t, t)))
                    dist = jnp.clip(pq - pk_col, 0, t - 1)
                    for h in range(N_HEADS):
                        table = jnp.broadcast_to(tab_ref[h:h + 1, :], (t, t))
                        bias = jnp.take_along_axis(table, dist, axis=1, mode="promise_in_bounds")
                        add_ref[rows, head_cols[h]] = selb_ref[rows, :] + bias

            all_heads(c, lambda h: add_ref[0:n_keys, head_cols[h]].reshape(n_groups, 8, tq), lambda h: 0.0, n_keys)

    def chunk_is_far(c):
        pk_max = pos_ref[0, c * (tc // t)]
        for j in range(1, tc // t):
            pk_max = jnp.maximum(pk_max, pos_ref[0, c * (tc // t) + j])
        return pq_min - jnp.max(pk_max) >= FAR_DIST

    def far_pair(a):
        scores = []
        for c, ref in ((a, selb_ref), (a + 1, selb2_ref)):
            start = pl.multiple_of(c * tc, tc)
            scores.append(lax.dot_general(kk_ref[0, pl.ds(start, tc), :], qs_ref[...], NT_DIMS,
                                          preferred_element_type=F32))
            keys = chunk3(keys_ref, c)
            sel = (keys > tau[None]) | ((keys == tau[None]) & (k_in_chunk <= (cut - c * tc)[None]))
            ref[...] = jnp.where(sel, 0.0, NEG_BIG).reshape(tc, tq)
        for (c, ref), s_all in zip(((a, selb_ref), (a + 1, selb2_ref)), scores):
            for half in range(2):
                for h in range(half * N_HEADS // 2, (half + 1) * N_HEADS // 2):
                    softmax_head(h, s_all[:, head_cols[h]], ref[...].reshape(groups, 8, tq), last_bias[h], tc)
                accumulate(c, half, tc)

    def chunk_pair(i, carry):
        a = 2 * i
        both_far = jnp.logical_and(chunk_is_far(a), chunk_is_far(a + 1))

        @pl.when(both_far)
        def _():
            far_pair(a)

        @pl.when(jnp.logical_not(both_far))
        def _():
            attend_chunk(a)
            attend_chunk(a + 1)

        return carry

    lax.fori_loop(0, last // 2, chunk_pair, 0)

    @pl.when(last % 2 == 1)
    def _():
        attend_chunk(last - 1)
    for lo, n_keys in ladder:
        @pl.when(jnp.logical_and(causal > lo, causal <= n_keys))
        def _(n_keys=n_keys):
            attend_chunk(last, n_keys)

    for p in range(N_HEADS // 2):
        outs = []
        for h in (2 * p, 2 * p + 1):
            cols = head_cols[h]
            outs.append(acc_ref[0:HEAD_DIM, cols] / acc_ref[HEAD_DIM:HEAD_DIM + 1, cols])
        o_ref[0, :, p * LANES:(p + 1) * LANES] = jnp.transpose(jnp.concatenate(outs, axis=0))


def _dsa_attention(rel_bias, dist_table, pb, ixw, dq, dkk, vvt, pos_rows, b, s, tq, tc):
    topk = min(TOPK_MAX, s // 4)
    pb3 = pb.reshape(b, s, N_BF)
    width = N_HEADS * tq
    kernel = functools.partial(_dsa_kernel, tq=tq, tc=tc, topk=topk, seq=s)
    return pl.pallas_call(
        kernel,
        grid=(b, s // tq),
        in_specs=[pl.BlockSpec(memory_space=pltpu.SMEM),
                  pl.BlockSpec(dist_table.shape, lambda bi, i: (0, 0)),
                  pl.BlockSpec((1, tq, 2 * LANES), lambda bi, i: (bi, i, B_IXQ // 256)),
                  pl.BlockSpec((1, tq, LANES), lambda bi, i: (bi, i, 0)),
                  pl.BlockSpec((1, s, LANES), lambda bi, i: (bi, 0, B_IXK // LANES)),
                  pl.BlockSpec((1, tq, N_HEADS * LANES), lambda bi, i: (bi, i, 0)),
                  pl.BlockSpec((1, s, LANES), lambda bi, i: (bi, 0, 0)),
                  pl.BlockSpec((1, s // tc, V_ROWS, tc), lambda bi, i: (bi, 0, 0, 0)),
                  pl.BlockSpec((1, s // LANES, 1, LANES), lambda bi, i: (bi, 0, 0, 0))],
        out_specs=pl.BlockSpec((1, tq, BRANCH_WIDTH), lambda bi, i: (bi, i, 0)),
        out_shape=jax.ShapeDtypeStruct((b, s, BRANCH_WIDTH), F32),
        scratch_shapes=[pltpu.VMEM((s, tq), I32),
                        pltpu.VMEM((width, LANES), BF16),
                        pltpu.VMEM((width, LANES), BF16),
                        pltpu.VMEM((LANES, tq), F32),
                        pltpu.VMEM((tc, width), BF16),
                        pltpu.VMEM((tc, tq), F32),
                        pltpu.VMEM((tc, tq), F32),
                        pltpu.VMEM((tc, width), F32),
                        pltpu.VMEM((8, width), F32),
                        pltpu.VMEM((8, width), F32),
                        pltpu.VMEM((V_ROWS, width), F32),
                        pltpu.VMEM((2, 8, tq), I32)],
        compiler_params=pltpu.CompilerParams(dimension_semantics=("parallel", "arbitrary"),
                                             vmem_limit_bytes=VMEM_LIMIT),
        name="dsa_attention",
    )(rel_bias, dist_table, pb3, ixw.reshape(b, s, LANES), pb3, dq.reshape(b, s, -1), dkk.reshape(b, s, -1), vvt, pos_rows)


def _merge_kernel(x_ref, ya_ref, yb_ref, yc_ref, za_ref, zb_ref, zc_ref, ga_ref, gb_ref, gc_ref,
                  bias_ref, wbr_ref, wout_ref, o_ref):
    merged = None
    branches = ((ya_ref, za_ref, ga_ref), (yb_ref, zb_ref, gb_ref), (yc_ref, zc_ref, gc_ref))
    for n, (y_ref, z_ref, g_ref) in enumerate(branches):
        z = z_ref[...].astype(F32)
        branch = (y_ref[...] * (z * jax.nn.sigmoid(z))).astype(BF16)
        up = jnp.dot(branch, wbr_ref[n], preferred_element_type=F32)
        gate = jax.nn.sigmoid(g_ref[...].astype(F32) + bias_ref[n:n + 1, :])
        merged = gate * up if merged is None else merged + gate * up
    o_ref[...] = x_ref[...] + jnp.dot(merged.astype(BF16), wout_ref[...], preferred_element_type=F32)


def _merge(x2, ya, yb, yc, pb, gate_b, wbr, wout, tm):
    m, d = x2.shape
    row = lambda width: pl.BlockSpec((tm, width), lambda i: (i, 0))
    col = lambda width, off: pl.BlockSpec((tm, width), lambda i, _b=off // width: (i, _b))
    z_specs = [col(BRANCH_WIDTH, B_Z + n * BRANCH_WIDTH) for n in range(N_BRANCH)]
    g_specs = [col(d, B_G + n * d) for n in range(N_BRANCH)]
    return pl.pallas_call(
        _merge_kernel,
        grid=(m // tm,),
        in_specs=[row(d), row(BRANCH_WIDTH), row(BRANCH_WIDTH), row(BRANCH_WIDTH)] + z_specs + g_specs + [
            pl.BlockSpec(gate_b.shape, lambda i: (0, 0)),
            pl.BlockSpec(wbr.shape, lambda i: (0, 0, 0)),
            pl.BlockSpec(wout.shape, lambda i: (0, 0))],
        out_specs=row(d),
        out_shape=jax.ShapeDtypeStruct((m, d), F32),
        compiler_params=pltpu.CompilerParams(dimension_semantics=("parallel",), vmem_limit_bytes=VMEM_LIMIT),
        name="merge",
    )(x2, ya, yb, yc, pb, pb, pb, pb, pb, pb, gate_b, wbr, wout)


def _lane_table(vals_by_lane):
    t = jnp.zeros((LANES,), F32)
    for start, v in vals_by_lane.items():
        t = t.at[start:start + v.shape[0]].set(v)
    return t[None, :]


def _layer_weights(l, w_all, norm_g, mla_q_norm_g, mla_kv_norm_g, mla_w_uq, mla_w_ukv, mla_q_g, mla_k_g,
                   dsa_q_g, dsa_k_g, gate_b, w_branch, w_out):
    uq, ukv = mla_w_uq[l], mla_w_ukv[l]
    half = MLA_ROPE // 2
    q_pieces, qr_pieces, k_pieces, v_pieces = [], [], [], []
    for h in range(N_HEADS):
        o = h * MLA_QK
        q_pieces += [(o, MLA_QK, 1.0), (None, LANES - MLA_QK, 1.0)]
        qr_pieces += [(None, MLA_NOPE, 1.0), (o + MLA_NOPE + half, half, 1.0), (o + MLA_NOPE, half, 1.0),
                      (None, LANES - MLA_QK, 1.0)]
        k_pieces += [(h * 128, MLA_NOPE, 1.0), (None, LANES - MLA_NOPE, 1.0)]
        v_pieces += [(h * 128 + MLA_NOPE, 64, 1.0)]
    scale = MLA_QK ** -0.5
    qg, kg = mla_q_g[l], mla_k_g[l]
    rot_gain = lambda g: jnp.concatenate([g[MLA_NOPE + half:], g[MLA_NOPE:MLA_NOPE + half]])
    dq_g = jnp.concatenate([jnp.concatenate([dsa_q_g[l], jnp.zeros((HEAD_DIM,), F32)]) if h % 2 == 0 else
                            jnp.concatenate([jnp.zeros((HEAD_DIM,), F32), dsa_q_g[l]]) for h in range(N_HEADS)])
    return dict(
        norm_g=norm_g[l][None, :], w_all=w_all[l],
        wuq=_assemble(uq, q_pieces).astype(BF16), wuqr=_assemble(uq, qr_pieces).astype(BF16),
        wuk=_assemble(ukv, k_pieces).astype(BF16), wuv=_assemble(ukv, v_pieces).astype(BF16),
        qng=mla_q_norm_g[l][None, :], kvng=mla_kv_norm_g[l][None, :],
        gq=_lane_table({0: qg * scale}), gqr=_lane_table({MLA_NOPE: rot_gain(qg) * scale}),
        gk=_lane_table({0: kg}), gkr=_lane_table({MLA_NOPE: rot_gain(kg)}),
        gdq=(dq_g * HEAD_DIM ** -0.5)[None, :], gdk=jnp.concatenate([dsa_k_g[l], dsa_k_g[l]])[None, :],
        gate_b=gate_b[l], wbr=w_branch[l].astype(BF16), wout=w_out[l].astype(BF16))


def kernel(x, positions, norm_g, w_in, mla_q_norm_g, mla_kv_norm_g, mla_w_uq, mla_w_ukv, mla_q_g, mla_k_g,
           dsa_q_g, dsa_k_g, rel_bias, gate_b, w_branch, w_out):
    b, s, d = x.shape
    depth = w_in.shape[0]
    m = b * s
    assert d == D_MODEL and w_in.shape[-1] == D_IN and s % 512 == 0 and s & (s - 1) == 0

    half = MLA_ROPE // 2
    inv_freq = ROPE_THETA ** (-(jnp.arange(half, dtype=F32) * 2.0) / MLA_ROPE)
    ang = positions.astype(F32)[..., None] * inv_freq
    cos, sin = jnp.cos(ang).reshape(m, half), jnp.sin(ang).reshape(m, half)
    cos_t = jnp.concatenate([jnp.ones((m, MLA_NOPE), F32), cos, cos, jnp.ones((m, LANES - MLA_QK), F32)], axis=1)
    sin_t = jnp.concatenate([jnp.zeros((m, MLA_NOPE), F32), -sin, sin, jnp.zeros((m, LANES - MLA_QK), F32)], axis=1)

    bf_pieces, f32_pieces = _in_proj_layouts()
    w_in16 = w_in.astype(BF16)
    w_all = _assemble(w_in16, f32_pieces + bf_pieces)

    tm = min(1024, m)
    mla_tc = min(1024, s)
    dsa_tc = 512
    tp = min(512, m)
    pos_rows = positions.reshape(b, s // LANES, 1, LANES)
    dist_table = rel_bias[BUCKET_OF_DIST, :].T
    x2 = x.reshape(m, d)
    for l in range(depth):
        lw = _layer_weights(l, w_all, norm_g, mla_q_norm_g, mla_kv_norm_g, mla_w_uq, mla_w_ukv,
                            mla_q_g, mla_k_g, dsa_q_g, dsa_k_g, gate_b, w_branch, w_out)
        pb = _in_proj(x2, lw["norm_g"], lw["w_all"], N_F32, N_BF, BF16, min(2 * tm, m), 1024)
        qb, kb, vbt, dq, dkk, ixw, vvt = _prologue(x2, cos_t, sin_t, lw, mla_tc, dsa_tc)
        ya = _sb_attention(pb, b, s, 128)
        vbt = vbt.reshape(b, s // mla_tc, N_HEADS // 2, LANES + 8, mla_tc)
        yb = _mla_attention(qb, kb, vbt, b, s, LANES, mla_tc)
        vvt = vvt.reshape(b, s // dsa_tc, V_ROWS, dsa_tc)
        yc = _dsa_attention(rel_bias, dist_table, pb, ixw, dq, dkk, vvt, pos_rows, b, s, LANES, dsa_tc)
        x2 = _merge(x2, ya.reshape(m, -1), yb.reshape(m, -1), yc.reshape(m, -1), pb, lw["gate_b"],
                    lw["wbr"], lw["wout"], tp)
    return x2.reshape(b, s, d)
```
